```python
import jax, jax.numpy as jnp
from jax import lax
import numpy as np

D_MODEL = 1024
BATCH = 4
SEQ = 8192
DEPTH = 1
DEC_BATCH = 8
DEC_SEQ = 16
PAST_LEN = 2048

CHUNK = 64
M_HEADS = 4
M_HEAD_DIM = 128
M_WIDTH = M_HEADS * M_HEAD_DIM
A_Q_HEADS = 8
A_KV_HEADS = 2
A_HEAD_DIM = 64
A_GROUP = A_Q_HEADS // A_KV_HEADS
A_WIDTH = A_Q_HEADS * A_HEAD_DIM
KV_WIDTH = A_KV_HEADS * A_HEAD_DIM
MIX_WIDTH = M_WIDTH + A_WIDTH
IN_WIDTH = 4 * M_WIDTH + 2 * M_HEADS + A_WIDTH + 2 * KV_WIDTH
WINDOW = 128
WIN_CHUNKS = WINDOW // CHUNK
ROPE_THETA = 10000.0
D_FF = 2816
UP_WIDTH = 2 * D_FF
CONV_W = 3
LN_EPS = 1e-5
DEEPNORM_ALPHA = (2 * DEPTH) ** 0.25
DEEPNORM_BETA = (8 * DEPTH) ** -0.25

kernel_name = 'hybrid_mlstm_swa_convffn_stream_step'


def layer_norm(x, g, b):
    xf = x.astype(jnp.float32)
    mu = jnp.mean(xf, axis=-1, keepdims=True)
    var = jnp.mean(jnp.square(xf - mu), axis=-1, keepdims=True)
    y = (xf - mu) * lax.rsqrt(var + LN_EPS) * g.astype(jnp.float32) + b.astype(jnp.float32)
    return y.astype(x.dtype)


def head_norm(h, g):
    mu = jnp.mean(h, axis=-1, keepdims=True)
    var = jnp.mean(jnp.square(h - mu), axis=-1, keepdims=True)
    return (h - mu) * lax.rsqrt(var + LN_EPS) * g.astype(jnp.float32).reshape(M_HEADS, M_HEAD_DIM)


def rope(x, pos):
    half = x.shape[-1] // 2
    inv = ROPE_THETA ** (-jnp.arange(half, dtype=jnp.float32) / half)
    ang = pos.astype(jnp.float32)[:, None] * inv[None, :]
    cos = jnp.cos(ang)[:, None, :]
    sin = jnp.sin(ang)[:, None, :]
    xf = x.astype(jnp.float32)
    x1, x2 = xf[..., :half], xf[..., half:]
    return jnp.concatenate([x1 * cos - x2 * sin, x2 * cos + x1 * sin], axis=-1).astype(x.dtype)


def split_in(z):
    sizes = (M_WIDTH, M_WIDTH, M_WIDTH, M_WIDTH, M_HEADS, M_HEADS, A_WIDTH, KV_WIDTH, KV_WIDTH)
    idx = []
    acc = 0
    for s in sizes[:-1]:
        acc += s
        idx.append(acc)
    return jnp.split(z, idx, axis=-1)


def mlstm_chunkwise(q, k, v, logi, logf, C0, n0, m0, chunk):
    B, T, H, D = q.shape
    nc = T // chunk

    def vec_chunks(a):
        return a.astype(jnp.float32).reshape(B, nc, chunk, H, D).transpose(1, 0, 3, 2, 4)

    def gate_chunks(a):
        return a.reshape(B, nc, chunk, H).transpose(1, 0, 3, 2)

    causal = jnp.tril(jnp.ones((chunk, chunk), dtype=bool))

    def step(carry, inp):
        C, n, m = carry
        qc, kc, vc, li, lf = inp
        b = jnp.cumsum(lf, axis=-1)
        inter = b + m[..., None]
        dmat = jnp.where(causal, b[..., :, None] - b[..., None, :] + li[..., None, :], -jnp.inf)
        mt = jnp.maximum(inter, jnp.max(dmat, axis=-1))
        w_inter = jnp.exp(inter - mt)
        s = jnp.einsum('bhld,bhsd->bhls', qc, kc) * jnp.exp(dmat - mt[..., None])
        num = w_inter[..., None] * jnp.einsum('bhld,bhde->bhle', qc, C) + jnp.einsum('bhls,bhse->bhle', s, vc)
        den = w_inter * jnp.einsum('bhld,bhd->bhl', qc, n) + jnp.sum(s, axis=-1)
        h = num / jnp.maximum(jnp.abs(den), jnp.exp(-mt))[..., None]
        b_last = b[..., -1]
        g = b_last[..., None] - b + li
        m_new = jnp.maximum(b_last + m, jnp.max(g, axis=-1))
        decay = jnp.exp(b_last + m - m_new)
        wk = jnp.exp(g - m_new[..., None])
        C_new = decay[..., None, None] * C + jnp.einsum('bhl,bhld,bhle->bhde', wk, kc, vc)
        n_new = decay[..., None] * n + jnp.einsum('bhl,bhld->bhd', wk, kc)
        return (C_new, n_new, m_new), h

    carry0 = (C0.astype(jnp.float32), n0.astype(jnp.float32), m0.astype(jnp.float32))
    (C1, n1, m1), hs = lax.scan(step, carry0, (vec_chunks(q), vec_chunks(k), vec_chunks(v), gate_chunks(logi), gate_chunks(logf)))
    h = hs.transpose(1, 0, 3, 2, 4).reshape(B, T, H, D)
    return h, C1, n1, m1


def band_attend(q, k, v, mask, sinks):
    s = jnp.einsum('bnlkgd,bnskd->bnkgls', q, k, preferred_element_type=jnp.float32) * (A_HEAD_DIM ** -0.5)
    if mask is not None:
        s = jnp.where(mask, s, -jnp.inf)
    sink = sinks.astype(jnp.float32).reshape(A_KV_HEADS, A_GROUP)[None, None, :, :, None, None]
    mx = jnp.maximum(jnp.max(s, axis=-1, keepdims=True), sink)
    p = jnp.exp(s - mx)
    p = p / (jnp.sum(p, axis=-1, keepdims=True) + jnp.exp(sink - mx))
    o = jnp.einsum('bnkgls,bnskd->bnlkgd', p.astype(v.dtype), v, preferred_element_type=jnp.float32)
    return o.astype(v.dtype)


def swa_prompt(q, k, v, sinks):
    B, T = q.shape[0], q.shape[1]
    nc = T // CHUNK
    qc = q.reshape(B, nc, CHUNK, A_KV_HEADS, A_GROUP, A_HEAD_DIM)
    pad = ((0, 0), (WIN_CHUNKS * CHUNK, 0), (0, 0), (0, 0))
    kp = jnp.pad(k, pad).reshape(B, nc + WIN_CHUNKS, CHUNK, A_KV_HEADS, A_HEAD_DIM)
    vp = jnp.pad(v, pad).reshape(B, nc + WIN_CHUNKS, CHUNK, A_KV_HEADS, A_HEAD_DIM)
    kb = jnp.concatenate([kp[:, j:j + nc] for j in range(WIN_CHUNKS + 1)], axis=2)
    vb = jnp.concatenate([vp[:, j:j + nc] for j in range(WIN_CHUNKS + 1)], axis=2)
    band = (WIN_CHUNKS + 1) * CHUNK
    valid = (jnp.arange(nc)[:, None] + jnp.arange(band)[None, :] // CHUNK - WIN_CHUNKS) >= 0
    o = band_attend(qc, kb, vb, valid[None, :, None, None, None, :], sinks)
    return o.reshape(B, T, A_WIDTH)


def swa_step(q, k, v, k_cache, v_cache, sinks):
    B, T = q.shape[0], q.shape[1]
    qc = q.reshape(B, 1, T, A_KV_HEADS, A_GROUP, A_HEAD_DIM)
    kb = jnp.concatenate([k_cache.astype(k.dtype), k], axis=1)[:, None]
    vb = jnp.concatenate([v_cache.astype(v.dtype), v], axis=1)[:, None]
    o = band_attend(qc, kb, vb, None, sinks)
    return o.reshape(B, T, A_WIDTH)


def conv_ffn(h, conv_prev, w_up, w_conv, b_conv, w_down):
    T = h.shape[1]
    u = h @ w_up
    ext = jnp.concatenate([conv_prev.astype(u.dtype), u], axis=1)
    c = b_conv + sum(ext[:, j:j + T] * w_conv[j] for j in range(CONV_W))
    a, g = jnp.split(c, 2, axis=-1)
    return (jax.nn.silu(g) * a) @ w_down, ext[:, -(CONV_W - 1):]


def trunk_layer(x, pos, C0, n0, m0, k_cache, v_cache, conv_prev, mchunk,
                w_in, b_igate, b_fgate, g_mlstm_norm, attn_sinks, w_out, ln1_g, ln1_b,
                w_up, w_conv, b_conv, w_down, ln2_g, ln2_b):
    B, T, _ = x.shape
    qm, km, vm, om, ip, fp, qa, ka, va = split_in(x @ w_in)
    qm = qm.reshape(B, T, M_HEADS, M_HEAD_DIM)
    km = km.reshape(B, T, M_HEADS, M_HEAD_DIM) * (M_HEAD_DIM ** -0.5)
    vm = vm.reshape(B, T, M_HEADS, M_HEAD_DIM)
    logi = ip.astype(jnp.float32) + b_igate.astype(jnp.float32)
    logf = jax.nn.log_sigmoid(fp.astype(jnp.float32) + b_fgate.astype(jnp.float32))
    hm, C1, n1, m1 = mlstm_chunkwise(qm, km, vm, logi, logf, C0, n0, m0, mchunk)
    og = jax.nn.sigmoid(om.astype(jnp.float32)).reshape(B, T, M_HEADS, M_HEAD_DIM)
    hm = (og * head_norm(hm, g_mlstm_norm)).reshape(B, T, M_WIDTH).astype(x.dtype)
    qa = rope(qa.reshape(B, T, A_Q_HEADS, A_HEAD_DIM), pos)
    ka = rope(ka.reshape(B, T, A_KV_HEADS, A_HEAD_DIM), pos)
    va = va.reshape(B, T, A_KV_HEADS, A_HEAD_DIM)
    if k_cache is None:
        ha = swa_prompt(qa, ka, va, attn_sinks)
        k_rows, v_rows = ka[:, -WINDOW:], va[:, -WINDOW:]
    else:
        ha = swa_step(qa, ka, va, k_cache, v_cache, attn_sinks)
        k_rows, v_rows = ka, va
    mix = jnp.concatenate([hm, ha], axis=-1) @ w_out
    h = layer_norm(DEEPNORM_ALPHA * x + mix, ln1_g, ln1_b)
    f, conv_rows = conv_ffn(h, conv_prev, w_up, w_conv, b_conv, w_down)
    y = layer_norm(DEEPNORM_ALPHA * h + f, ln2_g, ln2_b)
    return y, (C1, n1, m1, k_rows, v_rows, conv_rows)


def setup_inputs(seed: int = 0) -> dict:
    key = jax.random.key(seed)
    ks = jax.random.split(key, 24)

    def nrm(k, shape, scale):
        return jax.random.normal(k, shape, jnp.float32) * scale

    return {
        'x_prompt': nrm(ks[0], (BATCH, SEQ, D_MODEL), 1.0),
        'x_sample': nrm(ks[1], (DEC_BATCH, DEC_SEQ, D_MODEL), 1.0),
        'state_mlstm_C': nrm(ks[2], (DEPTH, DEC_BATCH, M_HEADS, M_HEAD_DIM, M_HEAD_DIM), 0.3),
        'state_mlstm_n': nrm(ks[3], (DEPTH, DEC_BATCH, M_HEADS, M_HEAD_DIM), 0.3),
        'state_mlstm_m': nrm(ks[4], (DEPTH, DEC_BATCH, M_HEADS), 1.0),
        'cache_swa_k': nrm(ks[5], (DEPTH, DEC_BATCH, WINDOW, A_KV_HEADS, A_HEAD_DIM), 1.0),
        'cache_swa_v': nrm(ks[6], (DEPTH, DEC_BATCH, WINDOW, A_KV_HEADS, A_HEAD_DIM), 1.0),
        'state_conv': nrm(ks[7], (DEPTH, DEC_BATCH, CONV_W - 1, UP_WIDTH), 1.0),
        'w_in': nrm(ks[8], (DEPTH, D_MODEL, IN_WIDTH), D_MODEL ** -0.5),
        'b_igate': nrm(ks[9], (DEPTH, M_HEADS), 0.1),
        'b_fgate': jnp.linspace(3.0, 6.0, M_HEADS, dtype=jnp.float32)[None, :] + nrm(ks[10], (DEPTH, M_HEADS), 0.1),
        'g_mlstm_norm': 1.0 + nrm(ks[11], (DEPTH, M_WIDTH), 0.02),
        'attn_sinks': nrm(ks[12], (DEPTH, A_Q_HEADS), 0.5),
        'w_out': nrm(ks[13], (DEPTH, MIX_WIDTH, D_MODEL), MIX_WIDTH ** -0.5 * DEEPNORM_BETA),
        'ln1_g': 1.0 + nrm(ks[14], (DEPTH, D_MODEL), 0.02),
        'ln1_b': nrm(ks[15], (DEPTH, D_MODEL), 0.02),
        'w_up': nrm(ks[16], (DEPTH, D_MODEL, UP_WIDTH), D_MODEL ** -0.5),
        'w_conv': nrm(ks[17], (DEPTH, CONV_W, UP_WIDTH), CONV_W ** -0.5),
        'b_conv': nrm(ks[18], (DEPTH, UP_WIDTH), 0.02),
        'w_down': nrm(ks[19], (DEPTH, D_FF, D_MODEL), D_FF ** -0.5 * DEEPNORM_BETA),
        'ln2_g': 1.0 + nrm(ks[20], (DEPTH, D_MODEL), 0.02),
        'ln2_b': nrm(ks[21], (DEPTH, D_MODEL), 0.02),
    }


def reference(x_prompt, x_sample, state_mlstm_C, state_mlstm_n, state_mlstm_m, cache_swa_k, cache_swa_v,
              state_conv, w_in, b_igate, b_fgate, g_mlstm_norm, attn_sinks, w_out, ln1_g, ln1_b,
              w_up, w_conv, b_conv, w_down, ln2_g, ln2_b):
    bp, tp = x_prompt.shape[0], x_prompt.shape[1]
    bs, ts = x_sample.shape[0], x_sample.shape[1]
    pos_p = jnp.arange(tp, dtype=jnp.int32)
    pos_s = PAST_LEN + jnp.arange(ts, dtype=jnp.int32)
    hp, hs = x_prompt, x_sample
    sp, ss = [], []
    for l in range(DEPTH):
        w = (w_in[l], b_igate[l], b_fgate[l], g_mlstm_norm[l], attn_sinks[l], w_out[l], ln1_g[l], ln1_b[l],
             w_up[l], w_conv[l], b_conv[l], w_down[l], ln2_g[l], ln2_b[l])
        hp, st_p = trunk_layer(hp, pos_p,
                               jnp.zeros((bp, M_HEADS, M_HEAD_DIM, M_HEAD_DIM), jnp.float32),
                               jnp.zeros((bp, M_HEADS, M_HEAD_DIM), jnp.float32),
                               jnp.zeros((bp, M_HEADS), jnp.float32),
                               None, None,
                               jnp.zeros((bp, CONV_W - 1, UP_WIDTH), hp.dtype),
                               CHUNK, *w)
        hs, st_s = trunk_layer(hs, pos_s, state_mlstm_C[l], state_mlstm_n[l], state_mlstm_m[l],
                               cache_swa_k[l], cache_swa_v[l], state_conv[l], ts, *w)
        sp.append(st_p)
        ss.append(st_s)
    P = [jnp.stack([s[i] for s in sp]) for i in range(6)]
    S = [jnp.stack([s[i] for s in ss]) for i in range(6)]
    return (hp, hs, P[0], P[1], P[2], P[3], P[4], P[5], S[0], S[1], S[2], S[3], S[4], S[5])
```

```python
import functools
import math

import jax
import jax.numpy as jnp
from jax import lax
from jax.experimental import pallas as pl
from jax.experimental.pallas import tpu as pltpu

D_MODEL = 1024
M_HEADS = 4
M_HEAD_DIM = 128
M_WIDTH = M_HEADS * M_HEAD_DIM
A_Q_HEADS = 8
A_KV_HEADS = 2
A_HEAD_DIM = 64
A_GROUP = A_Q_HEADS // A_KV_HEADS
A_WIDTH = A_Q_HEADS * A_HEAD_DIM
KV_WIDTH = A_KV_HEADS * A_HEAD_DIM
WINDOW = 128
CHUNK = 64
ROPE_THETA = 10000.0
D_FF = 2816
UP_WIDTH = 2 * D_FF
CONV_W = 3
LN_EPS = 1e-5
PAST_LEN = 2048

LANES = 128
SUBLANES = 8
GATE_LANE = M_HEADS
MAIN_WIDTH = 4 * M_WIDTH + A_WIDTH + 2 * KV_WIDTH
FF_BLOCK = 256
VMEM_LIMIT_BYTES = 56 * 1024 * 1024

F32 = jnp.float32
BF16 = jnp.bfloat16
NEG_INF = float("-inf")


def _layer_norm(y, g, b):
    mu = jnp.mean(y, axis=-1, keepdims=True)
    d = y - mu
    var = jnp.mean(d * d, axis=-1, keepdims=True)
    return d * lax.rsqrt(var + LN_EPS) * g + b


def _log_sigmoid(x):
    return jnp.minimum(x, 0.0) - jnp.log1p(jnp.exp(-jnp.abs(x)))


def _group_scan(x, group, reverse):
    rows = x.shape[0]
    pos = lax.broadcasted_iota(jnp.int32, x.shape, 0) % group
    acc = x
    k = 1
    while k < group:
        if reverse:
            shifted = pltpu.roll(acc, rows - k, 0)
            ok = pos < group - k
        else:
            shifted = pltpu.roll(acc, k, 0)
            ok = pos >= k
        acc = acc + jnp.where(ok, shifted, 0.0)
        k *= 2
    if reverse:
        acc = acc - x
    return acc


def _rope(x, cos, sin_signed):
    lane = lax.broadcasted_iota(jnp.int32, x.shape, 1) % A_HEAD_DIM
    partner = jnp.where(lane < A_HEAD_DIM // 2,
                        pltpu.roll(x, LANES - A_HEAD_DIM // 2, 1),
                        pltpu.roll(x, A_HEAD_DIM // 2, 1))
    return x * cos + partner * sin_signed


def _mixer_body(sinks_ref, x_ref, cos_ref, sin_ref, wmain_ref, wgate_ref, gbias_ref, gnorm_ref, wout_ref,
                ln1g_ref, ln1b_ref, c0_ref, n0_ref, m0_ref, kc_ref, vc_ref,
                h_ref, cout_ref, nout_ref, mout_ref, kout_ref, vout_ref,
                q_sc, k_sc, v_sc, o_sc, qa_sc, ka_sc, va_sc, b_sc, g_sc, rt_sc, mix_sc, c_sc, n_sc, m_sc,
                *, ch, tm, nt, has_cache, rows_out, alpha):
    t = pl.program_id(1)
    nch = tm // ch
    band = WINDOW + ch

    @pl.when(t == 0)
    def _load_state():
        c_sc[...] = c0_ref[0]
        n_sc[...] = n0_ref[0]
        m_sc[...] = m0_ref[0]
        kc = kc_ref[0].astype(BF16)
        vc = vc_ref[0].astype(BF16)
        for j in range(A_KV_HEADS):
            ka_sc[j, 0:WINDOW, :] = kc[:, j * A_HEAD_DIM:(j + 1) * A_HEAD_DIM]
            va_sc[j, 0:WINDOW, :] = vc[:, j * A_HEAD_DIM:(j + 1) * A_HEAD_DIM]

    if nt > 1:
        @pl.when(t > 0)
        def _slide_window():
            for j in range(A_KV_HEADS):
                ka_sc[j, 0:WINDOW, :] = ka_sc[j, tm:tm + WINDOW, :]
                va_sc[j, 0:WINDOW, :] = va_sc[j, tm:tm + WINDOW, :]

    xb = x_ref[0].astype(BF16)

    def proj(lo, hi):
        return jnp.dot(xb, wmain_ref[:, lo:hi], preferred_element_type=F32)

    q_sc[...] = proj(0, M_WIDTH)
    k_sc[...] = proj(M_WIDTH, 2 * M_WIDTH) * (M_HEAD_DIM ** -0.5)
    v_sc[...] = proj(2 * M_WIDTH, 3 * M_WIDTH).astype(BF16)
    o_sc[...] = jax.nn.sigmoid(proj(3 * M_WIDTH, 4 * M_WIDTH))

    cos = cos_ref[...]
    sin_signed = sin_ref[...]
    base = 4 * M_WIDTH
    for p in range(A_Q_HEADS // 2):
        slab = _rope(proj(base + p * LANES, base + (p + 1) * LANES), cos, sin_signed)
        slab = (slab * (A_HEAD_DIM ** -0.5)).astype(BF16)
        qa_sc[2 * p] = slab[:, 0:A_HEAD_DIM]
        qa_sc[2 * p + 1] = slab[:, A_HEAD_DIM:LANES]
    base += A_WIDTH
    ka = _rope(proj(base, base + KV_WIDTH), cos, sin_signed)
    va = proj(base + KV_WIDTH, base + 2 * KV_WIDTH)
    kab = ka.astype(BF16)
    vab = va.astype(BF16)
    for j in range(A_KV_HEADS):
        ka_sc[j, WINDOW:WINDOW + tm, :] = kab[:, j * A_HEAD_DIM:(j + 1) * A_HEAD_DIM]
        va_sc[j, WINDOW:WINDOW + tm, :] = vab[:, j * A_HEAD_DIM:(j + 1) * A_HEAD_DIM]
    kout_ref[0] = ka[tm - rows_out:tm, :]
    vout_ref[0] = va[tm - rows_out:tm, :]

    gates = jnp.dot(xb, wgate_ref[...], preferred_element_type=F32) + gbias_ref[...]
    log_i = pltpu.roll(gates, GATE_LANE, 1)
    log_f = _log_sigmoid(gates)
    b_incl = _group_scan(log_f, ch, reverse=False)
    b_tail = _group_scan(log_f, ch, reverse=True)
    b_sc[...] = b_incl
    g_sc[...] = b_tail + log_i
    r_t = jnp.transpose(log_i - b_incl)
    for c in range(nch):
        rt_sc[c] = r_t[0:SUBLANES, c * ch:(c + 1) * ch]

    causal = (lax.broadcasted_iota(jnp.int32, (ch, ch), 0) >= lax.broadcasted_iota(jnp.int32, (ch, ch), 1))
    band_pos = lax.broadcasted_iota(jnp.int32, (1, band), 1)
    sink_cols = []
    for j in range(A_KV_HEADS):
        sink_cols.append(jnp.concatenate(
            [jnp.full((ch, 1), sinks_ref[A_GROUP * j + i], F32) for i in range(A_GROUP)], axis=0))
    gnorm = gnorm_ref[...]

    def chunk_step(c, carry):
        r0 = pl.multiple_of(c * ch, ch)
        rows = pl.ds(r0, ch)
        b_tile = b_sc[rows, :]
        g_tile = g_sc[rows, :]
        r_rows = rt_sc[c]

        for hd in range(M_HEADS):
            lane = GATE_LANE + hd
            cols = slice(hd * M_HEAD_DIM, (hd + 1) * M_HEAD_DIM)
            b_col = b_tile[:, lane:lane + 1]
            g_col = g_tile[:, lane:lane + 1]
            r_row = r_rows[lane:lane + 1, :]
            b_last = b_col[ch - 1:ch, :]
            m_old = m_sc[hd][:, 0:1]
            n_old = n_sc[hd]
            c_old = c_sc[hd]

            dmat = jnp.where(causal, b_col + r_row, NEG_INF)
            inter = b_col + m_old
            mt = jnp.maximum(inter, jnp.max(dmat, axis=-1, keepdims=True))
            w_inter = jnp.exp(inter - mt)
            qf = q_sc[rows, cols]
            kf = k_sc[rows, cols]
            qb = qf.astype(BF16)
            vb = v_sc[rows, cols]
            qk = lax.dot_general(qb, kf.astype(BF16), (((1,), (1,)), ((), ())), preferred_element_type=F32)
            s = qk * jnp.exp(dmat - mt)
            num = (w_inter * jnp.dot(qb, c_old.astype(BF16), preferred_element_type=F32)
                   + jnp.dot(s.astype(BF16), vb, preferred_element_type=F32))
            den = (w_inter * jnp.sum(qf * n_old, axis=-1, keepdims=True)
                   + jnp.sum(s, axis=-1, keepdims=True))
            hh = num / jnp.maximum(jnp.abs(den), jnp.exp(-mt))
            mu = jnp.mean(hh, axis=-1, keepdims=True)
            dlt = hh - mu
            var = jnp.mean(dlt * dlt, axis=-1, keepdims=True)
            hn = dlt * lax.rsqrt(var + LN_EPS) * gnorm[:, cols]
            mix_sc[rows, cols] = (o_sc[rows, cols] * hn).astype(BF16)

            m_new = jnp.maximum(b_last + m_old, jnp.max(g_col, axis=0, keepdims=True))
            decay = jnp.exp(b_last + m_old - m_new)
            kw = kf * jnp.exp(g_col - m_new)
            upd = lax.dot_general(kw.astype(BF16), vb, (((0,), (0,)), ((), ())), preferred_element_type=F32)
            c_sc[hd] = decay * c_old + upd
            n_sc[hd] = decay * n_old + jnp.sum(kw, axis=0, keepdims=True)
            m_sc[hd] = jnp.broadcast_to(m_new, (1, LANES))

        if not has_cache:
            first_valid = WINDOW - (t * nch + c) * ch
            mask_bias = jnp.where(band_pos >= first_valid, 0.0, NEG_INF)
        outs = []
        for j in range(A_KV_HEADS):
            qg = jnp.concatenate([qa_sc[A_GROUP * j + i, rows, :] for i in range(A_GROUP)], axis=0)
            kb = ka_sc[j, pl.ds(r0, band), :]
            vband = va_sc[j, pl.ds(r0, band), :]
            sc = lax.dot_general(qg, kb, (((1,), (1,)), ((), ())), preferred_element_type=F32)
            if not has_cache:
                sc = sc + mask_bias
            mx = jnp.maximum(jnp.max(sc, axis=-1, keepdims=True), sink_cols[j])
            pr = jnp.exp(sc - mx)
            dn = jnp.sum(pr, axis=-1, keepdims=True) + jnp.exp(sink_cols[j] - mx)
            ov = jnp.dot(pr.astype(BF16), vband, preferred_element_type=F32) / dn
            for i in range(A_GROUP):
                outs.append(ov[i * ch:(i + 1) * ch, :])
        mix_sc[rows, M_WIDTH:M_WIDTH + A_WIDTH] = jnp.concatenate(outs, axis=-1).astype(BF16)
        return carry

    lax.fori_loop(0, nch, chunk_step, 0)

    mixed = jnp.dot(mix_sc[...], wout_ref[...], preferred_element_type=F32)
    h_ref[0] = _layer_norm(alpha * x_ref[0] + mixed, ln1g_ref[...], ln1b_ref[...])

    @pl.when(t == nt - 1)
    def _store_state():
        cout_ref[0] = c_sc[...]
        nout_ref[0] = n_sc[...]
        mout_ref[0] = m_sc[...]


def _ffn_body(h_ref, prev_ref, wup_ref, wconv_ref, bconv_ref, wdown_ref, ln2g_ref, ln2b_ref,
              y_ref, rows_ref, carry_sc, acc_sc, *, tm, nt, alpha):
    t = pl.program_id(1)

    @pl.when(t == 0)
    def _load_carry():
        carry_sc[...] = prev_ref[0]

    hf = h_ref[0]
    hb = hf.astype(BF16)
    acc_sc[...] = jnp.zeros_like(acc_sc)

    def conv_branch(lo):
        cols = slice(lo, lo + FF_BLOCK)
        u = jnp.dot(hb, wup_ref[:, cols], preferred_element_type=F32)
        ext = jnp.concatenate([carry_sc[:, cols], u], axis=0)
        carry_sc[:, cols] = u[tm - SUBLANES:tm, :]
        rows = tm + SUBLANES
        back1 = pltpu.roll(ext, 1, 0)[SUBLANES:rows, :]
        back2 = pltpu.roll(ext, 2, 0)[SUBLANES:rows, :]
        w = wconv_ref[:, cols]
        return bconv_ref[:, cols] + back2 * w[0:1, :] + back1 * w[1:2, :] + u * w[2:3, :]

    for blk in range(D_FF // FF_BLOCK):
        val = conv_branch(blk * FF_BLOCK)
        gate = conv_branch(D_FF + blk * FF_BLOCK)
        act = (gate * jax.nn.sigmoid(gate) * val).astype(BF16)
        acc_sc[...] += jnp.dot(act, wdown_ref[blk * FF_BLOCK:(blk + 1) * FF_BLOCK, :],
                               preferred_element_type=F32)

    y_ref[0] = _layer_norm(alpha * hf + acc_sc[...], ln2g_ref[...], ln2b_ref[...])

    @pl.when(t == nt - 1)
    def _store_rows():
        rows_ref[0] = carry_sc[...]


def _const_spec(shape):
    zeros = (0,) * len(shape)
    return pl.BlockSpec(shape, lambda b, t: zeros, pipeline_mode=pl.Buffered(1))


def _batch_spec(shape):
    zeros = (0,) * (len(shape) - 1)
    return pl.BlockSpec(shape, lambda b, t: (b,) + zeros)


def _mixer_call(x, pos, c0, n0, m0, kc, vc, wts, *, ch, tm, has_cache, alpha):
    bsz, seq, _ = x.shape
    nt = seq // tm
    nch = tm // ch
    rows_out = min(WINDOW, seq)
    assert seq % tm == 0 and tm % ch == 0 and rows_out <= tm and (nt == 1 or tm >= WINDOW)

    half = A_HEAD_DIM // 2
    inv = ROPE_THETA ** (-jnp.arange(half, dtype=F32) / half)
    ang = pos.astype(F32)[:, None] * inv[None, :]
    cos, sin = jnp.cos(ang), jnp.sin(ang)
    cos_t = jnp.tile(cos, (1, LANES // half))
    sin_t = jnp.tile(jnp.concatenate([-sin, sin], axis=-1), (1, LANES // A_HEAD_DIM))

    body = functools.partial(_mixer_body, ch=ch, tm=tm, nt=nt, has_cache=has_cache, rows_out=rows_out,
                             alpha=alpha)
    tile3 = pl.BlockSpec((1, tm, D_MODEL), lambda b, t: (b, t, 0))
    in_specs = [
        pl.BlockSpec(memory_space=pltpu.SMEM),
        tile3,
        pl.BlockSpec((tm, LANES), lambda b, t: (t, 0)),
        pl.BlockSpec((tm, LANES), lambda b, t: (t, 0)),
        _const_spec((D_MODEL, MAIN_WIDTH)),
        _const_spec((D_MODEL, LANES)),
        _const_spec((1, LANES)),
        _const_spec((1, M_WIDTH)),
        _const_spec((M_WIDTH + A_WIDTH, D_MODEL)),
        _const_spec((1, D_MODEL)),
        _const_spec((1, D_MODEL)),
        _batch_spec((1, M_HEADS, M_HEAD_DIM, M_HEAD_DIM)),
        _batch_spec((1, M_HEADS, 1, M_HEAD_DIM)),
        _batch_spec((1, M_HEADS, 1, LANES)),
        _batch_spec((1, WINDOW, KV_WIDTH)),
        _batch_spec((1, WINDOW, KV_WIDTH)),
    ]
    out_specs = [
        tile3,
        _batch_spec((1, M_HEADS, M_HEAD_DIM, M_HEAD_DIM)),
        _batch_spec((1, M_HEADS, 1, M_HEAD_DIM)),
        _batch_spec((1, M_HEADS, 1, LANES)),
        _batch_spec((1, rows_out, KV_WIDTH)),
        _batch_spec((1, rows_out, KV_WIDTH)),
    ]
    out_shape = [
        jax.ShapeDtypeStruct((bsz, seq, D_MODEL), F32),
        jax.ShapeDtypeStruct((bsz, M_HEADS, M_HEAD_DIM, M_HEAD_DIM), F32),
        jax.ShapeDtypeStruct((bsz, M_HEADS, 1, M_HEAD_DIM), F32),
        jax.ShapeDtypeStruct((bsz, M_HEADS, 1, LANES), F32),
        jax.ShapeDtypeStruct((bsz, rows_out, KV_WIDTH), F32),
        jax.ShapeDtypeStruct((bsz, rows_out, KV_WIDTH), F32),
    ]
    scratch = [
        pltpu.VMEM((tm, M_WIDTH), F32),
        pltpu.VMEM((tm, M_WIDTH), F32),
        pltpu.VMEM((tm, M_WIDTH), BF16),
        pltpu.VMEM((tm, M_WIDTH), F32),
        pltpu.VMEM((A_Q_HEADS, tm, A_HEAD_DIM), BF16),
        pltpu.VMEM((A_KV_HEADS, WINDOW + tm, A_HEAD_DIM), BF16),
        pltpu.VMEM((A_KV_HEADS, WINDOW + tm, A_HEAD_DIM), BF16),
        pltpu.VMEM((tm, LANES), F32),
        pltpu.VMEM((tm, LANES), F32),
        pltpu.VMEM((nch, SUBLANES, ch), F32),
        pltpu.VMEM((tm, M_WIDTH + A_WIDTH), BF16),
        pltpu.VMEM((M_HEADS, M_HEAD_DIM, M_HEAD_DIM), F32),
        pltpu.VMEM((M_HEADS, 1, M_HEAD_DIM), F32),
        pltpu.VMEM((M_HEADS, 1, LANES), F32),
    ]
    return pl.pallas_call(
        body,
        grid=(bsz, nt),
        in_specs=in_specs,
        out_specs=out_specs,
        out_shape=out_shape,
        scratch_shapes=scratch,
        compiler_params=pltpu.CompilerParams(dimension_semantics=("arbitrary", "arbitrary"),
                                             vmem_limit_bytes=VMEM_LIMIT_BYTES),
    )(wts["sinks"], x, cos_t, sin_t, wts["w_main"], wts["w_gate"], wts["gate_bias"], wts["g_norm"],
      wts["w_out"], wts["ln1_g"], wts["ln1_b"], c0, n0, m0, kc, vc)


def _ffn_call(h, conv_prev, wts, *, tm, alpha):
    bsz, seq, _ = h.shape
    nt = seq // tm
    assert seq % tm == 0 and tm % SUBLANES == 0
    body = functools.partial(_ffn_body, tm=tm, nt=nt, alpha=alpha)
    tile3 = pl.BlockSpec((1, tm, D_MODEL), lambda b, t: (b, t, 0))
    return pl.pallas_call(
        body,
        grid=(bsz, nt),
        in_specs=[
            tile3,
            _batch_spec((1, SUBLANES, UP_WIDTH)),
            _const_spec((D_MODEL, UP_WIDTH)),
            _const_spec((CONV_W, UP_WIDTH)),
            _const_spec((1, UP_WIDTH)),
            _const_spec((D_FF, D_MODEL)),
            _const_spec((1, D_MODEL)),
            _const_spec((1, D_MODEL)),
        ],
        out_specs=[tile3, _batch_spec((1, SUBLANES, UP_WIDTH))],
        out_shape=[jax.ShapeDtypeStruct((bsz, seq, D_MODEL), F32),
                   jax.ShapeDtypeStruct((bsz, SUBLANES, UP_WIDTH), F32)],
        scratch_shapes=[pltpu.VMEM((SUBLANES, UP_WIDTH), F32), pltpu.VMEM((tm, D_MODEL), F32)],
        compiler_params=pltpu.CompilerParams(dimension_semantics=("arbitrary", "arbitrary"),
                                             vmem_limit_bytes=VMEM_LIMIT_BYTES),
    )(h, conv_prev, wts["w_up"], wts["w_conv"], wts["b_conv"], wts["w_down"], wts["ln2_g"], wts["ln2_b"])


def _prep_weights(w_in, b_igate, b_fgate, g_mlstm_norm, attn_sinks, w_out, ln1_g, ln1_b,
                  w_up, w_conv, b_conv, w_down, ln2_g, ln2_b):
    gate_lo = 4 * M_WIDTH
    gate_hi = gate_lo + 2 * M_HEADS
    w_main = jnp.concatenate([w_in[:, :gate_lo], w_in[:, gate_hi:]], axis=1).astype(BF16)
    w_gate = jnp.pad(w_in[:, gate_lo:gate_hi], ((0, 0), (0, LANES - 2 * M_HEADS))).astype(BF16)
    gate_bias = jnp.pad(jnp.concatenate([b_igate, b_fgate]).astype(F32), (0, LANES - 2 * M_HEADS))
    row = lambda a: a.astype(F32).reshape(1, -1)
    return dict(
        sinks=attn_sinks.astype(F32), w_main=w_main, w_gate=w_gate, gate_bias=row(gate_bias),
        g_norm=row(g_mlstm_norm), w_out=w_out.astype(BF16), ln1_g=row(ln1_g), ln1_b=row(ln1_b),
        w_up=w_up.astype(BF16), w_conv=w_conv.astype(F32), b_conv=row(b_conv), w_down=w_down.astype(BF16),
        ln2_g=row(ln2_g), ln2_b=row(ln2_b))


def _stream_layer(x, pos, c0, n0, m0, kc, vc, conv_prev, wts, *, ch, tm, has_cache, alpha):
    bsz, seq, _ = x.shape
    n0 = n0.reshape(bsz, M_HEADS, 1, M_HEAD_DIM)
    m0 = jnp.broadcast_to(m0.reshape(bsz, M_HEADS, 1, 1), (bsz, M_HEADS, 1, LANES))
    kc = kc.reshape(bsz, WINDOW, KV_WIDTH)
    vc = vc.reshape(bsz, WINDOW, KV_WIDTH)
    prev8 = jnp.pad(conv_prev, ((0, 0), (SUBLANES - (CONV_W - 1), 0), (0, 0)))
    h, c1, n1, m1, k_rows, v_rows = _mixer_call(x, pos, c0, n0, m0, kc, vc, wts, ch=ch, tm=tm,
                                                has_cache=has_cache, alpha=alpha)
    y, rows8 = _ffn_call(h, prev8, wts, tm=tm, alpha=alpha)
    rows_out = k_rows.shape[1]
    state = (c1, n1.reshape(bsz, M_HEADS, M_HEAD_DIM), m1[:, :, 0, 0],
             k_rows.reshape(bsz, rows_out, A_KV_HEADS, A_HEAD_DIM),
             v_rows.reshape(bsz, rows_out, A_KV_HEADS, A_HEAD_DIM),
             rows8[:, SUBLANES - (CONV_W - 1):, :])
    return y, state


def kernel(x_prompt, x_sample, state_mlstm_C, state_mlstm_n, state_mlstm_m, cache_swa_k, cache_swa_v,
           state_conv, w_in, b_igate, b_fgate, g_mlstm_norm, attn_sinks, w_out, ln1_g, ln1_b,
           w_up, w_conv, b_conv, w_down, ln2_g, ln2_b):
    depth = w_in.shape[0]
    alpha = (2 * depth) ** 0.25
    bp, tp = x_prompt.shape[0], x_prompt.shape[1]
    bs, ts = x_sample.shape[0], x_sample.shape[1]
    pos_p = jnp.arange(tp, dtype=jnp.int32)
    pos_s = PAST_LEN + jnp.arange(ts, dtype=jnp.int32)
    tm_p = min(tp, 512)
    hp, hs = x_prompt, x_sample
    sp, ss = [], []
    for l in range(depth):
        wts = _prep_weights(w_in[l], b_igate[l], b_fgate[l], g_mlstm_norm[l], attn_sinks[l], w_out[l],
                            ln1_g[l], ln1_b[l], w_up[l], w_conv[l], b_conv[l], w_down[l], ln2_g[l], ln2_b[l])
        hp, st_p = _stream_layer(
            hp, pos_p,
            jnp.zeros((bp, M_HEADS, M_HEAD_DIM, M_HEAD_DIM), F32),
            jnp.zeros((bp, M_HEADS, M_HEAD_DIM), F32),
            jnp.zeros((bp, M_HEADS), F32),
            jnp.zeros((bp, WINDOW, A_KV_HEADS, A_HEAD_DIM), F32),
            jnp.zeros((bp, WINDOW, A_KV_HEADS, A_HEAD_DIM), F32),
            jnp.zeros((bp, CONV_W - 1, UP_WIDTH), F32),
            wts, ch=min(CHUNK, tp), tm=tm_p, has_cache=False, alpha=alpha)
        hs, st_s = _stream_layer(
            hs, pos_s, state_mlstm_C[l], state_mlstm_n[l], state_mlstm_m[l],
            cache_swa_k[l], cache_swa_v[l], state_conv[l],
            wts, ch=ts, tm=ts, has_cache=True, alpha=alpha)
        sp.append(st_p)
        ss.append(st_s)
    P = [jnp.stack([s[i] for s in sp]) for i in range(6)]
    S = [jnp.stack([s[i] for s in ss]) for i in range(6)]
    return (hp, hs, P[0], P[1], P[2], P[3], P[4], P[5], S[0], S[1], S[2], S[3], S[4], S[5])
```

```python
import functools

import jax
import jax.numpy as jnp
from jax import lax
from jax.experimental import pallas as pl
from jax.experimental.pallas import tpu as pltpu

D_MODEL = 1024
M_HEADS = 4
M_HEAD_DIM = 128
M_WIDTH = M_HEADS * M_HEAD_DIM
A_Q_HEADS = 8
A_KV_HEADS = 2
A_HEAD_DIM = 64
A_GROUP = A_Q_HEADS // A_KV_HEADS
A_WIDTH = A_Q_HEADS * A_HEAD_DIM
KV_WIDTH = A_KV_HEADS * A_HEAD_DIM
WINDOW = 128
CHUNK = 64
ROPE_THETA = 10000.0
D_FF = 2816
UP_WIDTH = 2 * D_FF
CONV_W = 3
LN_EPS = 1e-5
PAST_LEN = 2048

LANES = 128
SUBLANES = 8
GATE_LANE = M_HEADS
MAIN_WIDTH = 4 * M_WIDTH + A_WIDTH + 2 * KV_WIDTH
FF_BLOCK = 256
VMEM_LIMIT_BYTES = 56 * 1024 * 1024

F32 = jnp.float32
BF16 = jnp.bfloat16
NEG_INF = float("-inf")


def _layer_norm(y, g, b):
    mu = jnp.mean(y, axis=-1, keepdims=True)
    d = y - mu
    var = jnp.mean(d * d, axis=-1, keepdims=True)
    return d * lax.rsqrt(var + LN_EPS) * g + b


def _log_sigmoid(x):
    return jnp.minimum(x, 0.0) - jnp.log1p(jnp.exp(-jnp.abs(x)))


def _chunk_scan(x, group, *, reverse=False, use_max=False):
    rows = x.shape[0]
    pos = lax.broadcasted_iota(jnp.int32, x.shape, 0) % group
    fill = NEG_INF if use_max else 0.0
    acc = x
    k = 1
    while k < group:
        if reverse:
            shifted = jnp.where(pos < group - k, pltpu.roll(acc, rows - k, 0), fill)
        else:
            shifted = jnp.where(pos >= k, pltpu.roll(acc, k, 0), fill)
        acc = jnp.maximum(acc, shifted) if use_max else acc + shifted
        k *= 2
    return acc


def _rope(x, cos, sin_signed):
    lane = lax.broadcasted_iota(jnp.int32, x.shape, 1) % A_HEAD_DIM
    partner = jnp.where(lane < A_HEAD_DIM // 2,
                        pltpu.roll(x, LANES - A_HEAD_DIM // 2, 1),
                        pltpu.roll(x, A_HEAD_DIM // 2, 1))
    return x * cos + partner * sin_signed


def _mixer_body(sinks_ref, x_ref, cos_ref, sin_ref, wmain_ref, wgate_ref, gbias_ref, gnorm_ref, wout_ref,
                ln1g_ref, ln1b_ref, cn0_ref, m0_ref, kc_ref, vc_ref,
                h_ref, cnout_ref, mout_ref, kout_ref, vout_ref,
                q_sc, kb_sc, kf_sc, vaug_sc, o_sc, qa_sc, ka_sc, ve_sc, vo_sc,
                wi_sc, em_sc, wk_sc, dec_sc, d_sc, hh_sc, mix_sc, cn_sc, m_sc,
                *, ch, tm, nt, has_cache, rows_out, alpha):
    t = pl.program_id(1)
    nch = tm // ch
    band = WINDOW + ch
    left = lax.broadcasted_iota(jnp.int32, (1, LANES), 1) < A_HEAD_DIM
    ones_left = jnp.where(left, 1.0, 0.0).astype(BF16)
    ones_right = jnp.where(left, 0.0, 1.0).astype(BF16)

    def store_values(row_slice, nrows, va):
        swapped = pltpu.roll(va, A_HEAD_DIM, 1)
        ve_sc[0, row_slice, 0:LANES] = jnp.where(left, va, 0.0).astype(BF16)
        vo_sc[0, row_slice, 0:LANES] = jnp.where(left, 0.0, swapped).astype(BF16)
        ve_sc[1, row_slice, 0:LANES] = jnp.where(left, swapped, 0.0).astype(BF16)
        vo_sc[1, row_slice, 0:LANES] = jnp.where(left, 0.0, va).astype(BF16)
        for j in range(A_KV_HEADS):
            ve_sc[j, row_slice, LANES:2 * LANES] = jnp.broadcast_to(ones_left, (nrows, LANES))
            vo_sc[j, row_slice, LANES:2 * LANES] = jnp.broadcast_to(ones_right, (nrows, LANES))

    @pl.when(t == 0)
    def _load_state():
        cn_sc[...] = cn0_ref[0]
        m_sc[...] = m0_ref[0]
        kc = kc_ref[0].astype(BF16)
        for j in range(A_KV_HEADS):
            ka_sc[j, 0:WINDOW, :] = kc[:, j * A_HEAD_DIM:(j + 1) * A_HEAD_DIM]
        store_values(slice(0, WINDOW), WINDOW, vc_ref[0])

    if nt > 1:
        @pl.when(t > 0)
        def _slide_window():
            for j in range(A_KV_HEADS):
                ka_sc[j, 0:WINDOW, :] = ka_sc[j, tm:tm + WINDOW, :]
                ve_sc[j, 0:WINDOW, :] = ve_sc[j, tm:tm + WINDOW, :]
                vo_sc[j, 0:WINDOW, :] = vo_sc[j, tm:tm + WINDOW, :]

    xb = x_ref[0].astype(BF16)

    def proj(lo, hi):
        return jnp.dot(xb, wmain_ref[:, lo:hi], preferred_element_type=F32)

    q_sc[...] = proj(0, M_WIDTH).astype(BF16)
    kf = proj(M_WIDTH, 2 * M_WIDTH) * (M_HEAD_DIM ** -0.5)
    kf_sc[...] = kf
    kb_sc[...] = kf.astype(BF16)
    vm = proj(2 * M_WIDTH, 3 * M_WIDTH).astype(BF16)
    for hd in range(M_HEADS):
        vaug_sc[hd, :, 0:M_HEAD_DIM] = vm[:, hd * M_HEAD_DIM:(hd + 1) * M_HEAD_DIM]
        vaug_sc[hd, :, M_HEAD_DIM:2 * M_HEAD_DIM] = jnp.ones((tm, M_HEAD_DIM), BF16)
    o_sc[...] = jax.nn.sigmoid(proj(3 * M_WIDTH, 4 * M_WIDTH))

    cos = cos_ref[...]
    sin_signed = sin_ref[...]
    base = 4 * M_WIDTH
    for p in range(A_Q_HEADS // 2):
        slab = _rope(proj(base + p * LANES, base + (p + 1) * LANES), cos, sin_signed)
        slab = (slab * (A_HEAD_DIM ** -0.5)).astype(BF16)
        qa_sc[2 * p] = slab[:, 0:A_HEAD_DIM]
        qa_sc[2 * p + 1] = slab[:, A_HEAD_DIM:LANES]
    base += A_WIDTH
    ka = _rope(proj(base, base + KV_WIDTH), cos, sin_signed)
    va = proj(base + KV_WIDTH, base + 2 * KV_WIDTH)
    kab = ka.astype(BF16)
    for j in range(A_KV_HEADS):
        ka_sc[j, WINDOW:WINDOW + tm, :] = kab[:, j * A_HEAD_DIM:(j + 1) * A_HEAD_DIM]
    store_values(slice(WINDOW, WINDOW + tm), tm, va)
    kout_ref[0] = ka[tm - rows_out:tm, :]
    vout_ref[0] = va[tm - rows_out:tm, :]

    gates = jnp.dot(xb, wgate_ref[...], preferred_element_type=F32) + gbias_ref[...]
    log_i = pltpu.roll(gates, GATE_LANE, 1)
    log_f = _log_sigmoid(gates)
    b_incl = _chunk_scan(log_f, ch)
    tail = _chunk_scan(log_f, ch, reverse=True) - log_f
    g_row = tail + log_i
    r_val = log_i - b_incl
    r_cummax = _chunk_scan(r_val, ch, use_max=True)
    pos = lax.broadcasted_iota(jnp.int32, (nch, ch, LANES), 1)
    b_last = jnp.max(jnp.where(pos == ch - 1, b_incl.reshape(nch, ch, LANES), NEG_INF), axis=1)
    g_max = jnp.max(g_row.reshape(nch, ch, LANES), axis=1)
    m_run = m_sc[...]
    m_starts = []
    for c in range(nch):
        m_starts.append(m_run)
        m_run = jnp.maximum(b_last[c:c + 1, :] + m_run, g_max[c:c + 1, :])
    m_sc[...] = m_run
    m_start = jnp.concatenate(m_starts, axis=0)
    m_next = jnp.concatenate(m_starts[1:] + [m_run], axis=0)
    decay = jnp.exp(b_last + m_start - m_next)

    def per_row(a):
        return jnp.broadcast_to(a[:, None, :], (nch, ch, LANES)).reshape(tm, LANES)

    m_rows = per_row(m_start)
    mx = jnp.maximum(m_rows, r_cummax)
    w_inter = jnp.exp(m_rows - mx)
    inv_floor = jnp.exp(-(b_incl + mx))
    w_state = jnp.exp(g_row - per_row(m_next))
    r_t = jnp.transpose(r_val)
    causal = (lax.broadcasted_iota(jnp.int32, (ch, ch), 0) >= lax.broadcasted_iota(jnp.int32, (ch, ch), 1))
    for hd in range(M_HEADS):
        ln = GATE_LANE + hd
        wi_sc[hd] = jnp.broadcast_to(w_inter[:, ln:ln + 1], (tm, LANES))
        em_sc[hd] = jnp.broadcast_to(inv_floor[:, ln:ln + 1], (tm, LANES))
        wk_sc[hd] = jnp.broadcast_to(w_state[:, ln:ln + 1], (tm, LANES))
        dec_sc[hd] = jnp.broadcast_to(decay[:, ln:ln + 1], (nch, LANES))
        mx_b = jnp.broadcast_to(mx[:, ln:ln + 1], (tm, ch))
        for c in range(nch):
            r_row = r_t[ln:ln + 1, c * ch:(c + 1) * ch]
            d_sc[hd, c] = jnp.where(causal, jnp.exp(r_row - mx_b[c * ch:(c + 1) * ch, :]), 0.0)

    band_pos = lax.broadcasted_iota(jnp.int32, (1, band), 1)

    def chunk_step(c, carry):
        r0 = pl.multiple_of(c * ch, ch)
        rows = pl.ds(r0, ch)

        nt_dims = (((1,), (1,)), ((), ()))
        heads = range(M_HEADS)
        hcols = [slice(hd * M_HEAD_DIM, (hd + 1) * M_HEAD_DIM) for hd in heads]
        qb = [q_sc[rows, hcols[hd]] for hd in heads]
        vaug = [vaug_sc[hd, rows, :] for hd in heads]
        state = [cn_sc[hd] for hd in heads]
        qk = [lax.dot_general(qb[hd], kb_sc[rows, hcols[hd]], nt_dims, preferred_element_type=F32)
              for hd in heads]
        inter = [jnp.dot(qb[hd], state[hd].astype(BF16), preferred_element_type=F32)
                 for hd in heads]
        kw = [(kf_sc[rows, hcols[hd]] * wk_sc[hd, rows, :]).astype(BF16) for hd in heads]
        upd = [lax.dot_general(kw[hd], vaug[hd], (((0,), (0,)), ((), ())), preferred_element_type=F32)
               for hd in heads]

        if not has_cache:
            first_valid = WINDOW - (t * nch + c) * ch
            mask_bias = jnp.where(band_pos >= first_valid, 0.0, NEG_INF)
        qheads = range(A_Q_HEADS)
        kband = [ka_sc[j, pl.ds(r0, band), :] for j in range(A_KV_HEADS)]
        sc = [lax.dot_general(qa_sc[hq, rows, :], kband[hq // A_GROUP], nt_dims, preferred_element_type=F32)
              for hq in qheads]
        if not has_cache:
            sc = [a + mask_bias for a in sc]

        s = [(qk[hd] * d_sc[hd, c]).astype(BF16) for hd in heads]
        mx_col = [jnp.maximum(jnp.max(sc[hq], axis=-1, keepdims=True), sinks_ref[hq]) for hq in qheads]
        intra = [jnp.dot(s[hd], vaug[hd], preferred_element_type=F32) for hd in heads]
        mx_wide = [jnp.broadcast_to(mx_col[hq], (ch, band)) for hq in qheads]
        pr = [jnp.exp(sc[hq] - mx_wide[hq]).astype(BF16) for hq in qheads]
        sink_term = [jnp.exp(sinks_ref[hq] - mx_wide[hq][:, 0:LANES]) for hq in qheads]
        part = [jnp.dot(pr[hq], (ve_sc, vo_sc)[hq % 2][hq // A_GROUP, pl.ds(r0, band), :],
                        preferred_element_type=F32) for hq in qheads]

        for hd in heads:
            wi = wi_sc[hd, rows, :]
            num = intra[hd][:, 0:M_HEAD_DIM] + wi * inter[hd][:, 0:M_HEAD_DIM]
            den = intra[hd][:, M_HEAD_DIM:] + wi * inter[hd][:, M_HEAD_DIM:]
            hh_sc[rows, hcols[hd]] = num / jnp.maximum(jnp.abs(den), em_sc[hd, rows, :])
            dec = dec_sc[hd, pl.ds(c, 1), :]
            cn_sc[hd] = jnp.concatenate([dec, dec], axis=1) * state[hd] + upd[hd]

        for p in range(A_Q_HEADS // 2):
            acc = part[2 * p] + part[2 * p + 1]
            dn = acc[:, LANES:] + jnp.where(left, sink_term[2 * p], sink_term[2 * p + 1])
            mix_sc[rows, M_WIDTH + p * LANES:M_WIDTH + (p + 1) * LANES] = (acc[:, 0:LANES] / dn).astype(BF16)
        return carry

    lax.fori_loop(0, nch, chunk_step, 0)

    for hd in range(M_HEADS):
        cols = slice(hd * M_HEAD_DIM, (hd + 1) * M_HEAD_DIM)
        hh = hh_sc[:, cols]
        mu = jnp.mean(hh, axis=-1, keepdims=True)
        dlt = hh - mu
        var = jnp.mean(dlt * dlt, axis=-1, keepdims=True)
        mix_sc[:, cols] = (o_sc[:, cols] * (dlt * lax.rsqrt(var + LN_EPS) * gnorm_ref[:, cols])).astype(BF16)

    mixed = jnp.dot(mix_sc[...], wout_ref[...], preferred_element_type=F32)
    h_ref[0] = _layer_norm(alpha * x_ref[0] + mixed, ln1g_ref[...], ln1b_ref[...])

    @pl.when(t == nt - 1)
    def _store_state():
        cnout_ref[0] = cn_sc[...]
        mout_ref[0] = m_sc[...]


def _ffn_body(h_ref, prev_ref, wup_ref, wconv_ref, bconv_ref, wdown_ref, ln2g_ref, ln2b_ref,
              y_ref, rows_ref, carry_sc, acc_sc, *, tm, nt, alpha):
    t = pl.program_id(1)

    @pl.when(t == 0)
    def _load_carry():
        carry_sc[...] = prev_ref[0]

    hf = h_ref[0]
    hb = hf.astype(BF16)
    acc_sc[...] = jnp.zeros_like(acc_sc)

    def conv_branch(lo):
        cols = slice(lo, lo + FF_BLOCK)
        u = jnp.dot(hb, wup_ref[:, cols], preferred_element_type=F32)
        ext = jnp.concatenate([carry_sc[:, cols], u], axis=0)
        carry_sc[:, cols] = u[tm - SUBLANES:tm, :]
        rows = tm + SUBLANES
        back1 = pltpu.roll(ext, 1, 0)[SUBLANES:rows, :]
        back2 = pltpu.roll(ext, 2, 0)[SUBLANES:rows, :]
        w = wconv_ref[:, cols]
        return bconv_ref[:, cols] + back2 * w[0:1, :] + back1 * w[1:2, :] + u * w[2:3, :]

    for blk in range(D_FF // FF_BLOCK):
        val = conv_branch(blk * FF_BLOCK)
        gate = conv_branch(D_FF + blk * FF_BLOCK)
        act = (gate * jax.nn.sigmoid(gate) * val).astype(BF16)
        acc_sc[...] += jnp.dot(act, wdown_ref[blk * FF_BLOCK:(blk + 1) * FF_BLOCK, :],
                               preferred_element_type=F32)

    y_ref[0] = _layer_norm(alpha * hf + acc_sc[...], ln2g_ref[...], ln2b_ref[...])

    @pl.when(t == nt - 1)
    def _store_rows():
        rows_ref[0] = carry_sc[...]


def _const_spec(shape):
    zeros = (0,) * len(shape)
    return pl.BlockSpec(shape, lambda b, t: zeros, pipeline_mode=pl.Buffered(1))


def _batch_spec(shape):
    zeros = (0,) * (len(shape) - 1)
    return pl.BlockSpec(shape, lambda b, t: (b,) + zeros)


def _mixer_call(x, pos, cn0, m0, kc, vc, wts, *, ch, tm, has_cache, alpha):
    bsz, seq, _ = x.shape
    nt = seq // tm
    nch = tm // ch
    rows_out = min(WINDOW, seq)
    assert seq % tm == 0 and tm % ch == 0 and rows_out <= tm and (nt == 1 or tm >= WINDOW)

    half = A_HEAD_DIM // 2
    inv = ROPE_THETA ** (-jnp.arange(half, dtype=F32) / half)
    ang = pos.astype(F32)[:, None] * inv[None, :]
    cos, sin = jnp.cos(ang), jnp.sin(ang)
    cos_t = jnp.tile(cos, (1, LANES // half))
    sin_t = jnp.tile(jnp.concatenate([-sin, sin], axis=-1), (1, LANES // A_HEAD_DIM))

    body = functools.partial(_mixer_body, ch=ch, tm=tm, nt=nt, has_cache=has_cache, rows_out=rows_out,
                             alpha=alpha)
    tile3 = pl.BlockSpec((1, tm, D_MODEL), lambda b, t: (b, t, 0))
    state_shape = (1, M_HEADS, M_HEAD_DIM, 2 * M_HEAD_DIM)
    in_specs = [
        pl.BlockSpec(memory_space=pltpu.SMEM),
        tile3,
        pl.BlockSpec((tm, LANES), lambda b, t: (t, 0)),
        pl.BlockSpec((tm, LANES), lambda b, t: (t, 0)),
        _const_spec((D_MODEL, MAIN_WIDTH)),
        _const_spec((D_MODEL, LANES)),
        _const_spec((1, LANES)),
        _const_spec((1, M_WIDTH)),
        _const_spec((M_WIDTH + A_WIDTH, D_MODEL)),
        _const_spec((1, D_MODEL)),
        _const_spec((1, D_MODEL)),
        _batch_spec(state_shape),
        _batch_spec((1, 1, LANES)),
        _batch_spec((1, WINDOW, KV_WIDTH)),
        _batch_spec((1, WINDOW, KV_WIDTH)),
    ]
    out_specs = [
        tile3,
        _batch_spec(state_shape),
        _batch_spec((1, 1, LANES)),
        _batch_spec((1, rows_out, KV_WIDTH)),
        _batch_spec((1, rows_out, KV_WIDTH)),
    ]
    out_shape = [
        jax.ShapeDtypeStruct((bsz, seq, D_MODEL), F32),
        jax.ShapeDtypeStruct((bsz,) + state_shape[1:], F32),
        jax.ShapeDtypeStruct((bsz, 1, LANES), F32),
        jax.ShapeDtypeStruct((bsz, rows_out, KV_WIDTH), F32),
        jax.ShapeDtypeStruct((bsz, rows_out, KV_WIDTH), F32),
    ]
    scratch = [
        pltpu.VMEM((tm, M_WIDTH), BF16),
        pltpu.VMEM((tm, M_WIDTH), BF16),
        pltpu.VMEM((tm, M_WIDTH), F32),
        pltpu.VMEM((M_HEADS, tm, 2 * M_HEAD_DIM), BF16),
        pltpu.VMEM((tm, M_WIDTH), F32),
        pltpu.VMEM((A_Q_HEADS, tm, A_HEAD_DIM), BF16),
        pltpu.VMEM((A_KV_HEADS, WINDOW + tm, A_HEAD_DIM), BF16),
        pltpu.VMEM((A_KV_HEADS, WINDOW + tm, 2 * LANES), BF16),
        pltpu.VMEM((A_KV_HEADS, WINDOW + tm, 2 * LANES), BF16),
        pltpu.VMEM((M_HEADS, tm, LANES), F32),
        pltpu.VMEM((M_HEADS, tm, LANES), F32),
        pltpu.VMEM((M_HEADS, tm, LANES), F32),
        pltpu.VMEM((M_HEADS, nch, LANES), F32),
        pltpu.VMEM((M_HEADS, nch, ch, ch), F32),
        pltpu.VMEM((tm, M_WIDTH), F32),
        pltpu.VMEM((tm, M_WIDTH + A_WIDTH), BF16),
        pltpu.VMEM((M_HEADS, M_HEAD_DIM, 2 * M_HEAD_DIM), F32),
        pltpu.VMEM((1, LANES), F32),
    ]
    return pl.pallas_call(
        body,
        grid=(bsz, nt),
        in_specs=in_specs,
        out_specs=out_specs,
        out_shape=out_shape,
        scratch_shapes=scratch,
        compiler_params=pltpu.CompilerParams(dimension_semantics=("arbitrary", "arbitrary"),
                                             vmem_limit_bytes=VMEM_LIMIT_BYTES),
    )(wts["sinks"], x, cos_t, sin_t, wts["w_main"], wts["w_gate"], wts["gate_bias"], wts["g_norm"],
      wts["w_out"], wts["ln1_g"], wts["ln1_b"], cn0, m0, kc, vc)


def _ffn_call(h, conv_prev, wts, *, tm, alpha):
    bsz, seq, _ = h.shape
    nt = seq // tm
    assert seq % tm == 0 and tm % SUBLANES == 0
    body = functools.partial(_ffn_body, tm=tm, nt=nt, alpha=alpha)
    tile3 = pl.BlockSpec((1, tm, D_MODEL), lambda b, t: (b, t, 0))
    return pl.pallas_call(
        body,
        grid=(bsz, nt),
        in_specs=[
            tile3,
            _batch_spec((1, SUBLANES, UP_WIDTH)),
            _const_spec((D_MODEL, UP_WIDTH)),
            _const_spec((CONV_W, UP_WIDTH)),
            _const_spec((1, UP_WIDTH)),
            _const_spec((D_FF, D_MODEL)),
            _const_spec((1, D_MODEL)),
            _const_spec((1, D_MODEL)),
        ],
        out_specs=[tile3, _batch_spec((1, SUBLANES, UP_WIDTH))],
        out_shape=[jax.ShapeDtypeStruct((bsz, seq, D_MODEL), F32),
                   jax.ShapeDtypeStruct((bsz, SUBLANES, UP_WIDTH), F32)],
        scratch_shapes=[pltpu.VMEM((SUBLANES, UP_WIDTH), F32), pltpu.VMEM((tm, D_MODEL), F32)],
        compiler_params=pltpu.CompilerParams(dimension_semantics=("arbitrary", "arbitrary"),
                                             vmem_limit_bytes=VMEM_LIMIT_BYTES),
    )(h, conv_prev, wts["w_up"], wts["w_conv"], wts["b_conv"], wts["w_down"], wts["ln2_g"], wts["ln2_b"])


def _prep_weights(w_in, b_igate, b_fgate, g_mlstm_norm, attn_sinks, w_out, ln1_g, ln1_b,
                  w_up, w_conv, b_conv, w_down, ln2_g, ln2_b):
    gate_lo = 4 * M_WIDTH
    gate_hi = gate_lo + 2 * M_HEADS
    w_main = jnp.concatenate([w_in[:, :gate_lo], w_in[:, gate_hi:]], axis=1).astype(BF16)
    w_gate = jnp.pad(w_in[:, gate_lo:gate_hi], ((0, 0), (0, LANES - 2 * M_HEADS))).astype(BF16)
    gate_bias = jnp.pad(jnp.concatenate([b_igate, b_fgate]).astype(F32), (0, LANES - 2 * M_HEADS))
    row = lambda a: a.astype(F32).reshape(1, -1)
    return dict(
        sinks=attn_sinks.astype(F32), w_main=w_main, w_gate=w_gate, gate_bias=row(gate_bias),
        g_norm=row(g_mlstm_norm), w_out=w_out.astype(BF16), ln1_g=row(ln1_g), ln1_b=row(ln1_b),
        w_up=w_up.astype(BF16), w_conv=w_conv.astype(F32), b_conv=row(b_conv), w_down=w_down.astype(BF16),
        ln2_g=row(ln2_g), ln2_b=row(ln2_b))


def _stream_layer(x, pos, c0, n0, m0, kc, vc, conv_prev, wts, *, ch, tm, has_cache, alpha):
    bsz, seq, _ = x.shape
    n_rep = jnp.broadcast_to(n0[..., None], (bsz, M_HEADS, M_HEAD_DIM, M_HEAD_DIM))
    cn0 = jnp.concatenate([c0, n_rep], axis=-1).astype(F32)
    m0 = jnp.pad(m0.astype(F32), ((0, 0), (GATE_LANE, LANES - GATE_LANE - M_HEADS))).reshape(bsz, 1, LANES)
    kc = kc.reshape(bsz, WINDOW, KV_WIDTH)
    vc = vc.reshape(bsz, WINDOW, KV_WIDTH)
    prev8 = jnp.pad(conv_prev, ((0, 0), (SUBLANES - (CONV_W - 1), 0), (0, 0)))
    h, cn1, m1, k_rows, v_rows = _mixer_call(x, pos, cn0, m0, kc, vc, wts, ch=ch, tm=tm,
                                             has_cache=has_cache, alpha=alpha)
    y, rows8 = _ffn_call(h, prev8, wts, tm=tm, alpha=alpha)
    rows_out = k_rows.shape[1]
    state = (cn1[..., :M_HEAD_DIM], cn1[..., M_HEAD_DIM], m1[:, 0, GATE_LANE:GATE_LANE + M_HEADS],
             k_rows.reshape(bsz, rows_out, A_KV_HEADS, A_HEAD_DIM),
             v_rows.reshape(bsz, rows_out, A_KV_HEADS, A_HEAD_DIM),
             rows8[:, SUBLANES - (CONV_W - 1):, :])
    return y, state


def kernel(x_prompt, x_sample, state_mlstm_C, state_mlstm_n, state_mlstm_m, cache_swa_k, cache_swa_v,
           state_conv, w_in, b_igate, b_fgate, g_mlstm_norm, attn_sinks, w_out, ln1_g, ln1_b,
           w_up, w_conv, b_conv, w_down, ln2_g, ln2_b):
    depth = w_in.shape[0]
    alpha = (2 * depth) ** 0.25
    bp, tp = x_prompt.shape[0], x_prompt.shape[1]
    bs, ts = x_sample.shape[0], x_sample.shape[1]
    pos_p = jnp.arange(tp, dtype=jnp.int32)
    pos_s = PAST_LEN + jnp.arange(ts, dtype=jnp.int32)
    tm_p = min(tp, 512)
    hp, hs = x_prompt, x_sample
    sp, ss = [], []
    for l in range(depth):
        wts = _prep_weights(w_in[l], b_igate[l], b_fgate[l], g_mlstm_norm[l], attn_sinks[l], w_out[l],
                            ln1_g[l], ln1_b[l], w_up[l], w_conv[l], b_conv[l], w_down[l], ln2_g[l], ln2_b[l])
        hp, st_p = _stream_layer(
            hp, pos_p,
            jnp.zeros((bp, M_HEADS, M_HEAD_DIM, M_HEAD_DIM), F32),
            jnp.zeros((bp, M_HEADS, M_HEAD_DIM), F32),
            jnp.zeros((bp, M_HEADS), F32),
            jnp.zeros((bp, WINDOW, A_KV_HEADS, A_HEAD_DIM), F32),
            jnp.zeros((bp, WINDOW, A_KV_HEADS, A_HEAD_DIM), F32),
            jnp.zeros((bp, CONV_W - 1, UP_WIDTH), F32),
            wts, ch=min(CHUNK, tp), tm=tm_p, has_cache=False, alpha=alpha)
        hs, st_s = _stream_layer(
            hs, pos_s, state_mlstm_C[l], state_mlstm_n[l], state_mlstm_m[l],
            cache_swa_k[l], cache_swa_v[l], state_conv[l],
            wts, ch=ts, tm=ts, has_cache=True, alpha=alpha)
        sp.append(st_p)
        ss.append(st_s)
    P = [jnp.stack([s[i] for s in sp]) for i in range(6)]
    S = [jnp.stack([s[i] for s in ss]) for i in range(6)]
    return (hp, hs, P[0], P[1], P[2], P[3], P[4], P[5], S[0], S[1], S[2], S[3], S[4], S[5])
```

```python
import functools

import jax
import jax.numpy as jnp
from jax import lax
from jax.experimental import pallas as pl
from jax.experimental.pallas import tpu as pltpu

D_MODEL = 1024
M_HEADS = 4
M_HEAD_DIM = 128
M_WIDTH = M_HEADS * M_HEAD_DIM
A_Q_HEADS = 8
A_KV_HEADS = 2
A_HEAD_DIM = 64
A_GROUP = A_Q_HEADS // A_KV_HEADS
A_WIDTH = A_Q_HEADS * A_HEAD_DIM
KV_WIDTH = A_KV_HEADS * A_HEAD_DIM
WINDOW = 128
CHUNK = 64
ROPE_THETA = 10000.0
D_FF = 2816
UP_WIDTH = 2 * D_FF
CONV_W = 3
LN_EPS = 1e-5
PAST_LEN = 2048

LANES = 128
SUBLANES = 8
GATE_LANE = M_HEADS
MAIN_WIDTH = 4 * M_WIDTH + A_WIDTH + 2 * KV_WIDTH
FF_BLOCK = 256
VMEM_LIMIT_BYTES = 56 * 1024 * 1024

F32 = jnp.float32
BF16 = jnp.bfloat16
NEG_INF = float("-inf")


def _layer_norm(y, g, b):
    mu = jnp.mean(y, axis=-1, keepdims=True)
    d = y - mu
    var = jnp.mean(d * d, axis=-1, keepdims=True)
    return d * lax.rsqrt(var + LN_EPS) * g + b


def _log_sigmoid(x):
    return jnp.minimum(x, 0.0) - jnp.log1p(jnp.exp(-jnp.abs(x)))


def _chunk_scan(x, group, *, reverse=False, use_max=False):
    rows = x.shape[0]
    pos = lax.broadcasted_iota(jnp.int32, x.shape, 0) % group
    fill = NEG_INF if use_max else 0.0
    acc = x
    k = 1
    while k < group:
        if reverse:
            shifted = jnp.where(pos < group - k, pltpu.roll(acc, rows - k, 0), fill)
        else:
            shifted = jnp.where(pos >= k, pltpu.roll(acc, k, 0), fill)
        acc = jnp.maximum(acc, shifted) if use_max else acc + shifted
        k *= 2
    return acc


def _rope(x, cos, sin_signed):
    lane = lax.broadcasted_iota(jnp.int32, x.shape, 1) % A_HEAD_DIM
    partner = jnp.where(lane < A_HEAD_DIM // 2,
                        pltpu.roll(x, LANES - A_HEAD_DIM // 2, 1),
                        pltpu.roll(x, A_HEAD_DIM // 2, 1))
    return x * cos + partner * sin_signed


def _mixer_body(sinks_ref, x_ref, cos_ref, sin_ref, wmain_ref, wgate_ref, gbias_ref, gnorm_ref, wout_ref,
                ln1g_ref, ln1b_ref, cn0_ref, m0_ref, kc_ref, vc_ref,
                h_ref, cnout_ref, mout_ref, kout_ref, vout_ref,
                q_sc, kb_sc, kf_sc, vaug_sc, o_sc, qa_sc, ka_sc, ve_sc, vo_sc,
                wi_sc, em_sc, wk_sc, dec_sc, d_sc, hh_sc, mix_sc, cn_sc, m_sc,
                *, ch, tm, nt, has_cache, rows_out, alpha):
    t = pl.program_id(1)
    nch = tm // ch
    band = WINDOW + ch
    left = lax.broadcasted_iota(jnp.int32, (1, LANES), 1) < A_HEAD_DIM
    ones_left = jnp.where(left, 1.0, 0.0).astype(BF16)
    ones_right = jnp.where(left, 0.0, 1.0).astype(BF16)

    def store_values(row_slice, nrows, va):
        swapped = pltpu.roll(va, A_HEAD_DIM, 1)
        ve_sc[0, row_slice, 0:LANES] = jnp.where(left, va, 0.0).astype(BF16)
        vo_sc[0, row_slice, 0:LANES] = jnp.where(left, 0.0, swapped).astype(BF16)
        ve_sc[1, row_slice, 0:LANES] = jnp.where(left, swapped, 0.0).astype(BF16)
        vo_sc[1, row_slice, 0:LANES] = jnp.where(left, 0.0, va).astype(BF16)
        for j in range(A_KV_HEADS):
            ve_sc[j, row_slice, LANES:2 * LANES] = jnp.broadcast_to(ones_left, (nrows, LANES))
            vo_sc[j, row_slice, LANES:2 * LANES] = jnp.broadcast_to(ones_right, (nrows, LANES))

    @pl.when(t == 0)
    def _load_state():
        cn_sc[...] = cn0_ref[0]
        m_sc[...] = m0_ref[0]
        kc = kc_ref[0].astype(BF16)
        for j in range(A_KV_HEADS):
            ka_sc[j, 0:WINDOW, :] = kc[:, j * A_HEAD_DIM:(j + 1) * A_HEAD_DIM]
        store_values(slice(0, WINDOW), WINDOW, vc_ref[0])

    if nt > 1:
        @pl.when(t > 0)
        def _slide_window():
            for j in range(A_KV_HEADS):
                ka_sc[j, 0:WINDOW, :] = ka_sc[j, tm:tm + WINDOW, :]
                ve_sc[j, 0:WINDOW, :] = ve_sc[j, tm:tm + WINDOW, :]
                vo_sc[j, 0:WINDOW, :] = vo_sc[j, tm:tm + WINDOW, :]

    xb = x_ref[0].astype(BF16)

    def proj(lo, hi):
        return jnp.dot(xb, wmain_ref[:, lo:hi], preferred_element_type=F32)

    q_sc[...] = proj(0, M_WIDTH).astype(BF16)
    kf = proj(M_WIDTH, 2 * M_WIDTH) * (M_HEAD_DIM ** -0.5)
    kf_sc[...] = kf
    kb_sc[...] = kf.astype(BF16)
    vm = proj(2 * M_WIDTH, 3 * M_WIDTH).astype(BF16)
    for hd in range(M_HEADS):
        vaug_sc[hd, :, 0:M_HEAD_DIM] = vm[:, hd * M_HEAD_DIM:(hd + 1) * M_HEAD_DIM]
        vaug_sc[hd, :, M_HEAD_DIM:2 * M_HEAD_DIM] = jnp.ones((tm, M_HEAD_DIM), BF16)
    o_sc[...] = jax.nn.sigmoid(proj(3 * M_WIDTH, 4 * M_WIDTH))

    cos = cos_ref[...]
    sin_signed = sin_ref[...]
    base = 4 * M_WIDTH
    for p in range(A_Q_HEADS // 2):
        slab = _rope(proj(base + p * LANES, base + (p + 1) * LANES), cos, sin_signed)
        slab = (slab * (A_HEAD_DIM ** -0.5)).astype(BF16)
        qa_sc[2 * p] = slab[:, 0:A_HEAD_DIM]
        qa_sc[2 * p + 1] = slab[:, A_HEAD_DIM:LANES]
    base += A_WIDTH
    ka = _rope(proj(base, base + KV_WIDTH), cos, sin_signed)
    va = proj(base + KV_WIDTH, base + 2 * KV_WIDTH)
    kab = ka.astype(BF16)
    for j in range(A_KV_HEADS):
        ka_sc[j, WINDOW:WINDOW + tm, :] = kab[:, j * A_HEAD_DIM:(j + 1) * A_HEAD_DIM]
    store_values(slice(WINDOW, WINDOW + tm), tm, va)
    kout_ref[0] = ka[tm - rows_out:tm, :]
    vout_ref[0] = va[tm - rows_out:tm, :]

    gates = jnp.dot(xb, wgate_ref[...], preferred_element_type=F32) + gbias_ref[...]
    log_i = pltpu.roll(gates, GATE_LANE, 1)
    log_f = _log_sigmoid(gates)
    b_incl = _chunk_scan(log_f, ch)
    tail = _chunk_scan(log_f, ch, reverse=True) - log_f
    g_row = tail + log_i
    r_val = log_i - b_incl
    r_cummax = _chunk_scan(r_val, ch, use_max=True)
    pos = lax.broadcasted_iota(jnp.int32, (nch, ch, LANES), 1)
    b_last = jnp.max(jnp.where(pos == ch - 1, b_incl.reshape(nch, ch, LANES), NEG_INF), axis=1)
    g_max = jnp.max(g_row.reshape(nch, ch, LANES), axis=1)
    m_run = m_sc[...]
    m_starts = []
    for c in range(nch):
        m_starts.append(m_run)
        m_run = jnp.maximum(b_last[c:c + 1, :] + m_run, g_max[c:c + 1, :])
    m_sc[...] = m_run
    m_start = jnp.concatenate(m_starts, axis=0)
    m_next = jnp.concatenate(m_starts[1:] + [m_run], axis=0)
    decay = jnp.exp(b_last + m_start - m_next)

    def per_row(a):
        return jnp.broadcast_to(a[:, None, :], (nch, ch, LANES)).reshape(tm, LANES)

    m_rows = per_row(m_start)
    mx = jnp.maximum(m_rows, r_cummax)
    w_inter = jnp.exp(m_rows - mx)
    inv_floor = jnp.exp(-(b_incl + mx))
    w_state = jnp.exp(g_row - per_row(m_next))
    r_t = jnp.transpose(r_val)
    causal = (lax.broadcasted_iota(jnp.int32, (ch, ch), 0) >= lax.broadcasted_iota(jnp.int32, (ch, ch), 1))
    for hd in range(M_HEADS):
        ln = GATE_LANE + hd
        wi_sc[hd] = jnp.broadcast_to(w_inter[:, ln:ln + 1], (tm, LANES))
        em_sc[hd] = jnp.broadcast_to(inv_floor[:, ln:ln + 1], (tm, LANES))
        wk_sc[hd] = jnp.broadcast_to(w_state[:, ln:ln + 1], (tm, LANES))
        dec_sc[hd] = jnp.broadcast_to(decay[:, ln:ln + 1], (nch, LANES))
        mx_b = jnp.broadcast_to(mx[:, ln:ln + 1], (tm, ch))
        for c in range(nch):
            r_row = r_t[ln:ln + 1, c * ch:(c + 1) * ch]
            d_sc[hd, c] = jnp.where(causal, jnp.exp(r_row - mx_b[c * ch:(c + 1) * ch, :]), 0.0)

    band_pos = lax.broadcasted_iota(jnp.int32, (1, band), 1)

    def chunk_step(c, carry):
        r0 = pl.multiple_of(c * ch, ch)
        rows = pl.ds(r0, ch)

        nt_dims = (((1,), (1,)), ((), ()))
        heads = range(M_HEADS)
        hcols = [slice(hd * M_HEAD_DIM, (hd + 1) * M_HEAD_DIM) for hd in heads]
        qb = [q_sc[rows, hcols[hd]] for hd in heads]
        vaug = [vaug_sc[hd, rows, :] for hd in heads]
        state = [cn_sc[hd] for hd in heads]
        qk = [lax.dot_general(qb[hd], kb_sc[rows, hcols[hd]], nt_dims, preferred_element_type=F32)
              for hd in heads]
        inter = [jnp.dot(qb[hd], state[hd].astype(BF16), preferred_element_type=F32)
                 for hd in heads]
        kw = [(kf_sc[rows, hcols[hd]] * wk_sc[hd, rows, :]).astype(BF16) for hd in heads]
        upd = [lax.dot_general(kw[hd], vaug[hd], (((0,), (0,)), ((), ())), preferred_element_type=F32)
               for hd in heads]

        if not has_cache:
            first_valid = WINDOW - (t * nch + c) * ch
            mask_bias = jnp.where(band_pos >= first_valid, 0.0, NEG_INF)
        qheads = range(A_Q_HEADS)
        kband = [ka_sc[j, pl.ds(r0, band), :] for j in range(A_KV_HEADS)]
        sc = [lax.dot_general(qa_sc[hq, rows, :], kband[hq // A_GROUP], nt_dims, preferred_element_type=F32)
              for hq in qheads]
        if not has_cache:
            sc = [a + mask_bias for a in sc]

        s = [(qk[hd] * d_sc[hd, c]).astype(BF16) for hd in heads]
        mx_col = [jnp.maximum(jnp.max(sc[hq], axis=-1, keepdims=True), sinks_ref[hq]) for hq in qheads]
        intra = [jnp.dot(s[hd], vaug[hd], preferred_element_type=F32) for hd in heads]
        mx_wide = [jnp.broadcast_to(mx_col[hq], (ch, band)) for hq in qheads]
        pr = [jnp.exp(sc[hq] - mx_wide[hq]).astype(BF16) for hq in qheads]
        sink_term = [jnp.exp(sinks_ref[hq] - mx_wide[hq][:, 0:LANES]) for hq in qheads]
        part = [jnp.dot(pr[hq], (ve_sc, vo_sc)[hq % 2][hq // A_GROUP, pl.ds(r0, band), :],
                        preferred_element_type=F32) for hq in qheads]

        for hd in heads:
            wi = wi_sc[hd, rows, :]
            num = intra[hd][:, 0:M_HEAD_DIM] + wi * inter[hd][:, 0:M_HEAD_DIM]
            den = intra[hd][:, M_HEAD_DIM:] + wi * inter[hd][:, M_HEAD_DIM:]
            hh_sc[rows, hcols[hd]] = num / jnp.maximum(jnp.abs(den), em_sc[hd, rows, :])
            dec = dec_sc[hd, pl.ds(c, 1), :]
            cn_sc[hd] = jnp.concatenate([dec, dec], axis=1) * state[hd] + upd[hd]

        for p in range(A_Q_HEADS // 2):
            acc = part[2 * p] + part[2 * p + 1]
            dn = acc[:, LANES:] + jnp.where(left, sink_term[2 * p], sink_term[2 * p + 1])
            mix_sc[rows, M_WIDTH + p * LANES:M_WIDTH + (p + 1) * LANES] = (acc[:, 0:LANES] / dn).astype(BF16)
        return carry

    lax.fori_loop(0, nch, chunk_step, 0)

    for hd in range(M_HEADS):
        cols = slice(hd * M_HEAD_DIM, (hd + 1) * M_HEAD_DIM)
        hh = hh_sc[:, cols]
        mu = jnp.mean(hh, axis=-1, keepdims=True)
        dlt = hh - mu
        var = jnp.mean(dlt * dlt, axis=-1, keepdims=True)
        mix_sc[:, cols] = (o_sc[:, cols] * (dlt * lax.rsqrt(var + LN_EPS) * gnorm_ref[:, cols])).astype(BF16)

    mixed = jnp.dot(mix_sc[...], wout_ref[...], preferred_element_type=F32)
    h_ref[0] = _layer_norm(alpha * x_ref[0] + mixed, ln1g_ref[...], ln1b_ref[...])

    @pl.when(t == nt - 1)
    def _store_state():
        cnout_ref[0] = cn_sc[...]
        mout_ref[0] = m_sc[...]


def _ffn_body(h_ref, prev_ref, wup_ref, wconv_ref, bconv_ref, wdown_ref, ln2g_ref, ln2b_ref,
              y_ref, rows_ref, carry_sc, acc_sc, perm_sc, *, tm, nt, alpha):
    t = pl.program_id(1)

    @pl.when(t == 0)
    def _load_carry():
        carry_sc[...] = prev_ref[0]

    nv = tm // SUBLANES
    pitch = nv + 1
    nl = D_MODEL // LANES
    for k in range(nl):
        for s in range(SUBLANES):
            perm_sc[k, s * pitch:s * pitch + nv, :] = h_ref[0, s * nv:(s + 1) * nv, k * LANES:(k + 1) * LANES]
    hf = jnp.concatenate(
        [jnp.concatenate([perm_sc[k, pl.ds(a, SUBLANES, stride=pitch), :] for k in range(nl)], axis=1)
         for a in range(nv)], axis=0)
    hb = hf.astype(BF16)
    nblk = D_FF // FF_BLOCK
    first_sublane = lax.broadcasted_iota(jnp.int32, (SUBLANES, FF_BLOCK), 0) == 0

    def up_project(blk, half):
        return jnp.dot(hb, wup_ref[:, half * D_FF + blk * FF_BLOCK:half * D_FF + (blk + 1) * FF_BLOCK],
                       preferred_element_type=F32)

    def down_project(act, blk):
        down = jnp.dot(act, wdown_ref[blk * FF_BLOCK:(blk + 1) * FF_BLOCK, :], preferred_element_type=F32)
        if blk == 0:
            acc_sc[...] = down
        else:
            acc_sc[...] += down

    def conv(u, blk, half):
        cols = slice(half * D_FF + blk * FF_BLOCK, half * D_FF + (blk + 1) * FF_BLOCK)
        prev = carry_sc[:, cols]
        last1 = u[tm - SUBLANES:tm, :]
        last2 = u[tm - 2 * SUBLANES:tm - SUBLANES, :]
        lead1 = jnp.where(first_sublane, prev[SUBLANES - 1:SUBLANES, :], pltpu.roll(last1, 1, 0))
        lead2 = jnp.where(first_sublane, prev[SUBLANES - 2:SUBLANES - 1, :], pltpu.roll(last2, 1, 0))
        carry_sc[SUBLANES - 2:SUBLANES - 1, cols] = last2[SUBLANES - 1:SUBLANES, :]
        carry_sc[SUBLANES - 1:SUBLANES, cols] = last1[SUBLANES - 1:SUBLANES, :]
        back1 = jnp.concatenate([lead1, u[0:tm - SUBLANES, :]], axis=0)
        older = [u[0:tm - 2 * SUBLANES, :]] if nv > 2 else []
        back2 = jnp.concatenate([lead2, lead1] + older, axis=0)
        w = wconv_ref[:, cols]
        return bconv_ref[:, cols] + back2 * w[0:1, :] + back1 * w[1:2, :] + u * w[2:3, :]

    u_val, u_gate = up_project(0, 0), up_project(0, 1)
    act_prev = None
    for blk in range(nblk):
        more = blk + 1 < nblk
        if more:
            u_val_next = up_project(blk + 1, 0)
        val = conv(u_val, blk, 0)
        if more:
            u_gate_next = up_project(blk + 1, 1)
        gate = conv(u_gate, blk, 1)
        act = (gate * jax.nn.sigmoid(gate) * val).astype(BF16)
        if act_prev is not None:
            down_project(act_prev, blk - 1)
        act_prev = act
        if more:
            u_val, u_gate = u_val_next, u_gate_next
    down_project(act_prev, nblk - 1)

    y = _layer_norm(alpha * hf + acc_sc[...], ln2g_ref[...], ln2b_ref[...])
    for k in range(nl):
        for a in range(nv):
            perm_sc[k, pl.ds(a, SUBLANES, stride=pitch), :] = (
                y[a * SUBLANES:(a + 1) * SUBLANES, k * LANES:(k + 1) * LANES])
        for s in range(SUBLANES):
            y_ref[0, s * nv:(s + 1) * nv, k * LANES:(k + 1) * LANES] = perm_sc[k, s * pitch:s * pitch + nv, :]

    @pl.when(t == nt - 1)
    def _store_rows():
        rows_ref[0] = carry_sc[...]


def _const_spec(shape):
    zeros = (0,) * len(shape)
    return pl.BlockSpec(shape, lambda b, t: zeros, pipeline_mode=pl.Buffered(1))


def _batch_spec(shape):
    zeros = (0,) * (len(shape) - 1)
    return pl.BlockSpec(shape, lambda b, t: (b,) + zeros)


def _mixer_call(x, pos, cn0, m0, kc, vc, wts, *, ch, tm, has_cache, alpha):
    bsz, seq, _ = x.shape
    nt = seq // tm
    nch = tm // ch
    rows_out = min(WINDOW, seq)
    assert seq % tm == 0 and tm % ch == 0 and rows_out <= tm and (nt == 1 or tm >= WINDOW)

    half = A_HEAD_DIM // 2
    inv = ROPE_THETA ** (-jnp.arange(half, dtype=F32) / half)
    ang = pos.astype(F32)[:, None] * inv[None, :]
    cos, sin = jnp.cos(ang), jnp.sin(ang)
    cos_t = jnp.tile(cos, (1, LANES // half))
    sin_t = jnp.tile(jnp.concatenate([-sin, sin], axis=-1), (1, LANES // A_HEAD_DIM))

    body = functools.partial(_mixer_body, ch=ch, tm=tm, nt=nt, has_cache=has_cache, rows_out=rows_out,
                             alpha=alpha)
    tile3 = pl.BlockSpec((1, tm, D_MODEL), lambda b, t: (b, t, 0))
    state_shape = (1, M_HEADS, M_HEAD_DIM, 2 * M_HEAD_DIM)
    in_specs = [
        pl.BlockSpec(memory_space=pltpu.SMEM),
        tile3,
        pl.BlockSpec((tm, LANES), lambda b, t: (t, 0)),
        pl.BlockSpec((tm, LANES), lambda b, t: (t, 0)),
        _const_spec((D_MODEL, MAIN_WIDTH)),
        _const_spec((D_MODEL, LANES)),
        _const_spec((1, LANES)),
        _const_spec((1, M_WIDTH)),
        _const_spec((M_WIDTH + A_WIDTH, D_MODEL)),
        _const_spec((1, D_MODEL)),
        _const_spec((1, D_MODEL)),
        _batch_spec(state_shape),
        _batch_spec((1, 1, LANES)),
        _batch_spec((1, WINDOW, KV_WIDTH)),
        _batch_spec((1, WINDOW, KV_WIDTH)),
    ]
    out_specs = [
        tile3,
        _batch_spec(state_shape),
        _batch_spec((1, 1, LANES)),
        _batch_spec((1, rows_out, KV_WIDTH)),
        _batch_spec((1, rows_out, KV_WIDTH)),
    ]
    out_shape = [
        jax.ShapeDtypeStruct((bsz, seq, D_MODEL), F32),
        jax.ShapeDtypeStruct((bsz,) + state_shape[1:], F32),
        jax.ShapeDtypeStruct((bsz, 1, LANES), F32),
        jax.ShapeDtypeStruct((bsz, rows_out, KV_WIDTH), F32),
        jax.ShapeDtypeStruct((bsz, rows_out, KV_WIDTH), F32),
    ]
    scratch = [
        pltpu.VMEM((tm, M_WIDTH), BF16),
        pltpu.VMEM((tm, M_WIDTH), BF16),
        pltpu.VMEM((tm, M_WIDTH), F32),
        pltpu.VMEM((M_HEADS, tm, 2 * M_HEAD_DIM), BF16),
        pltpu.VMEM((tm, M_WIDTH), F32),
        pltpu.VMEM((A_Q_HEADS, tm, A_HEAD_DIM), BF16),
        pltpu.VMEM((A_KV_HEADS, WINDOW + tm, A_HEAD_DIM), BF16),
        pltpu.VMEM((A_KV_HEADS, WINDOW + tm, 2 * LANES), BF16),
        pltpu.VMEM((A_KV_HEADS, WINDOW + tm, 2 * LANES), BF16),
        pltpu.VMEM((M_HEADS, tm, LANES), F32),
        pltpu.VMEM((M_HEADS, tm, LANES), F32),
        pltpu.VMEM((M_HEADS, tm, LANES), F32),
        pltpu.VMEM((M_HEADS, nch, LANES), F32),
        pltpu.VMEM((M_HEADS, nch, ch, ch), F32),
        pltpu.VMEM((tm, M_WIDTH), F32),
        pltpu.VMEM((tm, M_WIDTH + A_WIDTH), BF16),
        pltpu.VMEM((M_HEADS, M_HEAD_DIM, 2 * M_HEAD_DIM), F32),
        pltpu.VMEM((1, LANES), F32),
    ]
    return pl.pallas_call(
        body,
        grid=(bsz, nt),
        in_specs=in_specs,
        out_specs=out_specs,
        out_shape=out_shape,
        scratch_shapes=scratch,
        compiler_params=pltpu.CompilerParams(dimension_semantics=("arbitrary", "arbitrary"),
                                             vmem_limit_bytes=VMEM_LIMIT_BYTES),
    )(wts["sinks"], x, cos_t, sin_t, wts["w_main"], wts["w_gate"], wts["gate_bias"], wts["g_norm"],
      wts["w_out"], wts["ln1_g"], wts["ln1_b"], cn0, m0, kc, vc)


def _ffn_call(h, conv_prev, wts, *, tm, alpha):
    bsz, seq, _ = h.shape
    nt = seq // tm
    assert seq % tm == 0 and tm % SUBLANES == 0
    body = functools.partial(_ffn_body, tm=tm, nt=nt, alpha=alpha)
    nl = D_MODEL // LANES
    tile3 = pl.BlockSpec((1, tm, D_MODEL), lambda b, t: (b, t, 0))
    return pl.pallas_call(
        body,
        grid=(bsz, nt),
        in_specs=[
            tile3,
            _batch_spec((1, SUBLANES, UP_WIDTH)),
            _const_spec((D_MODEL, UP_WIDTH)),
            _const_spec((CONV_W, UP_WIDTH)),
            _const_spec((1, UP_WIDTH)),
            _const_spec((D_FF, D_MODEL)),
            _const_spec((1, D_MODEL)),
            _const_spec((1, D_MODEL)),
        ],
        out_specs=[tile3, _batch_spec((1, SUBLANES, UP_WIDTH))],
        out_shape=[jax.ShapeDtypeStruct((bsz, seq, D_MODEL), F32),
                   jax.ShapeDtypeStruct((bsz, SUBLANES, UP_WIDTH), F32)],
        scratch_shapes=[pltpu.VMEM((SUBLANES, UP_WIDTH), F32), pltpu.VMEM((tm, D_MODEL), F32),
                        pltpu.VMEM((nl, SUBLANES * (tm // SUBLANES + 1), LANES), F32)],
        compiler_params=pltpu.CompilerParams(dimension_semantics=("arbitrary", "arbitrary"),
                                             vmem_limit_bytes=VMEM_LIMIT_BYTES),
    )(h, conv_prev, wts["w_up"], wts["w_conv"], wts["b_conv"], wts["w_down"], wts["ln2_g"], wts["ln2_b"])


def _prep_weights(w_in, b_igate, b_fgate, g_mlstm_norm, attn_sinks, w_out, ln1_g, ln1_b,
                  w_up, w_conv, b_conv, w_down, ln2_g, ln2_b):
    gate_lo = 4 * M_WIDTH
    gate_hi = gate_lo + 2 * M_HEADS
    w_main = jnp.concatenate([w_in[:, :gate_lo], w_in[:, gate_hi:]], axis=1).astype(BF16)
    w_gate = jnp.pad(w_in[:, gate_lo:gate_hi], ((0, 0), (0, LANES - 2 * M_HEADS))).astype(BF16)
    gate_bias = jnp.pad(jnp.concatenate([b_igate, b_fgate]).astype(F32), (0, LANES - 2 * M_HEADS))
    row = lambda a: a.astype(F32).reshape(1, -1)
    return dict(
        sinks=attn_sinks.astype(F32), w_main=w_main, w_gate=w_gate, gate_bias=row(gate_bias),
        g_norm=row(g_mlstm_norm), w_out=w_out.astype(BF16), ln1_g=row(ln1_g), ln1_b=row(ln1_b),
        w_up=w_up.astype(BF16), w_conv=w_conv.astype(F32), b_conv=row(b_conv), w_down=w_down.astype(BF16),
        ln2_g=row(ln2_g), ln2_b=row(ln2_b))


def _stream_layer(x, pos, c0, n0, m0, kc, vc, conv_prev, wts, *, ch, tm, has_cache, alpha):
    bsz, seq, _ = x.shape
    n_rep = jnp.broadcast_to(n0[..., None], (bsz, M_HEADS, M_HEAD_DIM, M_HEAD_DIM))
    cn0 = jnp.concatenate([c0, n_rep], axis=-1).astype(F32)
    m0 = jnp.pad(m0.astype(F32), ((0, 0), (GATE_LANE, LANES - GATE_LANE - M_HEADS))).reshape(bsz, 1, LANES)
    kc = kc.reshape(bsz, WINDOW, KV_WIDTH)
    vc = vc.reshape(bsz, WINDOW, KV_WIDTH)
    prev8 = jnp.pad(conv_prev, ((0, 0), (SUBLANES - (CONV_W - 1), 0), (0, 0)))
    h, cn1, m1, k_rows, v_rows = _mixer_call(x, pos, cn0, m0, kc, vc, wts, ch=ch, tm=tm,
                                             has_cache=has_cache, alpha=alpha)
    y, rows8 = _ffn_call(h, prev8, wts, tm=tm, alpha=alpha)
    rows_out = k_rows.shape[1]
    state = (cn1[..., :M_HEAD_DIM], cn1[..., M_HEAD_DIM], m1[:, 0, GATE_LANE:GATE_LANE + M_HEADS],
             k_rows.reshape(bsz, rows_out, A_KV_HEADS, A_HEAD_DIM),
             v_rows.reshape(bsz, rows_out, A_KV_HEADS, A_HEAD_DIM),
             rows8[:, SUBLANES - (CONV_W - 1):, :])
    return y, state


def kernel(x_prompt, x_sample, state_mlstm_C, state_mlstm_n, state_mlstm_m, cache_swa_k, cache_swa_v,
           state_conv, w_in, b_igate, b_fgate, g_mlstm_norm, attn_sinks, w_out, ln1_g, ln1_b,
           w_up, w_conv, b_conv, w_down, ln2_g, ln2_b):
    depth = w_in.shape[0]
    alpha = (2 * depth) ** 0.25
    bp, tp = x_prompt.shape[0], x_prompt.shape[1]
    bs, ts = x_sample.shape[0], x_sample.shape[1]
    pos_p = jnp.arange(tp, dtype=jnp.int32)
    pos_s = PAST_LEN + jnp.arange(ts, dtype=jnp.int32)
    tm_p = min(tp, 512)
    hp, hs = x_prompt, x_sample
    sp, ss = [], []
    for l in range(depth):
        wts = _prep_weights(w_in[l], b_igate[l], b_fgate[l], g_mlstm_norm[l], attn_sinks[l], w_out[l],
                            ln1_g[l], ln1_b[l], w_up[l], w_conv[l], b_conv[l], w_down[l], ln2_g[l], ln2_b[l])
        hp, st_p = _stream_layer(
            hp, pos_p,
            jnp.zeros((bp, M_HEADS, M_HEAD_DIM, M_HEAD_DIM), F32),
            jnp.zeros((bp, M_HEADS, M_HEAD_DIM), F32),
            jnp.zeros((bp, M_HEADS), F32),
            jnp.zeros((bp, WINDOW, A_KV_HEADS, A_HEAD_DIM), F32),
            jnp.zeros((bp, WINDOW, A_KV_HEADS, A_HEAD_DIM), F32),
            jnp.zeros((bp, CONV_W - 1, UP_WIDTH), F32),
            wts, ch=min(CHUNK, tp), tm=tm_p, has_cache=False, alpha=alpha)
        hs, st_s = _stream_layer(
            hs, pos_s, state_mlstm_C[l], state_mlstm_n[l], state_mlstm_m[l],
            cache_swa_k[l], cache_swa_v[l], state_conv[l],
            wts, ch=ts, tm=ts, has_cache=True, alpha=alpha)
        sp.append(st_p)
        ss.append(st_s)
    P = [jnp.stack([s[i] for s in sp]) for i in range(6)]
    S = [jnp.stack([s[i] for s in ss]) for i in range(6)]
    return (hp, hs, P[0], P[1], P[2], P[3], P[4], P[5], S[0], S[1], S[2], S[3], S[4], S[5])
```

```python
import functools

import jax
import jax.numpy as jnp
from jax import lax
from jax.experimental import pallas as pl
from jax.experimental.pallas import tpu as pltpu

D_MODEL = 1024
M_HEADS = 4
M_HEAD_DIM = 128
M_WIDTH = M_HEADS * M_HEAD_DIM
A_Q_HEADS = 8
A_KV_HEADS = 2
A_HEAD_DIM = 64
A_GROUP = A_Q_HEADS // A_KV_HEADS
A_WIDTH = A_Q_HEADS * A_HEAD_DIM
KV_WIDTH = A_KV_HEADS * A_HEAD_DIM
WINDOW = 128
CHUNK = 64
ROPE_THETA = 10000.0
D_FF = 2816
UP_WIDTH = 2 * D_FF
CONV_W = 3
LN_EPS = 1e-5
PAST_LEN = 2048

LANES = 128
SUBLANES = 8
GATE_LANE = M_HEADS
MAIN_WIDTH = 4 * M_WIDTH + A_WIDTH + 2 * KV_WIDTH
FF_BLOCK = 256
OUT_PIECE = 256
assert M_HEADS == A_Q_HEADS // 2
VMEM_LIMIT_BYTES = 56 * 1024 * 1024

F32 = jnp.float32
BF16 = jnp.bfloat16
NEG_INF = float("-inf")


def _layer_norm(y, g, b):
    mu = jnp.mean(y, axis=-1, keepdims=True)
    d = y - mu
    var = jnp.mean(d * d, axis=-1, keepdims=True)
    return d * lax.rsqrt(var + LN_EPS) * g + b


def _log_sigmoid(x):
    return jnp.minimum(x, 0.0) - jnp.log1p(jnp.exp(-jnp.abs(x)))


def _chunk_scan(x, group, *, reverse=False, use_max=False):
    rows = x.shape[0]
    pos = lax.broadcasted_iota(jnp.int32, x.shape, 0) % group
    fill = NEG_INF if use_max else 0.0
    acc = x
    k = 1
    while k < group:
        if reverse:
            shifted = jnp.where(pos < group - k, pltpu.roll(acc, rows - k, 0), fill)
        else:
            shifted = jnp.where(pos >= k, pltpu.roll(acc, k, 0), fill)
        acc = jnp.maximum(acc, shifted) if use_max else acc + shifted
        k *= 2
    return acc


def _rope(x, cos, sin_signed):
    lane = lax.broadcasted_iota(jnp.int32, x.shape, 1) % A_HEAD_DIM
    partner = jnp.where(lane < A_HEAD_DIM // 2,
                        pltpu.roll(x, LANES - A_HEAD_DIM // 2, 1),
                        pltpu.roll(x, A_HEAD_DIM // 2, 1))
    return x * cos + partner * sin_signed


def _mixer_body(sinks_ref, x_ref, cos_ref, sin_ref, wmain_ref, wgate_ref, gbias_ref, gnorm_ref, wout_ref,
                ln1g_ref, ln1b_ref, cn0_ref, m0_ref, kc_ref, vc_ref,
                h_ref, cnout_ref, mout_ref, kout_ref, vout_ref,
                q_sc, kb_sc, kf_sc, vaug_sc, o_sc, qa_sc, ka_sc, ve_sc, vo_sc,
                wi_sc, em_sc, wk_sc, dec_sc, d_sc, hh_sc, mix_sc, cn_sc, m_sc,
                *, ch, tm, nt, has_cache, rows_out, alpha):
    t = pl.program_id(1)
    nch = tm // ch
    band = WINDOW + ch
    left = lax.broadcasted_iota(jnp.int32, (1, LANES), 1) < A_HEAD_DIM
    ones_left = jnp.where(left, 1.0, 0.0).astype(BF16)
    ones_right = jnp.where(left, 0.0, 1.0).astype(BF16)

    def store_values(row_slice, nrows, va):
        swapped = pltpu.roll(va, A_HEAD_DIM, 1)
        ve_sc[0, row_slice, 0:LANES] = jnp.where(left, va, 0.0).astype(BF16)
        vo_sc[0, row_slice, 0:LANES] = jnp.where(left, 0.0, swapped).astype(BF16)
        ve_sc[1, row_slice, 0:LANES] = jnp.where(left, swapped, 0.0).astype(BF16)
        vo_sc[1, row_slice, 0:LANES] = jnp.where(left, 0.0, va).astype(BF16)
        for j in range(A_KV_HEADS):
            ve_sc[j, row_slice, LANES:2 * LANES] = jnp.broadcast_to(ones_left, (nrows, LANES))
            vo_sc[j, row_slice, LANES:2 * LANES] = jnp.broadcast_to(ones_right, (nrows, LANES))

    @pl.when(t == 0)
    def _load_state():
        cn_sc[...] = cn0_ref[0]
        m_sc[...] = m0_ref[0]
        kc = kc_ref[0].astype(BF16)
        for j in range(A_KV_HEADS):
            ka_sc[j, 0:WINDOW, :] = kc[:, j * A_HEAD_DIM:(j + 1) * A_HEAD_DIM]
        store_values(slice(0, WINDOW), WINDOW, vc_ref[0])

    if nt > 1:
        @pl.when(t > 0)
        def _slide_window():
            for j in range(A_KV_HEADS):
                ka_sc[j, 0:WINDOW, :] = ka_sc[j, tm:tm + WINDOW, :]
                ve_sc[j, 0:WINDOW, :] = ve_sc[j, tm:tm + WINDOW, :]
                vo_sc[j, 0:WINDOW, :] = vo_sc[j, tm:tm + WINDOW, :]

    xb = x_ref[0].astype(BF16)

    def proj(lo, hi):
        return jnp.dot(xb, wmain_ref[:, lo:hi], preferred_element_type=F32)

    gates = jnp.dot(xb, wgate_ref[...], preferred_element_type=F32) + gbias_ref[...]
    q_sc[...] = proj(0, M_WIDTH).astype(BF16)

    log_i = pltpu.roll(gates, GATE_LANE, 1)
    log_f = _log_sigmoid(gates)
    b_incl = _chunk_scan(log_f, ch)
    tail = _chunk_scan(log_f, ch, reverse=True) - log_f
    g_row = tail + log_i
    r_val = log_i - b_incl

    kf = proj(M_WIDTH, 2 * M_WIDTH) * (M_HEAD_DIM ** -0.5)
    kf_sc[...] = kf
    kb_sc[...] = kf.astype(BF16)

    r_cummax = _chunk_scan(r_val, ch, use_max=True)
    pos = lax.broadcasted_iota(jnp.int32, (nch, ch, LANES), 1)
    b_last = jnp.max(jnp.where(pos == ch - 1, b_incl.reshape(nch, ch, LANES), NEG_INF), axis=1)
    g_max = jnp.max(g_row.reshape(nch, ch, LANES), axis=1)
    m_run = m_sc[...]
    m_starts = []
    for c in range(nch):
        m_starts.append(m_run)
        m_run = jnp.maximum(b_last[c:c + 1, :] + m_run, g_max[c:c + 1, :])
    m_sc[...] = m_run
    m_start = jnp.concatenate(m_starts, axis=0)
    m_next = jnp.concatenate(m_starts[1:] + [m_run], axis=0)
    decay = jnp.exp(b_last + m_start - m_next)

    vm = proj(2 * M_WIDTH, 3 * M_WIDTH).astype(BF16)
    for hd in range(M_HEADS):
        vaug_sc[hd, :, 0:M_HEAD_DIM] = vm[:, hd * M_HEAD_DIM:(hd + 1) * M_HEAD_DIM]
        vaug_sc[hd, :, M_HEAD_DIM:2 * M_HEAD_DIM] = jnp.ones((tm, M_HEAD_DIM), BF16)

    def per_row(a):
        return jnp.broadcast_to(a[:, None, :], (nch, ch, LANES)).reshape(tm, LANES)

    m_rows = per_row(m_start)
    mx = jnp.maximum(m_rows, r_cummax)
    w_inter = jnp.exp(m_rows - mx)
    inv_floor = jnp.exp(-(b_incl + mx))
    w_state = jnp.exp(g_row - per_row(m_next))
    r_t = jnp.transpose(r_val)
    causal = (lax.broadcasted_iota(jnp.int32, (ch, ch), 0) >= lax.broadcasted_iota(jnp.int32, (ch, ch), 1))

    def spread_head(hd):
        ln = GATE_LANE + hd
        wi_sc[hd] = jnp.broadcast_to(w_inter[:, ln:ln + 1], (tm, LANES))
        em_sc[hd] = jnp.broadcast_to(inv_floor[:, ln:ln + 1], (tm, LANES))
        wk_sc[hd] = jnp.broadcast_to(w_state[:, ln:ln + 1], (tm, LANES))
        dec_sc[hd] = jnp.broadcast_to(decay[:, ln:ln + 1], (nch, LANES))
        mx_b = jnp.broadcast_to(mx[:, ln:ln + 1], (tm, ch))
        for c in range(nch):
            r_row = r_t[ln:ln + 1, c * ch:(c + 1) * ch]
            d_sc[hd, c] = jnp.where(causal, jnp.exp(r_row - mx_b[c * ch:(c + 1) * ch, :]), 0.0)

    o_sc[...] = jax.nn.sigmoid(proj(3 * M_WIDTH, 4 * M_WIDTH))
    cos = cos_ref[...]
    sin_signed = sin_ref[...]
    base = 4 * M_WIDTH
    for p in range(A_Q_HEADS // 2):
        spread_head(p)
        slab = _rope(proj(base + p * LANES, base + (p + 1) * LANES), cos, sin_signed)
        slab = (slab * (A_HEAD_DIM ** -0.5)).astype(BF16)
        qa_sc[2 * p] = slab[:, 0:A_HEAD_DIM]
        qa_sc[2 * p + 1] = slab[:, A_HEAD_DIM:LANES]
    base += A_WIDTH
    ka = _rope(proj(base, base + KV_WIDTH), cos, sin_signed)
    va = proj(base + KV_WIDTH, base + 2 * KV_WIDTH)
    kab = ka.astype(BF16)
    for j in range(A_KV_HEADS):
        ka_sc[j, WINDOW:WINDOW + tm, :] = kab[:, j * A_HEAD_DIM:(j + 1) * A_HEAD_DIM]
    store_values(slice(WINDOW, WINDOW + tm), tm, va)
    kout_ref[0] = ka[tm - rows_out:tm, :]
    vout_ref[0] = va[tm - rows_out:tm, :]

    band_pos = lax.broadcasted_iota(jnp.int32, (1, band), 1)
    hcols = [slice(hd * M_HEAD_DIM, (hd + 1) * M_HEAD_DIM) for hd in range(M_HEADS)]

    def head_norm(hh_rows, gate_rows, mix_rows):
        hh = [hh_sc[hh_rows, hcols[hd]] for hd in range(M_HEADS)]
        mu = [jnp.mean(a, axis=-1, keepdims=True) for a in hh]
        dlt = [hh[hd] - mu[hd] for hd in range(M_HEADS)]
        var = [jnp.mean(a * a, axis=-1, keepdims=True) for a in dlt]
        for hd in range(M_HEADS):
            hn = dlt[hd] * lax.rsqrt(var[hd] + LN_EPS) * gnorm_ref[:, hcols[hd]]
            mix_sc[mix_rows, hcols[hd]] = (o_sc[gate_rows, hcols[hd]] * hn).astype(BF16)

    hh_sc[0:ch, :] = jnp.zeros((ch, M_WIDTH), F32)

    def chunk_step(c, carry):
        r0 = pl.multiple_of(c * ch, ch)
        rows = pl.ds(r0, ch)
        out_rows = pl.ds(r0 + ch, ch)

        nt_dims = (((1,), (1,)), ((), ()))
        heads = range(M_HEADS)
        qb = [q_sc[rows, hcols[hd]] for hd in heads]
        vaug = [vaug_sc[hd, rows, :] for hd in heads]
        state = [cn_sc[hd] for hd in heads]
        qk = [lax.dot_general(qb[hd], kb_sc[rows, hcols[hd]], nt_dims, preferred_element_type=F32)
              for hd in heads]
        inter = [jnp.dot(qb[hd], state[hd].astype(BF16), preferred_element_type=F32)
                 for hd in heads]
        kw = [(kf_sc[rows, hcols[hd]] * wk_sc[hd, rows, :]).astype(BF16) for hd in heads]
        upd = [lax.dot_general(kw[hd], vaug[hd], (((0,), (0,)), ((), ())), preferred_element_type=F32)
               for hd in heads]

        if not has_cache:
            first_valid = WINDOW - (t * nch + c) * ch
            mask_bias = jnp.where(band_pos >= first_valid, 0.0, NEG_INF)
        qheads = range(A_Q_HEADS)
        kband = [ka_sc[j, pl.ds(r0, band), :] for j in range(A_KV_HEADS)]
        sc = [lax.dot_general(qa_sc[hq, rows, :], kband[hq // A_GROUP], nt_dims, preferred_element_type=F32)
              for hq in qheads]
        if not has_cache:
            sc = [a + mask_bias for a in sc]

        head_norm(rows, pl.ds(pl.multiple_of(jnp.maximum(r0 - ch, 0), ch), ch), rows)

        s = [(qk[hd] * d_sc[hd, c]).astype(BF16) for hd in heads]
        mx_col = [jnp.maximum(jnp.max(sc[hq], axis=-1, keepdims=True), sinks_ref[hq]) for hq in qheads]
        intra = [jnp.dot(s[hd], vaug[hd], preferred_element_type=F32) for hd in heads]
        mx_wide = [jnp.broadcast_to(mx_col[hq], (ch, band)) for hq in qheads]
        pr = [jnp.exp(sc[hq] - mx_wide[hq]).astype(BF16) for hq in qheads]
        sink_term = [jnp.exp(sinks_ref[hq] - mx_wide[hq][:, 0:LANES]) for hq in qheads]
        part = [jnp.dot(pr[hq], (ve_sc, vo_sc)[hq % 2][hq // A_GROUP, pl.ds(r0, band), :],
                        preferred_element_type=F32) for hq in qheads]

        for hd in heads:
            wi = wi_sc[hd, rows, :]
            num = intra[hd][:, 0:M_HEAD_DIM] + wi * inter[hd][:, 0:M_HEAD_DIM]
            den = intra[hd][:, M_HEAD_DIM:] + wi * inter[hd][:, M_HEAD_DIM:]
            hh_sc[out_rows, hcols[hd]] = num / jnp.maximum(jnp.abs(den), em_sc[hd, rows, :])
            dec = dec_sc[hd, pl.ds(c, 1), :]
            cn_sc[hd] = jnp.concatenate([dec, dec], axis=1) * state[hd] + upd[hd]

        for p in range(A_Q_HEADS // 2):
            acc = part[2 * p] + part[2 * p + 1]
            dn = acc[:, LANES:] + jnp.where(left, sink_term[2 * p], sink_term[2 * p + 1])
            mix_sc[out_rows, M_WIDTH + p * LANES:M_WIDTH + (p + 1) * LANES] = (acc[:, 0:LANES] / dn).astype(BF16)
        return carry

    lax.fori_loop(0, nch, chunk_step, 0)
    head_norm(slice(tm, tm + ch), slice(tm - ch, tm), slice(tm, tm + ch))

    piece = min(tm, OUT_PIECE)
    mixed = [jnp.dot(mix_sc[ch + lo:ch + lo + piece, :], wout_ref[...], preferred_element_type=F32)
             for lo in range(0, tm, piece)]
    for i, lo in enumerate(range(0, tm, piece)):
        h_ref[0, lo:lo + piece, :] = _layer_norm(alpha * x_ref[0, lo:lo + piece, :] + mixed[i],
                                                 ln1g_ref[...], ln1b_ref[...])

    @pl.when(t == nt - 1)
    def _store_state():
        cnout_ref[0] = cn_sc[...]
        mout_ref[0] = m_sc[...]


def _ffn_body(h_ref, prev_ref, wup_ref, wconv_ref, bconv_ref, wdown_ref, ln2g_ref, ln2b_ref,
              y_ref, rows_ref, carry_sc, acc_sc, perm_sc, *, tm, nt, alpha):
    t = pl.program_id(1)

    @pl.when(t == 0)
    def _load_carry():
        carry_sc[...] = prev_ref[0]

    nv = tm // SUBLANES
    pitch = nv + 1
    nl = D_MODEL // LANES
    for k in range(nl):
        for s in range(SUBLANES):
            perm_sc[k, s * pitch:s * pitch + nv, :] = h_ref[0, s * nv:(s + 1) * nv, k * LANES:(k + 1) * LANES]
    hf = jnp.concatenate(
        [jnp.concatenate([perm_sc[k, pl.ds(a, SUBLANES, stride=pitch), :] for k in range(nl)], axis=1)
         for a in range(nv)], axis=0)
    hb = hf.astype(BF16)
    nblk = D_FF // FF_BLOCK
    first_sublane = lax.broadcasted_iota(jnp.int32, (SUBLANES, FF_BLOCK), 0) == 0

    def up_project(blk, half):
        return jnp.dot(hb, wup_ref[:, half * D_FF + blk * FF_BLOCK:half * D_FF + (blk + 1) * FF_BLOCK],
                       preferred_element_type=F32)

    def down_project(act, blk):
        down = jnp.dot(act, wdown_ref[blk * FF_BLOCK:(blk + 1) * FF_BLOCK, :], preferred_element_type=F32)
        if blk == 0:
            acc_sc[...] = down
        else:
            acc_sc[...] += down

    def conv(u, blk, half):
        cols = slice(half * D_FF + blk * FF_BLOCK, half * D_FF + (blk + 1) * FF_BLOCK)
        prev = carry_sc[:, cols]
        last1 = u[tm - SUBLANES:tm, :]
        last2 = u[tm - 2 * SUBLANES:tm - SUBLANES, :]
        lead1 = jnp.where(first_sublane, prev[SUBLANES - 1:SUBLANES, :], pltpu.roll(last1, 1, 0))
        lead2 = jnp.where(first_sublane, prev[SUBLANES - 2:SUBLANES - 1, :], pltpu.roll(last2, 1, 0))
        carry_sc[SUBLANES - 2:SUBLANES - 1, cols] = last2[SUBLANES - 1:SUBLANES, :]
        carry_sc[SUBLANES - 1:SUBLANES, cols] = last1[SUBLANES - 1:SUBLANES, :]
        back1 = jnp.concatenate([lead1, u[0:tm - SUBLANES, :]], axis=0)
        older = [u[0:tm - 2 * SUBLANES, :]] if nv > 2 else []
        back2 = jnp.concatenate([lead2, lead1] + older, axis=0)
        w = wconv_ref[:, cols]
        return bconv_ref[:, cols] + back2 * w[0:1, :] + back1 * w[1:2, :] + u * w[2:3, :]

    u_val, u_gate = up_project(0, 0), up_project(0, 1)
    act_prev = None
    for blk in range(nblk):
        more = blk + 1 < nblk
        if more:
            u_val_next = up_project(blk + 1, 0)
        val = conv(u_val, blk, 0)
        if more:
            u_gate_next = up_project(blk + 1, 1)
        gate = conv(u_gate, blk, 1)
        act = (gate * jax.nn.sigmoid(gate) * val).astype(BF16)
        if act_prev is not None:
            down_project(act_prev, blk - 1)
        act_prev = act
        if more:
            u_val, u_gate = u_val_next, u_gate_next
    down_project(act_prev, nblk - 1)

    y = _layer_norm(alpha * hf + acc_sc[...], ln2g_ref[...], ln2b_ref[...])
    for k in range(nl):
        for a in range(nv):
            perm_sc[k, pl.ds(a, SUBLANES, stride=pitch), :] = (
                y[a * SUBLANES:(a + 1) * SUBLANES, k * LANES:(k + 1) * LANES])
        for s in range(SUBLANES):
            y_ref[0, s * nv:(s + 1) * nv, k * LANES:(k + 1) * LANES] = perm_sc[k, s * pitch:s * pitch + nv, :]

    @pl.when(t == nt - 1)
    def _store_rows():
        rows_ref[0] = carry_sc[...]


def _const_spec(shape):
    zeros = (0,) * len(shape)
    return pl.BlockSpec(shape, lambda b, t: zeros, pipeline_mode=pl.Buffered(1))


def _batch_spec(shape):
    zeros = (0,) * (len(shape) - 1)
    return pl.BlockSpec(shape, lambda b, t: (b,) + zeros)


def _mixer_call(x, pos, cn0, m0, kc, vc, wts, *, ch, tm, has_cache, alpha):
    bsz, seq, _ = x.shape
    nt = seq // tm
    nch = tm // ch
    rows_out = min(WINDOW, seq)
    assert seq % tm == 0 and tm % ch == 0 and rows_out <= tm and (nt == 1 or tm >= WINDOW)

    half = A_HEAD_DIM // 2
    inv = ROPE_THETA ** (-jnp.arange(half, dtype=F32) / half)
    ang = pos.astype(F32)[:, None] * inv[None, :]
    cos, sin = jnp.cos(ang), jnp.sin(ang)
    cos_t = jnp.tile(cos, (1, LANES // half))
    sin_t = jnp.tile(jnp.concatenate([-sin, sin], axis=-1), (1, LANES // A_HEAD_DIM))

    body = functools.partial(_mixer_body, ch=ch, tm=tm, nt=nt, has_cache=has_cache, rows_out=rows_out,
                             alpha=alpha)
    tile3 = pl.BlockSpec((1, tm, D_MODEL), lambda b, t: (b, t, 0))
    state_shape = (1, M_HEADS, M_HEAD_DIM, 2 * M_HEAD_DIM)
    in_specs = [
        pl.BlockSpec(memory_space=pltpu.SMEM),
        tile3,
        pl.BlockSpec((tm, LANES), lambda b, t: (t, 0)),
        pl.BlockSpec((tm, LANES), lambda b, t: (t, 0)),
        _const_spec((D_MODEL, MAIN_WIDTH)),
        _const_spec((D_MODEL, LANES)),
        _const_spec((1, LANES)),
        _const_spec((1, M_WIDTH)),
        _const_spec((M_WIDTH + A_WIDTH, D_MODEL)),
        _const_spec((1, D_MODEL)),
        _const_spec((1, D_MODEL)),
        _batch_spec(state_shape),
        _batch_spec((1, 1, LANES)),
        _batch_spec((1, WINDOW, KV_WIDTH)),
        _batch_spec((1, WINDOW, KV_WIDTH)),
    ]
    out_specs = [
        tile3,
        _batch_spec(state_shape),
        _batch_spec((1, 1, LANES)),
        _batch_spec((1, rows_out, KV_WIDTH)),
        _batch_spec((1, rows_out, KV_WIDTH)),
    ]
    out_shape = [
        jax.ShapeDtypeStruct((bsz, seq, D_MODEL), F32),
        jax.ShapeDtypeStruct((bsz,) + state_shape[1:], F32),
        jax.ShapeDtypeStruct((bsz, 1, LANES), F32),
        jax.ShapeDtypeStruct((bsz, rows_out, KV_WIDTH), F32),
        jax.ShapeDtypeStruct((bsz, rows_out, KV_WIDTH), F32),
    ]
    scratch = [
        pltpu.VMEM((tm, M_WIDTH), BF16),
        pltpu.VMEM((tm, M_WIDTH), BF16),
        pltpu.VMEM((tm, M_WIDTH), F32),
        pltpu.VMEM((M_HEADS, tm, 2 * M_HEAD_DIM), BF16),
        pltpu.VMEM((tm, M_WIDTH), F32),
        pltpu.VMEM((A_Q_HEADS, tm, A_HEAD_DIM), BF16),
        pltpu.VMEM((A_KV_HEADS, WINDOW + tm, A_HEAD_DIM), BF16),
        pltpu.VMEM((A_KV_HEADS, WINDOW + tm, 2 * LANES), BF16),
        pltpu.VMEM((A_KV_HEADS, WINDOW + tm, 2 * LANES), BF16),
        pltpu.VMEM((M_HEADS, tm, LANES), F32),
        pltpu.VMEM((M_HEADS, tm, LANES), F32),
        pltpu.VMEM((M_HEADS, tm, LANES), F32),
        pltpu.VMEM((M_HEADS, nch, LANES), F32),
        pltpu.VMEM((M_HEADS, nch, ch, ch), F32),
        pltpu.VMEM((ch + tm, M_WIDTH), F32),
        pltpu.VMEM((ch + tm, M_WIDTH + A_WIDTH), BF16),
        pltpu.VMEM((M_HEADS, M_HEAD_DIM, 2 * M_HEAD_DIM), F32),
        pltpu.VMEM((1, LANES), F32),
    ]
    return pl.pallas_call(
        body,
        grid=(bsz, nt),
        in_specs=in_specs,
        out_specs=out_specs,
        out_shape=out_shape,
        scratch_shapes=scratch,
        compiler_params=pltpu.CompilerParams(dimension_semantics=("arbitrary", "arbitrary"),
                                             vmem_limit_bytes=VMEM_LIMIT_BYTES),
    )(wts["sinks"], x, cos_t, sin_t, wts["w_main"], wts["w_gate"], wts["gate_bias"], wts["g_norm"],
      wts["w_out"], wts["ln1_g"], wts["ln1_b"], cn0, m0, kc, vc)


def _ffn_call(h, conv_prev, wts, *, tm, alpha):
    bsz, seq, _ = h.shape
    nt = seq // tm
    assert seq % tm == 0 and tm % SUBLANES == 0
    body = functools.partial(_ffn_body, tm=tm, nt=nt, alpha=alpha)
    nl = D_MODEL // LANES
    tile3 = pl.BlockSpec((1, tm, D_MODEL), lambda b, t: (b, t, 0))
    return pl.pallas_call(
        body,
        grid=(bsz, nt),
        in_specs=[
            tile3,
            _batch_spec((1, SUBLANES, UP_WIDTH)),
            _const_spec((D_MODEL, UP_WIDTH)),
            _const_spec((CONV_W, UP_WIDTH)),
            _const_spec((1, UP_WIDTH)),
            _const_spec((D_FF, D_MODEL)),
            _const_spec((1, D_MODEL)),
            _const_spec((1, D_MODEL)),
        ],
        out_specs=[tile3, _batch_spec((1, SUBLANES, UP_WIDTH))],
        out_shape=[jax.ShapeDtypeStruct((bsz, seq, D_MODEL), F32),
                   jax.ShapeDtypeStruct((bsz, SUBLANES, UP_WIDTH), F32)],
        scratch_shapes=[pltpu.VMEM((SUBLANES, UP_WIDTH), F32), pltpu.VMEM((tm, D_MODEL), F32),
                        pltpu.VMEM((nl, SUBLANES * (tm // SUBLANES + 1), LANES), F32)],
        compiler_params=pltpu.CompilerParams(dimension_semantics=("arbitrary", "arbitrary"),
                                             vmem_limit_bytes=VMEM_LIMIT_BYTES),
    )(h, conv_prev, wts["w_up"], wts["w_conv"], wts["b_conv"], wts["w_down"], wts["ln2_g"], wts["ln2_b"])


def _prep_weights(w_in, b_igate, b_fgate, g_mlstm_norm, attn_sinks, w_out, ln1_g, ln1_b,
                  w_up, w_conv, b_conv, w_down, ln2_g, ln2_b):
    gate_lo = 4 * M_WIDTH
    gate_hi = gate_lo + 2 * M_HEADS
    w_main = jnp.concatenate([w_in[:, :gate_lo], w_in[:, gate_hi:]], axis=1).astype(BF16)
    w_gate = jnp.pad(w_in[:, gate_lo:gate_hi], ((0, 0), (0, LANES - 2 * M_HEADS))).astype(BF16)
    gate_bias = jnp.pad(jnp.concatenate([b_igate, b_fgate]).astype(F32), (0, LANES - 2 * M_HEADS))
    row = lambda a: a.astype(F32).reshape(1, -1)
    return dict(
        sinks=attn_sinks.astype(F32), w_main=w_main, w_gate=w_gate, gate_bias=row(gate_bias),
        g_norm=row(g_mlstm_norm), w_out=w_out.astype(BF16), ln1_g=row(ln1_g), ln1_b=row(ln1_b),
        w_up=w_up.astype(BF16), w_conv=w_conv.astype(F32), b_conv=row(b_conv), w_down=w_down.astype(BF16),
        ln2_g=row(ln2_g), ln2_b=row(ln2_b))


def _stream_layer(x, pos, c0, n0, m0, kc, vc, conv_prev, wts, *, ch, tm, has_cache, alpha):
    bsz, seq, _ = x.shape
    n_rep = jnp.broadcast_to(n0[..., None], (bsz, M_HEADS, M_HEAD_DIM, M_HEAD_DIM))
    cn0 = jnp.concatenate([c0, n_rep], axis=-1).astype(F32)
    m0 = jnp.pad(m0.astype(F32), ((0, 0), (GATE_LANE, LANES - GATE_LANE - M_HEADS))).reshape(bsz, 1, LANES)
    kc = kc.reshape(bsz, WINDOW, KV_WIDTH)
    vc = vc.reshape(bsz, WINDOW, KV_WIDTH)
    prev8 = jnp.pad(conv_prev, ((0, 0), (SUBLANES - (CONV_W - 1), 0), (0, 0)))
    h, cn1, m1, k_rows, v_rows = _mixer_call(x, pos, cn0, m0, kc, vc, wts, ch=ch, tm=tm,
                                             has_cache=has_cache, alpha=alpha)
    y, rows8 = _ffn_call(h, prev8, wts, tm=tm, alpha=alpha)
    rows_out = k_rows.shape[1]
    state = (cn1[..., :M_HEAD_DIM], cn1[..., M_HEAD_DIM], m1[:, 0, GATE_LANE:GATE_LANE + M_HEADS],
             k_rows.reshape(bsz, rows_out, A_KV_HEADS, A_HEAD_DIM),
             v_rows.reshape(bsz, rows_out, A_KV_HEADS, A_HEAD_DIM),
             rows8[:, SUBLANES - (CONV_W - 1):, :])
    return y, state


def kernel(x_prompt, x_sample, state_mlstm_C, state_mlstm_n, state_mlstm_m, cache_swa_k, cache_swa_v,
           state_conv, w_in, b_igate, b_fgate, g_mlstm_norm, attn_sinks, w_out, ln1_g, ln1_b,
           w_up, w_conv, b_conv, w_down, ln2_g, ln2_b):
    depth = w_in.shape[0]
    alpha = (2 * depth) ** 0.25
    bp, tp = x_prompt.shape[0], x_prompt.shape[1]
    bs, ts = x_sample.shape[0], x_sample.shape[1]
    pos_p = jnp.arange(tp, dtype=jnp.int32)
    pos_s = PAST_LEN + jnp.arange(ts, dtype=jnp.int32)
    tm_p = min(tp, 512)
    hp, hs = x_prompt, x_sample
    sp, ss = [], []
    for l in range(depth):
        wts = _prep_weights(w_in[l], b_igate[l], b_fgate[l], g_mlstm_norm[l], attn_sinks[l], w_out[l],
                            ln1_g[l], ln1_b[l], w_up[l], w_conv[l], b_conv[l], w_down[l], ln2_g[l], ln2_b[l])
        hp, st_p = _stream_layer(
            hp, pos_p,
            jnp.zeros((bp, M_HEADS, M_HEAD_DIM, M_HEAD_DIM), F32),
            jnp.zeros((bp, M_HEADS, M_HEAD_DIM), F32),
            jnp.zeros((bp, M_HEADS), F32),
            jnp.zeros((bp, WINDOW, A_KV_HEADS, A_HEAD_DIM), F32),
            jnp.zeros((bp, WINDOW, A_KV_HEADS, A_HEAD_DIM), F32),
            jnp.zeros((bp, CONV_W - 1, UP_WIDTH), F32),
            wts, ch=min(CHUNK, tp), tm=tm_p, has_cache=False, alpha=alpha)
        hs, st_s = _stream_layer(
            hs, pos_s, state_mlstm_C[l], state_mlstm_n[l], state_mlstm_m[l],
            cache_swa_k[l], cache_swa_v[l], state_conv[l],
            wts, ch=ts, tm=ts, has_cache=True, alpha=alpha)
        sp.append(st_p)
        ss.append(st_s)
    P = [jnp.stack([s[i] for s in sp]) for i in range(6)]
    S = [jnp.stack([s[i] for s in ss]) for i in range(6)]
    return (hp, hs, P[0], P[1], P[2], P[3], P[4], P[5], S[0], S[1], S[2], S[3], S[4], S[5])
```

```python
import functools

import jax
import jax.numpy as jnp
from jax import lax
from jax.experimental import pallas as pl
from jax.experimental.pallas import tpu as pltpu

D_MODEL = 1024
M_HEADS = 4
M_HEAD_DIM = 128
M_WIDTH = M_HEADS * M_HEAD_DIM
A_Q_HEADS = 8
A_KV_HEADS = 2
A_HEAD_DIM = 64
A_GROUP = A_Q_HEADS // A_KV_HEADS
A_WIDTH = A_Q_HEADS * A_HEAD_DIM
KV_WIDTH = A_KV_HEADS * A_HEAD_DIM
WINDOW = 128
CHUNK = 64
ROPE_THETA = 10000.0
D_FF = 2816
UP_WIDTH = 2 * D_FF
CONV_W = 3
LN_EPS = 1e-5
PAST_LEN = 2048

LANES = 128
SUBLANES = 8
GATE_LANE = M_HEADS
MAIN_WIDTH = 4 * M_WIDTH + A_WIDTH + 2 * KV_WIDTH
FF_BLOCK = 256
OUT_PIECE = 256
CHUNKS_PER_STEP = 2
assert M_HEADS == A_Q_HEADS // 2
VMEM_LIMIT_BYTES = 56 * 1024 * 1024

F32 = jnp.float32
BF16 = jnp.bfloat16
NEG_INF = float("-inf")


def _layer_norm(y, g, b):
    mu = jnp.mean(y, axis=-1, keepdims=True)
    d = y - mu
    var = jnp.mean(d * d, axis=-1, keepdims=True)
    return d * lax.rsqrt(var + LN_EPS) * g + b


def _log_sigmoid(x):
    return jnp.minimum(x, 0.0) - jnp.log1p(jnp.exp(-jnp.abs(x)))


def _chunk_scan(x, group, *, reverse=False, use_max=False):
    rows = x.shape[0]
    pos = lax.broadcasted_iota(jnp.int32, x.shape, 0) % group
    fill = NEG_INF if use_max else 0.0
    acc = x
    k = 1
    while k < group:
        if reverse:
            shifted = jnp.where(pos < group - k, pltpu.roll(acc, rows - k, 0), fill)
        else:
            shifted = jnp.where(pos >= k, pltpu.roll(acc, k, 0), fill)
        acc = jnp.maximum(acc, shifted) if use_max else acc + shifted
        k *= 2
    return acc


def _chunks_per_step(nch):
    return max(k for k in range(1, CHUNKS_PER_STEP + 1) if nch % k == 0)


def _rope(x, cos, sin_signed):
    lane = lax.broadcasted_iota(jnp.int32, x.shape, 1) % A_HEAD_DIM
    partner = jnp.where(lane < A_HEAD_DIM // 2,
                        pltpu.roll(x, LANES - A_HEAD_DIM // 2, 1),
                        pltpu.roll(x, A_HEAD_DIM // 2, 1))
    return x * cos + partner * sin_signed


def _mixer_body(sinks_ref, x_ref, cos_ref, sin_ref, wmain_ref, wgate_ref, gbias_ref, gnorm_ref, wout_ref,
                ln1g_ref, ln1b_ref, cn0_ref, m0_ref, kc_ref, vc_ref,
                h_ref, cnout_ref, mout_ref, kout_ref, vout_ref,
                q_sc, kb_sc, kf_sc, vaug_sc, o_sc, qa_sc, ka_sc, ve_sc, vo_sc,
                wi_sc, em_sc, wk_sc, dec_sc, d_sc, hh_sc, mix_sc, cn_sc, m_sc,
                *, ch, tm, nt, has_cache, rows_out, alpha):
    t = pl.program_id(1)
    nch = tm // ch
    band = WINDOW + ch
    left = lax.broadcasted_iota(jnp.int32, (1, LANES), 1) < A_HEAD_DIM
    ones_left = jnp.where(left, 1.0, 0.0).astype(BF16)
    ones_right = jnp.where(left, 0.0, 1.0).astype(BF16)

    def store_values(row_slice, nrows, va):
        swapped = pltpu.roll(va, A_HEAD_DIM, 1)
        ve_sc[0, row_slice, 0:LANES] = jnp.where(left, va, 0.0).astype(BF16)
        vo_sc[0, row_slice, 0:LANES] = jnp.where(left, 0.0, swapped).astype(BF16)
        ve_sc[1, row_slice, 0:LANES] = jnp.where(left, swapped, 0.0).astype(BF16)
        vo_sc[1, row_slice, 0:LANES] = jnp.where(left, 0.0, va).astype(BF16)
        for j in range(A_KV_HEADS):
            ve_sc[j, row_slice, LANES:2 * LANES] = jnp.broadcast_to(ones_left, (nrows, LANES))
            vo_sc[j, row_slice, LANES:2 * LANES] = jnp.broadcast_to(ones_right, (nrows, LANES))

    @pl.when(t == 0)
    def _load_state():
        cn_sc[...] = cn0_ref[0]
        m_sc[...] = m0_ref[0]
        kc = kc_ref[0].astype(BF16)
        for j in range(A_KV_HEADS):
            ka_sc[j, 0:WINDOW, :] = kc[:, j * A_HEAD_DIM:(j + 1) * A_HEAD_DIM]
        store_values(slice(0, WINDOW), WINDOW, vc_ref[0])

    if nt > 1:
        @pl.when(t > 0)
        def _slide_window():
            for j in range(A_KV_HEADS):
                ka_sc[j, 0:WINDOW, :] = ka_sc[j, tm:tm + WINDOW, :]
                ve_sc[j, 0:WINDOW, :] = ve_sc[j, tm:tm + WINDOW, :]
                vo_sc[j, 0:WINDOW, :] = vo_sc[j, tm:tm + WINDOW, :]

    xb = x_ref[0].astype(BF16)

    def proj(lo, hi):
        return jnp.dot(xb, wmain_ref[:, lo:hi], preferred_element_type=F32)

    gates = jnp.dot(xb, wgate_ref[...], preferred_element_type=F32) + gbias_ref[...]
    q_sc[...] = proj(0, M_WIDTH).astype(BF16)

    log_i = pltpu.roll(gates, GATE_LANE, 1)
    log_f = _log_sigmoid(gates)
    b_incl = _chunk_scan(log_f, ch)
    tail = _chunk_scan(log_f, ch, reverse=True) - log_f
    g_row = tail + log_i
    r_val = log_i - b_incl

    kf = proj(M_WIDTH, 2 * M_WIDTH) * (M_HEAD_DIM ** -0.5)
    kf_sc[...] = kf
    kb_sc[...] = kf.astype(BF16)

    r_cummax = _chunk_scan(r_val, ch, use_max=True)
    pos = lax.broadcasted_iota(jnp.int32, (nch, ch, LANES), 1)
    b_last = jnp.max(jnp.where(pos == ch - 1, b_incl.reshape(nch, ch, LANES), NEG_INF), axis=1)
    g_max = jnp.max(g_row.reshape(nch, ch, LANES), axis=1)
    m_run = m_sc[...]
    m_starts = []
    for c in range(nch):
        m_starts.append(m_run)
        m_run = jnp.maximum(b_last[c:c + 1, :] + m_run, g_max[c:c + 1, :])
    m_sc[...] = m_run
    m_start = jnp.concatenate(m_starts, axis=0)
    m_next = jnp.concatenate(m_starts[1:] + [m_run], axis=0)
    decay = jnp.exp(b_last + m_start - m_next)

    vm = proj(2 * M_WIDTH, 3 * M_WIDTH).astype(BF16)
    for hd in range(M_HEADS):
        vaug_sc[hd, :, 0:M_HEAD_DIM] = vm[:, hd * M_HEAD_DIM:(hd + 1) * M_HEAD_DIM]
        vaug_sc[hd, :, M_HEAD_DIM:2 * M_HEAD_DIM] = jnp.ones((tm, M_HEAD_DIM), BF16)

    def per_row(a):
        return jnp.broadcast_to(a[:, None, :], (nch, ch, LANES)).reshape(tm, LANES)

    m_rows = per_row(m_start)
    mx = jnp.maximum(m_rows, r_cummax)
    w_inter = jnp.exp(m_rows - mx)
    inv_floor = jnp.exp(-(b_incl + mx))
    w_state = jnp.exp(g_row - per_row(m_next))
    r_t = jnp.transpose(r_val)
    causal = (lax.broadcasted_iota(jnp.int32, (ch, ch), 0) >= lax.broadcasted_iota(jnp.int32, (ch, ch), 1))

    def spread_head(hd):
        ln = GATE_LANE + hd
        wi_sc[hd] = jnp.broadcast_to(w_inter[:, ln:ln + 1], (tm, LANES))
        em_sc[hd] = jnp.broadcast_to(inv_floor[:, ln:ln + 1], (tm, LANES))
        wk_sc[hd] = jnp.broadcast_to(w_state[:, ln:ln + 1], (tm, LANES))
        dec_sc[hd] = jnp.broadcast_to(decay[:, ln:ln + 1], (nch, LANES))
        mx_b = jnp.broadcast_to(mx[:, ln:ln + 1], (tm, ch))
        for c in range(nch):
            r_row = r_t[ln:ln + 1, c * ch:(c + 1) * ch]
            d_sc[hd, c] = jnp.where(causal, jnp.exp(r_row - mx_b[c * ch:(c + 1) * ch, :]), 0.0)

    o_sc[...] = jax.nn.sigmoid(proj(3 * M_WIDTH, 4 * M_WIDTH))
    cos = cos_ref[...]
    sin_signed = sin_ref[...]
    base = 4 * M_WIDTH
    for p in range(A_Q_HEADS // 2):
        spread_head(p)
        slab = _rope(proj(base + p * LANES, base + (p + 1) * LANES), cos, sin_signed)
        slab = (slab * (A_HEAD_DIM ** -0.5)).astype(BF16)
        for e in range(2):
            pos = (p % 2) + 2 * e
            for c in range(nch):
                qa_sc[p // 2, c, pos * ch:(pos + 1) * ch, :] = (
                    slab[c * ch:(c + 1) * ch, e * A_HEAD_DIM:(e + 1) * A_HEAD_DIM])
    base += A_WIDTH
    ka = _rope(proj(base, base + KV_WIDTH), cos, sin_signed)
    va = proj(base + KV_WIDTH, base + 2 * KV_WIDTH)
    kab = ka.astype(BF16)
    for j in range(A_KV_HEADS):
        ka_sc[j, WINDOW:WINDOW + tm, :] = kab[:, j * A_HEAD_DIM:(j + 1) * A_HEAD_DIM]
    store_values(slice(WINDOW, WINDOW + tm), tm, va)
    kout_ref[0] = ka[tm - rows_out:tm, :]
    vout_ref[0] = va[tm - rows_out:tm, :]

    band_pos = lax.broadcasted_iota(jnp.int32, (1, band), 1)
    hcols = [slice(hd * M_HEAD_DIM, (hd + 1) * M_HEAD_DIM) for hd in range(M_HEADS)]

    def head_norm(hh_rows, gate_rows, mix_rows):
        hh = [hh_sc[hh_rows, hcols[hd]] for hd in range(M_HEADS)]
        mu = [jnp.mean(a, axis=-1, keepdims=True) for a in hh]
        dlt = [hh[hd] - mu[hd] for hd in range(M_HEADS)]
        var = [jnp.mean(a * a, axis=-1, keepdims=True) for a in dlt]
        for hd in range(M_HEADS):
            hn = dlt[hd] * lax.rsqrt(var[hd] + LN_EPS) * gnorm_ref[:, hcols[hd]]
            mix_sc[mix_rows, hcols[hd]] = (o_sc[gate_rows, hcols[hd]] * hn).astype(BF16)

    nsub = _chunks_per_step(nch)
    step = nsub * ch
    assert hh_sc.shape[0] == step + tm
    hh_sc[0:step, :] = jnp.zeros((step, M_WIDTH), F32)
    nt_dims = (((1,), (1,)), ((), ()))
    tn_dims = (((0,), (0,)), ((), ()))
    heads = range(M_HEADS)
    groups = range(A_KV_HEADS)
    subs = range(nsub)
    stack_order = [i for i in range(A_GROUP) if i % 2 == 0] + [i for i in range(A_GROUP) if i % 2 == 1]
    sink_cols = [jnp.concatenate([jnp.full((ch, 1), sinks_ref[A_GROUP * j + i], F32) for i in stack_order], axis=0)
                 for j in groups]

    def loop_step(i, carry):
        base_row = pl.multiple_of(i * step, step)
        r0 = [pl.multiple_of(base_row + u * ch, ch) for u in subs]
        rows = [pl.ds(r0[u], ch) for u in subs]
        out_rows = [pl.ds(r0[u] + step, ch) for u in subs]
        cidx = [i * nsub + u for u in subs]

        qb = [[q_sc[rows[u], hcols[hd]] for hd in heads] for u in subs]
        vaug = [[vaug_sc[hd, rows[u], :] for hd in heads] for u in subs]
        kw = [[(kf_sc[rows[u], hcols[hd]] * wk_sc[hd, rows[u], :]).astype(BF16) for hd in heads] for u in subs]
        upd = [[lax.dot_general(kw[u][hd], vaug[u][hd], tn_dims, preferred_element_type=F32) for hd in heads]
               for u in subs]
        qk = [[lax.dot_general(qb[u][hd], kb_sc[rows[u], hcols[hd]], nt_dims, preferred_element_type=F32)
               for hd in heads] for u in subs]

        kband = [[ka_sc[j, pl.ds(r0[u], band), :] for j in range(A_KV_HEADS)] for u in subs]
        sc = [[lax.dot_general(qa_sc[j, cidx[u]], kband[u][j], nt_dims, preferred_element_type=F32)
               for j in groups] for u in subs]
        if not has_cache:
            for u in subs:
                first_valid = WINDOW - (t * nch + cidx[u]) * ch
                mask_bias = jnp.where(band_pos >= first_valid, 0.0, NEG_INF)
                sc[u] = [a + mask_bias for a in sc[u]]

        head_norm(pl.ds(base_row, step), pl.ds(pl.multiple_of(jnp.maximum(base_row - step, 0), step), step),
                  pl.ds(base_row, step))

        state = [[cn_sc[hd] for hd in heads]]
        for u in subs:
            nxt = []
            for hd in heads:
                dec = dec_sc[hd, pl.ds(cidx[u], 1), :]
                nxt.append(jnp.concatenate([dec, dec], axis=1) * state[u][hd] + upd[u][hd])
            state.append(nxt)
        inter = [[jnp.dot(qb[u][hd], state[u][hd].astype(BF16), preferred_element_type=F32) for hd in heads]
                 for u in subs]
        for hd in heads:
            cn_sc[hd] = state[nsub][hd]

        s = [[(qk[u][hd] * d_sc[hd, cidx[u]]).astype(BF16) for hd in heads] for u in subs]
        mx_col = [[jnp.maximum(jnp.max(sc[u][j], axis=-1, keepdims=True), sink_cols[j]) for j in groups]
                  for u in subs]
        intra = [[jnp.dot(s[u][hd], vaug[u][hd], preferred_element_type=F32) for hd in heads]
                 for u in subs]
        mx_wide = [[jnp.broadcast_to(mx_col[u][j], (A_GROUP * ch, band)) for j in groups] for u in subs]
        pr = [[jnp.exp(sc[u][j] - mx_wide[u][j]).astype(BF16) for j in groups] for u in subs]
        sink_term = [[jnp.exp(sink_cols[j] - mx_wide[u][j][:, 0:LANES]) for j in groups] for u in subs]
        half = (A_GROUP // 2) * ch
        part = [[jnp.dot(pr[u][j][0:half], ve_sc[j, pl.ds(r0[u], band), :], preferred_element_type=F32)
                 + jnp.dot(pr[u][j][half:2 * half], vo_sc[j, pl.ds(r0[u], band), :], preferred_element_type=F32)
                 for j in groups] for u in subs]

        for u in subs:
            for hd in heads:
                wi = wi_sc[hd, rows[u], :]
                num = intra[u][hd][:, 0:M_HEAD_DIM] + wi * inter[u][hd][:, 0:M_HEAD_DIM]
                den = intra[u][hd][:, M_HEAD_DIM:] + wi * inter[u][hd][:, M_HEAD_DIM:]
                hh_sc[out_rows[u], hcols[hd]] = num / jnp.maximum(jnp.abs(den), em_sc[hd, rows[u], :])
        for u in subs:
            for j in groups:
                acc = part[u][j]
                dn = acc[:, LANES:] + jnp.where(left, sink_term[u][j][0:half], sink_term[u][j][half:2 * half])
                out = (acc[:, 0:LANES] / dn).astype(BF16)
                for k in range(A_GROUP // 2):
                    p = j * (A_GROUP // 2) + k
                    mix_sc[out_rows[u], M_WIDTH + p * LANES:M_WIDTH + (p + 1) * LANES] = (
                        out[k * ch:(k + 1) * ch, :])
        return carry

    lax.fori_loop(0, nch // nsub, loop_step, 0)
    head_norm(slice(tm, tm + step), slice(tm - step, tm), slice(tm, tm + step))

    piece = min(tm, OUT_PIECE)
    mixed = [jnp.dot(mix_sc[step + lo:step + lo + piece, :], wout_ref[...], preferred_element_type=F32)
             for lo in range(0, tm, piece)]
    for i, lo in enumerate(range(0, tm, piece)):
        h_ref[0, lo:lo + piece, :] = _layer_norm(alpha * x_ref[0, lo:lo + piece, :] + mixed[i],
                                                 ln1g_ref[...], ln1b_ref[...])

    @pl.when(t == nt - 1)
    def _store_state():
        cnout_ref[0] = cn_sc[...]
        mout_ref[0] = m_sc[...]


def _ffn_body(h_ref, prev_ref, wup_ref, wconv_ref, bconv_ref, wdown_ref, ln2g_ref, ln2b_ref,
              y_ref, rows_ref, carry_sc, acc_sc, perm_sc, *, tm, nt, alpha):
    t = pl.program_id(1)

    @pl.when(t == 0)
    def _load_carry():
        carry_sc[...] = prev_ref[0]

    nv = tm // SUBLANES
    pitch = nv + 1
    nl = D_MODEL // LANES
    for k in range(nl):
        for s in range(SUBLANES):
            perm_sc[k, s * pitch:s * pitch + nv, :] = h_ref[0, s * nv:(s + 1) * nv, k * LANES:(k + 1) * LANES]
    hf = jnp.concatenate(
        [jnp.concatenate([perm_sc[k, pl.ds(a, SUBLANES, stride=pitch), :] for k in range(nl)], axis=1)
         for a in range(nv)], axis=0)
    hb = hf.astype(BF16)
    nblk = D_FF // FF_BLOCK
    first_sublane = lax.broadcasted_iota(jnp.int32, (SUBLANES, FF_BLOCK), 0) == 0

    def up_project(blk, half):
        return jnp.dot(hb, wup_ref[:, half * D_FF + blk * FF_BLOCK:half * D_FF + (blk + 1) * FF_BLOCK],
                       preferred_element_type=F32)

    def down_project(act, blk):
        down = jnp.dot(act, wdown_ref[blk * FF_BLOCK:(blk + 1) * FF_BLOCK, :], preferred_element_type=F32)
        if blk == 0:
            acc_sc[...] = down
        else:
            acc_sc[...] += down

    def conv(u, blk, half):
        cols = slice(half * D_FF + blk * FF_BLOCK, half * D_FF + (blk + 1) * FF_BLOCK)
        prev = carry_sc[:, cols]
        last1 = u[tm - SUBLANES:tm, :]
        last2 = u[tm - 2 * SUBLANES:tm - SUBLANES, :]
        lead1 = jnp.where(first_sublane, prev[SUBLANES - 1:SUBLANES, :], pltpu.roll(last1, 1, 0))
        lead2 = jnp.where(first_sublane, prev[SUBLANES - 2:SUBLANES - 1, :], pltpu.roll(last2, 1, 0))
        carry_sc[SUBLANES - 2:SUBLANES - 1, cols] = last2[SUBLANES - 1:SUBLANES, :]
        carry_sc[SUBLANES - 1:SUBLANES, cols] = last1[SUBLANES - 1:SUBLANES, :]
        back1 = jnp.concatenate([lead1, u[0:tm - SUBLANES, :]], axis=0)
        older = [u[0:tm - 2 * SUBLANES, :]] if nv > 2 else []
        back2 = jnp.concatenate([lead2, lead1] + older, axis=0)
        w = wconv_ref[:, cols]
        return bconv_ref[:, cols] + back2 * w[0:1, :] + back1 * w[1:2, :] + u * w[2:3, :]

    u_val, u_gate = up_project(0, 0), up_project(0, 1)
    act_prev = None
    for blk in range(nblk):
        more = blk + 1 < nblk
        if more:
            u_val_next = up_project(blk + 1, 0)
        val = conv(u_val, blk, 0)
        if more:
            u_gate_next = up_project(blk + 1, 1)
        gate = conv(u_gate, blk, 1)
        act = (gate * jax.nn.sigmoid(gate) * val).astype(BF16)
        if act_prev is not None:
            down_project(act_prev, blk - 1)
        act_prev = act
        if more:
            u_val, u_gate = u_val_next, u_gate_next
    down_project(act_prev, nblk - 1)

    y = _layer_norm(alpha * hf + acc_sc[...], ln2g_ref[...], ln2b_ref[...])
    for k in range(nl):
        for a in range(nv):
            perm_sc[k, pl.ds(a, SUBLANES, stride=pitch), :] = (
                y[a * SUBLANES:(a + 1) * SUBLANES, k * LANES:(k + 1) * LANES])
        for s in range(SUBLANES):
            y_ref[0, s * nv:(s + 1) * nv, k * LANES:(k + 1) * LANES] = perm_sc[k, s * pitch:s * pitch + nv, :]

    @pl.when(t == nt - 1)
    def _store_rows():
        rows_ref[0] = carry_sc[...]


def _const_spec(shape):
    zeros = (0,) * len(shape)
    return pl.BlockSpec(shape, lambda b, t: zeros, pipeline_mode=pl.Buffered(1))


def _batch_spec(shape):
    zeros = (0,) * (len(shape) - 1)
    return pl.BlockSpec(shape, lambda b, t: (b,) + zeros)


def _mixer_call(x, pos, cn0, m0, kc, vc, wts, *, ch, tm, has_cache, alpha):
    bsz, seq, _ = x.shape
    nt = seq // tm
    nch = tm // ch
    rows_out = min(WINDOW, seq)
    assert seq % tm == 0 and tm % ch == 0 and rows_out <= tm and (nt == 1 or tm >= WINDOW)
    step = _chunks_per_step(nch) * ch

    half = A_HEAD_DIM // 2
    inv = ROPE_THETA ** (-jnp.arange(half, dtype=F32) / half)
    ang = pos.astype(F32)[:, None] * inv[None, :]
    cos, sin = jnp.cos(ang), jnp.sin(ang)
    cos_t = jnp.tile(cos, (1, LANES // half))
    sin_t = jnp.tile(jnp.concatenate([-sin, sin], axis=-1), (1, LANES // A_HEAD_DIM))

    body = functools.partial(_mixer_body, ch=ch, tm=tm, nt=nt, has_cache=has_cache, rows_out=rows_out,
                             alpha=alpha)
    tile3 = pl.BlockSpec((1, tm, D_MODEL), lambda b, t: (b, t, 0))
    state_shape = (1, M_HEADS, M_HEAD_DIM, 2 * M_HEAD_DIM)
    in_specs = [
        pl.BlockSpec(memory_space=pltpu.SMEM),
        tile3,
        pl.BlockSpec((tm, LANES), lambda b, t: (t, 0)),
        pl.BlockSpec((tm, LANES), lambda b, t: (t, 0)),
        _const_spec((D_MODEL, MAIN_WIDTH)),
        _const_spec((D_MODEL, LANES)),
        _const_spec((1, LANES)),
        _const_spec((1, M_WIDTH)),
        _const_spec((M_WIDTH + A_WIDTH, D_MODEL)),
        _const_spec((1, D_MODEL)),
        _const_spec((1, D_MODEL)),
        _batch_spec(state_shape),
        _batch_spec((1, 1, LANES)),
        _batch_spec((1, WINDOW, KV_WIDTH)),
        _batch_spec((1, WINDOW, KV_WIDTH)),
    ]
    out_specs = [
        tile3,
        _batch_spec(state_shape),
        _batch_spec((1, 1, LANES)),
        _batch_spec((1, rows_out, KV_WIDTH)),
        _batch_spec((1, rows_out, KV_WIDTH)),
    ]
    out_shape = [
        jax.ShapeDtypeStruct((bsz, seq, D_MODEL), F32),
        jax.ShapeDtypeStruct((bsz,) + state_shape[1:], F32),
        jax.ShapeDtypeStruct((bsz, 1, LANES), F32),
        jax.ShapeDtypeStruct((bsz, rows_out, KV_WIDTH), F32),
        jax.ShapeDtypeStruct((bsz, rows_out, KV_WIDTH), F32),
    ]
    scratch = [
        pltpu.VMEM((tm, M_WIDTH), BF16),
        pltpu.VMEM((tm, M_WIDTH), BF16),
        pltpu.VMEM((tm, M_WIDTH), F32),
        pltpu.VMEM((M_HEADS, tm, 2 * M_HEAD_DIM), BF16),
        pltpu.VMEM((tm, M_WIDTH), F32),
        pltpu.VMEM((A_KV_HEADS, nch, A_GROUP * ch, A_HEAD_DIM), BF16),
        pltpu.VMEM((A_KV_HEADS, WINDOW + tm, A_HEAD_DIM), BF16),
        pltpu.VMEM((A_KV_HEADS, WINDOW + tm, 2 * LANES), BF16),
        pltpu.VMEM((A_KV_HEADS, WINDOW + tm, 2 * LANES), BF16),
        pltpu.VMEM((M_HEADS, tm, LANES), F32),
        pltpu.VMEM((M_HEADS, tm, LANES), F32),
        pltpu.VMEM((M_HEADS, tm, LANES), F32),
        pltpu.VMEM((M_HEADS, nch, LANES), F32),
        pltpu.VMEM((M_HEADS, nch, ch, ch), F32),
        pltpu.VMEM((step + tm, M_WIDTH), F32),
        pltpu.VMEM((step + tm, M_WIDTH + A_WIDTH), BF16),
        pltpu.VMEM((M_HEADS, M_HEAD_DIM, 2 * M_HEAD_DIM), F32),
        pltpu.VMEM((1, LANES), F32),
    ]
    return pl.pallas_call(
        body,
        grid=(bsz, nt),
        in_specs=in_specs,
        out_specs=out_specs,
        out_shape=out_shape,
        scratch_shapes=scratch,
        compiler_params=pltpu.CompilerParams(dimension_semantics=("arbitrary", "arbitrary"),
                                             vmem_limit_bytes=VMEM_LIMIT_BYTES),
    )(wts["sinks"], x, cos_t, sin_t, wts["w_main"], wts["w_gate"], wts["gate_bias"], wts["g_norm"],
      wts["w_out"], wts["ln1_g"], wts["ln1_b"], cn0, m0, kc, vc)


def _ffn_call(h, conv_prev, wts, *, tm, alpha):
    bsz, seq, _ = h.shape
    nt = seq // tm
    assert seq % tm == 0 and tm % SUBLANES == 0
    body = functools.partial(_ffn_body, tm=tm, nt=nt, alpha=alpha)
    nl = D_MODEL // LANES
    tile3 = pl.BlockSpec((1, tm, D_MODEL), lambda b, t: (b, t, 0))
    return pl.pallas_call(
        body,
        grid=(bsz, nt),
        in_specs=[
            tile3,
            _batch_spec((1, SUBLANES, UP_WIDTH)),
            _const_spec((D_MODEL, UP_WIDTH)),
            _const_spec((CONV_W, UP_WIDTH)),
            _const_spec((1, UP_WIDTH)),
            _const_spec((D_FF, D_MODEL)),
            _const_spec((1, D_MODEL)),
            _const_spec((1, D_MODEL)),
        ],
        out_specs=[tile3, _batch_spec((1, SUBLANES, UP_WIDTH))],
        out_shape=[jax.ShapeDtypeStruct((bsz, seq, D_MODEL), F32),
                   jax.ShapeDtypeStruct((bsz, SUBLANES, UP_WIDTH), F32)],
        scratch_shapes=[pltpu.VMEM((SUBLANES, UP_WIDTH), F32), pltpu.VMEM((tm, D_MODEL), F32),
                        pltpu.VMEM((nl, SUBLANES * (tm // SUBLANES + 1), LANES), F32)],
        compiler_params=pltpu.CompilerParams(dimension_semantics=("arbitrary", "arbitrary"),
                                             vmem_limit_bytes=VMEM_LIMIT_BYTES),
    )(h, conv_prev, wts["w_up"], wts["w_conv"], wts["b_conv"], wts["w_down"], wts["ln2_g"], wts["ln2_b"])


def _prep_weights(w_in, b_igate, b_fgate, g_mlstm_norm, attn_sinks, w_out, ln1_g, ln1_b,
                  w_up, w_conv, b_conv, w_down, ln2_g, ln2_b):
    gate_lo = 4 * M_WIDTH
    gate_hi = gate_lo + 2 * M_HEADS
    w_main = jnp.concatenate([w_in[:, :gate_lo], w_in[:, gate_hi:]], axis=1).astype(BF16)
    w_gate = jnp.pad(w_in[:, gate_lo:gate_hi], ((0, 0), (0, LANES - 2 * M_HEADS))).astype(BF16)
    gate_bias = jnp.pad(jnp.concatenate([b_igate, b_fgate]).astype(F32), (0, LANES - 2 * M_HEADS))
    row = lambda a: a.astype(F32).reshape(1, -1)
    return dict(
        sinks=attn_sinks.astype(F32), w_main=w_main, w_gate=w_gate, gate_bias=row(gate_bias),
        g_norm=row(g_mlstm_norm), w_out=w_out.astype(BF16), ln1_g=row(ln1_g), ln1_b=row(ln1_b),
        w_up=w_up.astype(BF16), w_conv=w_conv.astype(F32), b_conv=row(b_conv), w_down=w_down.astype(BF16),
        ln2_g=row(ln2_g), ln2_b=row(ln2_b))


def _stream_layer(x, pos, c0, n0, m0, kc, vc, conv_prev, wts, *, ch, tm, has_cache, alpha):
    bsz, seq, _ = x.shape
    n_rep = jnp.broadcast_to(n0[..., None], (bsz, M_HEADS, M_HEAD_DIM, M_HEAD_DIM))
    cn0 = jnp.concatenate([c0, n_rep], axis=-1).astype(F32)
    m0 = jnp.pad(m0.astype(F32), ((0, 0), (GATE_LANE, LANES - GATE_LANE - M_HEADS))).reshape(bsz, 1, LANES)
    kc = kc.reshape(bsz, WINDOW, KV_WIDTH)
    vc = vc.reshape(bsz, WINDOW, KV_WIDTH)
    prev8 = jnp.pad(conv_prev, ((0, 0), (SUBLANES - (CONV_W - 1), 0), (0, 0)))
    h, cn1, m1, k_rows, v_rows = _mixer_call(x, pos, cn0, m0, kc, vc, wts, ch=ch, tm=tm,
                                             has_cache=has_cache, alpha=alpha)
    y, rows8 = _ffn_call(h, prev8, wts, tm=tm, alpha=alpha)
    rows_out = k_rows.shape[1]
    state = (cn1[..., :M_HEAD_DIM], cn1[..., M_HEAD_DIM], m1[:, 0, GATE_LANE:GATE_LANE + M_HEADS],
             k_rows.reshape(bsz, rows_out, A_KV_HEADS, A_HEAD_DIM),
             v_rows.reshape(bsz, rows_out, A_KV_HEADS, A_HEAD_DIM),
             rows8[:, SUBLANES - (CONV_W - 1):, :])
    return y, state


def kernel(x_prompt, x_sample, state_mlstm_C, state_mlstm_n, state_mlstm_m, cache_swa_k, cache_swa_v,
           state_conv, w_in, b_igate, b_fgate, g_mlstm_norm, attn_sinks, w_out, ln1_g, ln1_b,
           w_up, w_conv, b_conv, w_down, ln2_g, ln2_b):
    depth = w_in.shape[0]
    alpha = (2 * depth) ** 0.25
    bp, tp = x_prompt.shape[0], x_prompt.shape[1]
    bs, ts = x_sample.shape[0], x_sample.shape[1]
    pos_p = jnp.arange(tp, dtype=jnp.int32)
    pos_s = PAST_LEN + jnp.arange(ts, dtype=jnp.int32)
    tm_p = min(tp, 512)
    hp, hs = x_prompt, x_sample
    sp, ss = [], []
    for l in range(depth):
        wts = _prep_weights(w_in[l], b_igate[l], b_fgate[l], g_mlstm_norm[l], attn_sinks[l], w_out[l],
                            ln1_g[l], ln1_b[l], w_up[l], w_conv[l], b_conv[l], w_down[l], ln2_g[l], ln2_b[l])
        hp, st_p = _stream_layer(
            hp, pos_p,
            jnp.zeros((bp, M_HEADS, M_HEAD_DIM, M_HEAD_DIM), F32),
            jnp.zeros((bp, M_HEADS, M_HEAD_DIM), F32),
            jnp.zeros((bp, M_HEADS), F32),
            jnp.zeros((bp, WINDOW, A_KV_HEADS, A_HEAD_DIM), F32),
            jnp.zeros((bp, WINDOW, A_KV_HEADS, A_HEAD_DIM), F32),
            jnp.zeros((bp, CONV_W - 1, UP_WIDTH), F32),
            wts, ch=min(CHUNK, tp), tm=tm_p, has_cache=False, alpha=alpha)
        hs, st_s = _stream_layer(
            hs, pos_s, state_mlstm_C[l], state_mlstm_n[l], state_mlstm_m[l],
            cache_swa_k[l], cache_swa_v[l], state_conv[l],
            wts, ch=ts, tm=ts, has_cache=True, alpha=alpha)
        sp.append(st_p)
        ss.append(st_s)
    P = [jnp.stack([s[i] for s in sp]) for i in range(6)]
    S = [jnp.stack([s[i] for s in ss]) for i in range(6)]
    return (hp, hs, P[0], P[1], P[2], P[3], P[4], P[5], S[0], S[1], S[2], S[3], S[4], S[5])
```

```python
import functools

import jax
import jax.numpy as jnp
import numpy as np
from jax import lax
from jax.experimental import pallas as pl
from jax.experimental.pallas import tpu as pltpu

D_MODEL = 1024
M_HEADS = 4
M_HEAD_DIM = 128
M_WIDTH = M_HEADS * M_HEAD_DIM
A_Q_HEADS = 8
A_KV_HEADS = 2
A_HEAD_DIM = 64
A_GROUP = A_Q_HEADS // A_KV_HEADS
A_WIDTH = A_Q_HEADS * A_HEAD_DIM
KV_WIDTH = A_KV_HEADS * A_HEAD_DIM
WINDOW = 128
CHUNK = 64
ROPE_THETA = 10000.0
D_FF = 2816
UP_WIDTH = 2 * D_FF
CONV_W = 3
LN_EPS = 1e-5
PAST_LEN = 2048

LANES = 128
SUBLANES = 8
GATE_LANE = M_HEADS
MAIN_WIDTH = 4 * M_WIDTH + A_WIDTH + 2 * KV_WIDTH
FF_BLOCK = 256
OUT_PIECE = 256
CHUNKS_PER_STEP = 2
MIXER_ROWS = 512
FFN_ROWS = 1024
assert M_HEADS == A_Q_HEADS // 2
VMEM_LIMIT_BYTES = 56 * 1024 * 1024

F32 = jnp.float32
BF16 = jnp.bfloat16
NEG_INF = float("-inf")


def _layer_norm(y, g, b):
    mu = jnp.mean(y, axis=-1, keepdims=True)
    d = y - mu
    var = jnp.mean(d * d, axis=-1, keepdims=True)
    return d * lax.rsqrt(var + LN_EPS) * g + b


def _log_sigmoid(x):
    return jnp.minimum(x, 0.0) - jnp.log1p(jnp.exp(-jnp.abs(x)))


def _chunk_scan(x, group, *, reverse=False, use_max=False):
    rows = x.shape[0]
    pos = lax.broadcasted_iota(jnp.int32, x.shape, 0) % group
    fill = NEG_INF if use_max else 0.0
    acc = x
    k = 1
    while k < group:
        if reverse:
            shifted = jnp.where(pos < group - k, pltpu.roll(acc, rows - k, 0), fill)
        else:
            shifted = jnp.where(pos >= k, pltpu.roll(acc, k, 0), fill)
        acc = jnp.maximum(acc, shifted) if use_max else acc + shifted
        k *= 2
    return acc


def _chunks_per_step(nch):
    return max(k for k in range(1, CHUNKS_PER_STEP + 1) if nch % k == 0)


def _rope(x, cos, sin_signed):
    lane = lax.broadcasted_iota(jnp.int32, x.shape, 1) % A_HEAD_DIM
    partner = jnp.where(lane < A_HEAD_DIM // 2,
                        pltpu.roll(x, LANES - A_HEAD_DIM // 2, 1),
                        pltpu.roll(x, A_HEAD_DIM // 2, 1))
    return x * cos + partner * sin_signed


def _mixer_body(sinks_ref, x_ref, cos_ref, sin_ref, wmain_ref, wgate_ref, gbias_ref, gnorm_ref, wout_ref,
                ln1g_ref, ln1b_ref, cn0_ref, m0_ref, kc_ref, vc_ref,
                h_ref, cnout_ref, mout_ref, kout_ref, vout_ref,
                q_sc, kb_sc, kf_sc, vaug_sc, o_sc, qa_sc, ka_sc, ve_sc, vo_sc,
                wi_sc, em_sc, wk_sc, dec_sc, d_sc, hh_sc, mix_sc, cn_sc, m_sc,
                *, ch, tm, nt, has_cache, rows_out, alpha):
    t = pl.program_id(1)
    nch = tm // ch
    band = WINDOW + ch
    left = lax.broadcasted_iota(jnp.int32, (1, LANES), 1) < A_HEAD_DIM
    ones_left = jnp.where(left, 1.0, 0.0).astype(BF16)
    ones_right = jnp.where(left, 0.0, 1.0).astype(BF16)

    def store_values(row_slice, nrows, va):
        swapped = pltpu.roll(va, A_HEAD_DIM, 1)
        ve_sc[0, row_slice, 0:LANES] = jnp.where(left, va, 0.0).astype(BF16)
        vo_sc[0, row_slice, 0:LANES] = jnp.where(left, 0.0, swapped).astype(BF16)
        ve_sc[1, row_slice, 0:LANES] = jnp.where(left, swapped, 0.0).astype(BF16)
        vo_sc[1, row_slice, 0:LANES] = jnp.where(left, 0.0, va).astype(BF16)
        for j in range(A_KV_HEADS):
            ve_sc[j, row_slice, LANES:2 * LANES] = jnp.broadcast_to(ones_left, (nrows, LANES))
            vo_sc[j, row_slice, LANES:2 * LANES] = jnp.broadcast_to(ones_right, (nrows, LANES))

    @pl.when(t == 0)
    def _load_state():
        cn_sc[...] = cn0_ref[0]
        m_sc[...] = m0_ref[0]
        kc = kc_ref[0].astype(BF16)
        for j in range(A_KV_HEADS):
            ka_sc[j, 0:WINDOW, :] = kc[:, j * A_HEAD_DIM:(j + 1) * A_HEAD_DIM]
        store_values(slice(0, WINDOW), WINDOW, vc_ref[0])

    if nt > 1:
        @pl.when(t > 0)
        def _slide_window():
            for j in range(A_KV_HEADS):
                ka_sc[j, 0:WINDOW, :] = ka_sc[j, tm:tm + WINDOW, :]
                ve_sc[j, 0:WINDOW, :] = ve_sc[j, tm:tm + WINDOW, :]
                vo_sc[j, 0:WINDOW, :] = vo_sc[j, tm:tm + WINDOW, :]

    xb = x_ref[0].astype(BF16)

    def proj(lo, hi):
        return jnp.dot(xb, wmain_ref[:, lo:hi], preferred_element_type=F32)

    gates = jnp.dot(xb, wgate_ref[...], preferred_element_type=F32) + gbias_ref[...]
    q_sc[...] = proj(0, M_WIDTH).astype(BF16)

    log_i = pltpu.roll(gates, GATE_LANE, 1)
    log_f = _log_sigmoid(gates)
    b_incl = _chunk_scan(log_f, ch)
    tail = _chunk_scan(log_f, ch, reverse=True) - log_f
    g_row = tail + log_i
    r_val = log_i - b_incl

    kf = proj(M_WIDTH, 2 * M_WIDTH) * (M_HEAD_DIM ** -0.5)
    kf_sc[...] = kf
    kb_sc[...] = kf.astype(BF16)

    r_cummax = _chunk_scan(r_val, ch, use_max=True)
    pos = lax.broadcasted_iota(jnp.int32, (nch, ch, LANES), 1)
    b_last = jnp.max(jnp.where(pos == ch - 1, b_incl.reshape(nch, ch, LANES), NEG_INF), axis=1)
    g_max = jnp.max(g_row.reshape(nch, ch, LANES), axis=1)
    m_run = m_sc[...]
    m_starts = []
    for c in range(nch):
        m_starts.append(m_run)
        m_run = jnp.maximum(b_last[c:c + 1, :] + m_run, g_max[c:c + 1, :])
    m_sc[...] = m_run
    m_start = jnp.concatenate(m_starts, axis=0)
    m_next = jnp.concatenate(m_starts[1:] + [m_run], axis=0)
    decay = jnp.exp(b_last + m_start - m_next)

    vm = proj(2 * M_WIDTH, 3 * M_WIDTH).astype(BF16)
    for hd in range(M_HEADS):
        vaug_sc[hd, :, 0:M_HEAD_DIM] = vm[:, hd * M_HEAD_DIM:(hd + 1) * M_HEAD_DIM]
        vaug_sc[hd, :, M_HEAD_DIM:2 * M_HEAD_DIM] = jnp.ones((tm, M_HEAD_DIM), BF16)

    def per_row(a):
        return jnp.broadcast_to(a[:, None, :], (nch, ch, LANES)).reshape(tm, LANES)

    m_rows = per_row(m_start)
    mx = jnp.maximum(m_rows, r_cummax)
    w_inter = jnp.exp(m_rows - mx)
    inv_floor = jnp.exp(-(b_incl + mx))
    w_state = jnp.exp(g_row - per_row(m_next))
    r_t = jnp.transpose(r_val)
    causal = (lax.broadcasted_iota(jnp.int32, (ch, ch), 0) >= lax.broadcasted_iota(jnp.int32, (ch, ch), 1))

    def spread_head(hd):
        ln = GATE_LANE + hd
        wi_sc[hd] = jnp.broadcast_to(w_inter[:, ln:ln + 1], (tm, LANES))
        em_sc[hd] = jnp.broadcast_to(inv_floor[:, ln:ln + 1], (tm, LANES))
        wk_sc[hd] = jnp.broadcast_to(w_state[:, ln:ln + 1], (tm, LANES))
        dec_sc[hd] = jnp.broadcast_to(decay[:, ln:ln + 1], (nch, LANES))
        mx_b = jnp.broadcast_to(mx[:, ln:ln + 1], (tm, ch))
        for c in range(nch):
            r_row = r_t[ln:ln + 1, c * ch:(c + 1) * ch]
            d_sc[hd, c] = jnp.where(causal, jnp.exp(r_row - mx_b[c * ch:(c + 1) * ch, :]), 0.0)

    o_sc[...] = jax.nn.sigmoid(proj(3 * M_WIDTH, 4 * M_WIDTH))
    cos = cos_ref[...]
    sin_signed = sin_ref[...]
    base = 4 * M_WIDTH
    for p in range(A_Q_HEADS // 2):
        spread_head(p)
        slab = _rope(proj(base + p * LANES, base + (p + 1) * LANES), cos, sin_signed)
        slab = (slab * (A_HEAD_DIM ** -0.5)).astype(BF16)
        for e in range(2):
            pos = (p % 2) + 2 * e
            for c in range(nch):
                qa_sc[p // 2, c, pos * ch:(pos + 1) * ch, :] = (
                    slab[c * ch:(c + 1) * ch, e * A_HEAD_DIM:(e + 1) * A_HEAD_DIM])
    base += A_WIDTH
    ka = _rope(proj(base, base + KV_WIDTH), cos, sin_signed)
    va = proj(base + KV_WIDTH, base + 2 * KV_WIDTH)
    kab = ka.astype(BF16)
    for j in range(A_KV_HEADS):
        ka_sc[j, WINDOW:WINDOW + tm, :] = kab[:, j * A_HEAD_DIM:(j + 1) * A_HEAD_DIM]
    store_values(slice(WINDOW, WINDOW + tm), tm, va)
    kout_ref[0] = ka[tm - rows_out:tm, :]
    vout_ref[0] = va[tm - rows_out:tm, :]

    band_pos = lax.broadcasted_iota(jnp.int32, (1, band), 1)
    hcols = [slice(hd * M_HEAD_DIM, (hd + 1) * M_HEAD_DIM) for hd in range(M_HEADS)]

    def head_norm(hh_rows, gate_rows, mix_rows):
        hh = [hh_sc[hh_rows, hcols[hd]] for hd in range(M_HEADS)]
        mu = [jnp.mean(a, axis=-1, keepdims=True) for a in hh]
        dlt = [hh[hd] - mu[hd] for hd in range(M_HEADS)]
        var = [jnp.mean(a * a, axis=-1, keepdims=True) for a in dlt]
        for hd in range(M_HEADS):
            hn = dlt[hd] * lax.rsqrt(var[hd] + LN_EPS) * gnorm_ref[:, hcols[hd]]
            mix_sc[mix_rows, hcols[hd]] = (o_sc[gate_rows, hcols[hd]] * hn).astype(BF16)

    nsub = _chunks_per_step(nch)
    step = nsub * ch
    assert hh_sc.shape[0] == step + tm
    hh_sc[0:step, :] = jnp.zeros((step, M_WIDTH), F32)
    nt_dims = (((1,), (1,)), ((), ()))
    tn_dims = (((0,), (0,)), ((), ()))
    heads = range(M_HEADS)
    groups = range(A_KV_HEADS)
    subs = range(nsub)
    stack_order = [i for i in range(A_GROUP) if i % 2 == 0] + [i for i in range(A_GROUP) if i % 2 == 1]
    sink_cols = [jnp.concatenate([jnp.full((ch, 1), sinks_ref[A_GROUP * j + i], F32) for i in stack_order], axis=0)
                 for j in groups]

    def loop_step(i, carry):
        base_row = pl.multiple_of(i * step, step)
        r0 = [pl.multiple_of(base_row + u * ch, ch) for u in subs]
        rows = [pl.ds(r0[u], ch) for u in subs]
        out_rows = [pl.ds(r0[u] + step, ch) for u in subs]
        cidx = [i * nsub + u for u in subs]

        qb = [[q_sc[rows[u], hcols[hd]] for hd in heads] for u in subs]
        vaug = [[vaug_sc[hd, rows[u], :] for hd in heads] for u in subs]
        kw = [[(kf_sc[rows[u], hcols[hd]] * wk_sc[hd, rows[u], :]).astype(BF16) for hd in heads] for u in subs]
        upd = [[lax.dot_general(kw[u][hd], vaug[u][hd], tn_dims, preferred_element_type=F32) for hd in heads]
               for u in subs]
        qk = [[lax.dot_general(qb[u][hd], kb_sc[rows[u], hcols[hd]], nt_dims, preferred_element_type=F32)
               for hd in heads] for u in subs]

        kband = [[ka_sc[j, pl.ds(r0[u], band), :] for j in range(A_KV_HEADS)] for u in subs]
        sc = [[lax.dot_general(qa_sc[j, cidx[u]], kband[u][j], nt_dims, preferred_element_type=F32)
               for j in groups] for u in subs]
        if not has_cache:
            for u in subs:
                first_valid = WINDOW - (t * nch + cidx[u]) * ch
                mask_bias = jnp.where(band_pos >= first_valid, 0.0, NEG_INF)
                sc[u] = [a + mask_bias for a in sc[u]]

        head_norm(pl.ds(base_row, step), pl.ds(pl.multiple_of(jnp.maximum(base_row - step, 0), step), step),
                  pl.ds(base_row, step))

        state = [[cn_sc[hd] for hd in heads]]
        for u in subs:
            nxt = []
            for hd in heads:
                dec = dec_sc[hd, pl.ds(cidx[u], 1), :]
                nxt.append(jnp.concatenate([dec, dec], axis=1) * state[u][hd] + upd[u][hd])
            state.append(nxt)
        inter = [[jnp.dot(qb[u][hd], state[u][hd].astype(BF16), preferred_element_type=F32) for hd in heads]
                 for u in subs]
        for hd in heads:
            cn_sc[hd] = state[nsub][hd]

        s = [[(qk[u][hd] * d_sc[hd, cidx[u]]).astype(BF16) for hd in heads] for u in subs]
        mx_col = [[jnp.maximum(jnp.max(sc[u][j], axis=-1, keepdims=True), sink_cols[j]) for j in groups]
                  for u in subs]
        intra = [[jnp.dot(s[u][hd], vaug[u][hd], preferred_element_type=F32) for hd in heads]
                 for u in subs]
        mx_wide = [[jnp.broadcast_to(mx_col[u][j], (A_GROUP * ch, band)) for j in groups] for u in subs]
        pr = [[jnp.exp(sc[u][j] - mx_wide[u][j]).astype(BF16) for j in groups] for u in subs]
        sink_term = [[jnp.exp(sink_cols[j] - mx_wide[u][j][:, 0:LANES]) for j in groups] for u in subs]
        half = (A_GROUP // 2) * ch
        part = [[jnp.dot(pr[u][j][0:half], ve_sc[j, pl.ds(r0[u], band), :], preferred_element_type=F32)
                 + jnp.dot(pr[u][j][half:2 * half], vo_sc[j, pl.ds(r0[u], band), :], preferred_element_type=F32)
                 for j in groups] for u in subs]

        for u in subs:
            for hd in heads:
                wi = wi_sc[hd, rows[u], :]
                num = intra[u][hd][:, 0:M_HEAD_DIM] + wi * inter[u][hd][:, 0:M_HEAD_DIM]
                den = intra[u][hd][:, M_HEAD_DIM:] + wi * inter[u][hd][:, M_HEAD_DIM:]
                hh_sc[out_rows[u], hcols[hd]] = num / jnp.maximum(jnp.abs(den), em_sc[hd, rows[u], :])
        for u in subs:
            for j in groups:
                acc = part[u][j]
                dn = acc[:, LANES:] + jnp.where(left, sink_term[u][j][0:half], sink_term[u][j][half:2 * half])
                out = (acc[:, 0:LANES] / dn).astype(BF16)
                for k in range(A_GROUP // 2):
                    p = j * (A_GROUP // 2) + k
                    mix_sc[out_rows[u], M_WIDTH + p * LANES:M_WIDTH + (p + 1) * LANES] = (
                        out[k * ch:(k + 1) * ch, :])
        return carry

    lax.fori_loop(0, nch // nsub, loop_step, 0)
    head_norm(slice(tm, tm + step), slice(tm - step, tm), slice(tm, tm + step))

    piece = min(tm, OUT_PIECE)
    mixed = [jnp.dot(mix_sc[step + lo:step + lo + piece, :], wout_ref[...], preferred_element_type=F32)
             for lo in range(0, tm, piece)]
    for i, lo in enumerate(range(0, tm, piece)):
        h_ref[0, lo:lo + piece, :] = _layer_norm(alpha * x_ref[0, lo:lo + piece, :] + mixed[i],
                                                 ln1g_ref[...], ln1b_ref[...])

    @pl.when(t == nt - 1)
    def _store_state():
        cnout_ref[0] = cn_sc[...]
        mout_ref[0] = m_sc[...]


def _ffn_body(h_ref, prev_ref, wup_ref, wconv_ref, bconv_ref, wdown_ref, ln2g_ref, ln2b_ref,
              y_ref, rows_ref, carry_sc, acc_sc, perm_sc, *, tm, nt, alpha):
    t = pl.program_id(1)

    @pl.when(t == 0)
    def _load_carry():
        carry_sc[...] = prev_ref[0]

    nv = tm // SUBLANES
    pitch = nv + 1
    nl = D_MODEL // LANES
    for k in range(nl):
        for s in range(SUBLANES):
            perm_sc[k, s * pitch:s * pitch + nv, :] = h_ref[0, s * nv:(s + 1) * nv, k * LANES:(k + 1) * LANES]
    hf = jnp.concatenate(
        [jnp.concatenate([perm_sc[k, pl.ds(a, SUBLANES, stride=pitch), :] for k in range(nl)], axis=1)
         for a in range(nv)], axis=0)
    hb = hf.astype(BF16)
    nblk = D_FF // FF_BLOCK
    first_sublane = lax.broadcasted_iota(jnp.int32, (SUBLANES, FF_BLOCK), 0) == 0

    def up_project(blk, half):
        return jnp.dot(hb, wup_ref[:, half * D_FF + blk * FF_BLOCK:half * D_FF + (blk + 1) * FF_BLOCK],
                       preferred_element_type=F32)

    def down_project(act, blk):
        down = jnp.dot(act, wdown_ref[blk * FF_BLOCK:(blk + 1) * FF_BLOCK, :], preferred_element_type=F32)
        if blk == 0:
            acc_sc[...] = down
        else:
            acc_sc[...] += down

    def conv(u, blk, half):
        cols = slice(half * D_FF + blk * FF_BLOCK, half * D_FF + (blk + 1) * FF_BLOCK)
        prev = carry_sc[:, cols]
        last1 = u[tm - SUBLANES:tm, :]
        last2 = u[tm - 2 * SUBLANES:tm - SUBLANES, :]
        lead1 = jnp.where(first_sublane, prev[SUBLANES - 1:SUBLANES, :], pltpu.roll(last1, 1, 0))
        lead2 = jnp.where(first_sublane, prev[SUBLANES - 2:SUBLANES - 1, :], pltpu.roll(last2, 1, 0))
        carry_sc[SUBLANES - 2:SUBLANES - 1, cols] = last2[SUBLANES - 1:SUBLANES, :]
        carry_sc[SUBLANES - 1:SUBLANES, cols] = last1[SUBLANES - 1:SUBLANES, :]
        back1 = jnp.concatenate([lead1, u[0:tm - SUBLANES, :]], axis=0)
        older = [u[0:tm - 2 * SUBLANES, :]] if nv > 2 else []
        back2 = jnp.concatenate([lead2, lead1] + older, axis=0)
        w = wconv_ref[:, cols]
        return bconv_ref[:, cols] + back2 * w[0:1, :] + back1 * w[1:2, :] + u * w[2:3, :]

    u_val, u_gate = up_project(0, 0), up_project(0, 1)
    act_prev = None
    for blk in range(nblk):
        more = blk + 1 < nblk
        if more:
            u_val_next = up_project(blk + 1, 0)
        val = conv(u_val, blk, 0)
        if more:
            u_gate_next = up_project(blk + 1, 1)
        gate = conv(u_gate, blk, 1)
        act = (gate * jax.nn.sigmoid(gate) * val).astype(BF16)
        if act_prev is not None:
            down_project(act_prev, blk - 1)
        act_prev = act
        if more:
            u_val, u_gate = u_val_next, u_gate_next
    down_project(act_prev, nblk - 1)

    y = _layer_norm(alpha * hf + acc_sc[...], ln2g_ref[...], ln2b_ref[...])
    for k in range(nl):
        for a in range(nv):
            perm_sc[k, pl.ds(a, SUBLANES, stride=pitch), :] = (
                y[a * SUBLANES:(a + 1) * SUBLANES, k * LANES:(k + 1) * LANES])
        for s in range(SUBLANES):
            y_ref[0, s * nv:(s + 1) * nv, k * LANES:(k + 1) * LANES] = perm_sc[k, s * pitch:s * pitch + nv, :]

    @pl.when(t == nt - 1)
    def _store_rows():
        rows_ref[0] = carry_sc[...]


def _const_spec(shape):
    zeros = (0,) * len(shape)
    return pl.BlockSpec(shape, lambda b, t: zeros, pipeline_mode=pl.Buffered(1))


def _batch_spec(shape):
    zeros = (0,) * (len(shape) - 1)
    return pl.BlockSpec(shape, lambda b, t: (b,) + zeros)


def _mixer_call(x, pos0, cn0, m0, kc, vc, wts, *, ch, tm, has_cache, alpha):
    bsz, seq, _ = x.shape
    nt = seq // tm
    nch = tm // ch
    rows_out = min(WINDOW, seq)
    assert seq % tm == 0 and tm % ch == 0 and rows_out <= tm and (nt == 1 or tm >= WINDOW)
    step = _chunks_per_step(nch) * ch

    half = A_HEAD_DIM // 2
    inv = ROPE_THETA ** (-np.arange(half, dtype=np.float64) / half)
    ang = (pos0 + np.arange(seq, dtype=np.float64))[:, None] * inv[None, :]
    cos, sin = np.cos(ang), np.sin(ang)
    cos_t = jnp.asarray(np.tile(cos, (1, LANES // half)), F32)
    sin_t = jnp.asarray(np.tile(np.concatenate([-sin, sin], axis=-1), (1, LANES // A_HEAD_DIM)), F32)

    body = functools.partial(_mixer_body, ch=ch, tm=tm, nt=nt, has_cache=has_cache, rows_out=rows_out,
                             alpha=alpha)
    tile3 = pl.BlockSpec((1, tm, D_MODEL), lambda b, t: (b, t, 0))
    state_shape = (1, M_HEADS, M_HEAD_DIM, 2 * M_HEAD_DIM)
    in_specs = [
        pl.BlockSpec(memory_space=pltpu.SMEM),
        tile3,
        pl.BlockSpec((tm, LANES), lambda b, t: (t, 0)),
        pl.BlockSpec((tm, LANES), lambda b, t: (t, 0)),
        _const_spec((D_MODEL, MAIN_WIDTH)),
        _const_spec((D_MODEL, LANES)),
        _const_spec((1, LANES)),
        _const_spec((1, M_WIDTH)),
        _const_spec((M_WIDTH + A_WIDTH, D_MODEL)),
        _const_spec((1, D_MODEL)),
        _const_spec((1, D_MODEL)),
        _batch_spec(state_shape),
        _batch_spec((1, 1, LANES)),
        _batch_spec((1, WINDOW, KV_WIDTH)),
        _batch_spec((1, WINDOW, KV_WIDTH)),
    ]
    out_specs = [
        tile3,
        _batch_spec(state_shape),
        _batch_spec((1, 1, LANES)),
        _batch_spec((1, rows_out, KV_WIDTH)),
        _batch_spec((1, rows_out, KV_WIDTH)),
    ]
    out_shape = [
        jax.ShapeDtypeStruct((bsz, seq, D_MODEL), F32),
        jax.ShapeDtypeStruct((bsz,) + state_shape[1:], F32),
        jax.ShapeDtypeStruct((bsz, 1, LANES), F32),
        jax.ShapeDtypeStruct((bsz, rows_out, KV_WIDTH), F32),
        jax.ShapeDtypeStruct((bsz, rows_out, KV_WIDTH), F32),
    ]
    scratch = [
        pltpu.VMEM((tm, M_WIDTH), BF16),
        pltpu.VMEM((tm, M_WIDTH), BF16),
        pltpu.VMEM((tm, M_WIDTH), F32),
        pltpu.VMEM((M_HEADS, tm, 2 * M_HEAD_DIM), BF16),
        pltpu.VMEM((tm, M_WIDTH), F32),
        pltpu.VMEM((A_KV_HEADS, nch, A_GROUP * ch, A_HEAD_DIM), BF16),
        pltpu.VMEM((A_KV_HEADS, WINDOW + tm, A_HEAD_DIM), BF16),
        pltpu.VMEM((A_KV_HEADS, WINDOW + tm, 2 * LANES), BF16),
        pltpu.VMEM((A_KV_HEADS, WINDOW + tm, 2 * LANES), BF16),
        pltpu.VMEM((M_HEADS, tm, LANES), F32),
        pltpu.VMEM((M_HEADS, tm, LANES), F32),
        pltpu.VMEM((M_HEADS, tm, LANES), F32),
        pltpu.VMEM((M_HEADS, nch, LANES), F32),
        pltpu.VMEM((M_HEADS, nch, ch, ch), F32),
        pltpu.VMEM((step + tm, M_WIDTH), F32),
        pltpu.VMEM((step + tm, M_WIDTH + A_WIDTH), BF16),
        pltpu.VMEM((M_HEADS, M_HEAD_DIM, 2 * M_HEAD_DIM), F32),
        pltpu.VMEM((1, LANES), F32),
    ]
    return pl.pallas_call(
        body,
        grid=(bsz, nt),
        in_specs=in_specs,
        out_specs=out_specs,
        out_shape=out_shape,
        scratch_shapes=scratch,
        compiler_params=pltpu.CompilerParams(dimension_semantics=("arbitrary", "arbitrary"),
                                             vmem_limit_bytes=VMEM_LIMIT_BYTES),
    )(wts["sinks"], x, cos_t, sin_t, wts["w_main"], wts["w_gate"], wts["gate_bias"], wts["g_norm"],
      wts["w_out"], wts["ln1_g"], wts["ln1_b"], cn0, m0, kc, vc)


def _ffn_call(h, conv_prev, wts, *, tm, alpha):
    bsz, seq, _ = h.shape
    nt = seq // tm
    assert seq % tm == 0 and tm % SUBLANES == 0
    body = functools.partial(_ffn_body, tm=tm, nt=nt, alpha=alpha)
    nl = D_MODEL // LANES
    tile3 = pl.BlockSpec((1, tm, D_MODEL), lambda b, t: (b, t, 0))
    return pl.pallas_call(
        body,
        grid=(bsz, nt),
        in_specs=[
            tile3,
            _batch_spec((1, SUBLANES, UP_WIDTH)),
            _const_spec((D_MODEL, UP_WIDTH)),
            _const_spec((CONV_W, UP_WIDTH)),
            _const_spec((1, UP_WIDTH)),
            _const_spec((D_FF, D_MODEL)),
            _const_spec((1, D_MODEL)),
            _const_spec((1, D_MODEL)),
        ],
        out_specs=[tile3, _batch_spec((1, SUBLANES, UP_WIDTH))],
        out_shape=[jax.ShapeDtypeStruct((bsz, seq, D_MODEL), F32),
                   jax.ShapeDtypeStruct((bsz, SUBLANES, UP_WIDTH), F32)],
        scratch_shapes=[pltpu.VMEM((SUBLANES, UP_WIDTH), F32), pltpu.VMEM((tm, D_MODEL), F32),
                        pltpu.VMEM((nl, SUBLANES * (tm // SUBLANES + 1), LANES), F32)],
        compiler_params=pltpu.CompilerParams(dimension_semantics=("arbitrary", "arbitrary"),
                                             vmem_limit_bytes=VMEM_LIMIT_BYTES),
    )(h, conv_prev, wts["w_up"], wts["w_conv"], wts["b_conv"], wts["w_down"], wts["ln2_g"], wts["ln2_b"])


def _prep_weights(w_in, b_igate, b_fgate, g_mlstm_norm, attn_sinks, w_out, ln1_g, ln1_b,
                  w_up, w_conv, b_conv, w_down, ln2_g, ln2_b):
    gate_lo = 4 * M_WIDTH
    gate_hi = gate_lo + 2 * M_HEADS
    w_main = jnp.concatenate([w_in[:, :gate_lo], w_in[:, gate_hi:]], axis=1).astype(BF16)
    w_gate = jnp.pad(w_in[:, gate_lo:gate_hi], ((0, 0), (0, LANES - 2 * M_HEADS))).astype(BF16)
    gate_bias = jnp.pad(jnp.concatenate([b_igate, b_fgate]).astype(F32), (0, LANES - 2 * M_HEADS))
    row = lambda a: a.astype(F32).reshape(1, -1)
    return dict(
        sinks=attn_sinks.astype(F32), w_main=w_main, w_gate=w_gate, gate_bias=row(gate_bias),
        g_norm=row(g_mlstm_norm), w_out=w_out.astype(BF16), ln1_g=row(ln1_g), ln1_b=row(ln1_b),
        w_up=w_up.astype(BF16), w_conv=w_conv.astype(F32), b_conv=row(b_conv), w_down=w_down.astype(BF16),
        ln2_g=row(ln2_g), ln2_b=row(ln2_b))


def _stream_layer(x, pos0, c0, n0, m0, kc, vc, conv_prev, wts, *, ch, tm, has_cache, alpha):
    bsz, seq, _ = x.shape
    n_rep = jnp.broadcast_to(n0[..., None], (bsz, M_HEADS, M_HEAD_DIM, M_HEAD_DIM))
    cn0 = jnp.concatenate([c0, n_rep], axis=-1).astype(F32)
    m0 = jnp.pad(m0.astype(F32), ((0, 0), (GATE_LANE, LANES - GATE_LANE - M_HEADS))).reshape(bsz, 1, LANES)
    kc = kc.reshape(bsz, WINDOW, KV_WIDTH)
    vc = vc.reshape(bsz, WINDOW, KV_WIDTH)
    prev8 = jnp.pad(conv_prev, ((0, 0), (SUBLANES - (CONV_W - 1), 0), (0, 0)))
    h, cn1, m1, k_rows, v_rows = _mixer_call(x, pos0, cn0, m0, kc, vc, wts, ch=ch, tm=tm,
                                             has_cache=has_cache, alpha=alpha)
    y, rows8 = _ffn_call(h, prev8, wts, tm=min(seq, FFN_ROWS), alpha=alpha)
    rows_out = k_rows.shape[1]
    state = (cn1[..., :M_HEAD_DIM], cn1[..., M_HEAD_DIM], m1[:, 0, GATE_LANE:GATE_LANE + M_HEADS],
             k_rows.reshape(bsz, rows_out, A_KV_HEADS, A_HEAD_DIM),
             v_rows.reshape(bsz, rows_out, A_KV_HEADS, A_HEAD_DIM),
             rows8[:, SUBLANES - (CONV_W - 1):, :])
    return y, state


def kernel(x_prompt, x_sample, state_mlstm_C, state_mlstm_n, state_mlstm_m, cache_swa_k, cache_swa_v,
           state_conv, w_in, b_igate, b_fgate, g_mlstm_norm, attn_sinks, w_out, ln1_g, ln1_b,
           w_up, w_conv, b_conv, w_down, ln2_g, ln2_b):
    depth = w_in.shape[0]
    alpha = (2 * depth) ** 0.25
    bp, tp = x_prompt.shape[0], x_prompt.shape[1]
    bs, ts = x_sample.shape[0], x_sample.shape[1]
    tm_p = min(tp, MIXER_ROWS)
    hp, hs = x_prompt, x_sample
    sp, ss = [], []
    for l in range(depth):
        wts = _prep_weights(w_in[l], b_igate[l], b_fgate[l], g_mlstm_norm[l], attn_sinks[l], w_out[l],
                            ln1_g[l], ln1_b[l], w_up[l], w_conv[l], b_conv[l], w_down[l], ln2_g[l], ln2_b[l])
        hp, st_p = _stream_layer(
            hp, 0,
            jnp.zeros((bp, M_HEADS, M_HEAD_DIM, M_HEAD_DIM), F32),
            jnp.zeros((bp, M_HEADS, M_HEAD_DIM), F32),
            jnp.zeros((bp, M_HEADS), F32),
            jnp.zeros((bp, WINDOW, A_KV_HEADS, A_HEAD_DIM), F32),
            jnp.zeros((bp, WINDOW, A_KV_HEADS, A_HEAD_DIM), F32),
            jnp.zeros((bp, CONV_W - 1, UP_WIDTH), F32),
            wts, ch=min(CHUNK, tp), tm=tm_p, has_cache=False, alpha=alpha)
        hs, st_s = _stream_layer(
            hs, PAST_LEN, state_mlstm_C[l], state_mlstm_n[l], state_mlstm_m[l],
            cache_swa_k[l], cache_swa_v[l], state_conv[l],
            wts, ch=ts, tm=ts, has_cache=True, alpha=alpha)
        sp.append(st_p)
        ss.append(st_s)
    P = [jnp.stack([s[i] for s in sp]) for i in range(6)]
    S = [jnp.stack([s[i] for s in ss]) for i in range(6)]
    return (hp, hs, P[0], P[1], P[2], P[3], P[4], P[5], S[0], S[1], S[2], S[3], S[4], S[5])
```

```python
import functools

import jax
import jax.numpy as jnp
import numpy as np
from jax import lax
from jax.experimental import pallas as pl
from jax.experimental.pallas import tpu as pltpu

D_MODEL = 1024
M_HEADS = 4
M_HEAD_DIM = 128
M_WIDTH = M_HEADS * M_HEAD_DIM
A_Q_HEADS = 8
A_KV_HEADS = 2
A_HEAD_DIM = 64
A_GROUP = A_Q_HEADS // A_KV_HEADS
A_WIDTH = A_Q_HEADS * A_HEAD_DIM
KV_WIDTH = A_KV_HEADS * A_HEAD_DIM
WINDOW = 128
CHUNK = 64
ROPE_THETA = 10000.0
D_FF = 2816
UP_WIDTH = 2 * D_FF
CONV_W = 3
LN_EPS = 1e-5
PAST_LEN = 2048

LANES = 128
SUBLANES = 8
GATE_LANE = M_HEADS
MAIN_WIDTH = 4 * M_WIDTH + A_WIDTH + 2 * KV_WIDTH
FF_BLOCK = 256
OUT_PIECE = 256
CHUNKS_PER_STEP = 2
MIXER_ROWS = 512
FFN_ROWS = 1024
assert M_HEADS == A_Q_HEADS // 2
VMEM_LIMIT_BYTES = 56 * 1024 * 1024

F32 = jnp.float32
BF16 = jnp.bfloat16
NEG_INF = float("-inf")


def _layer_norm(y, g, b):
    mu = jnp.mean(y, axis=-1, keepdims=True)
    d = y - mu
    var = jnp.mean(d * d, axis=-1, keepdims=True)
    return d * lax.rsqrt(var + LN_EPS) * g + b


def _log_sigmoid(x):
    return jnp.minimum(x, 0.0) - jnp.log1p(jnp.exp(-jnp.abs(x)))


def _chunk_scan(x, group, *, reverse=False, use_max=False):
    rows = x.shape[0]
    pos = lax.broadcasted_iota(jnp.int32, x.shape, 0) % group
    fill = NEG_INF if use_max else 0.0
    acc = x
    k = 1
    while k < group:
        if reverse:
            shifted = jnp.where(pos < group - k, pltpu.roll(acc, rows - k, 0), fill)
        else:
            shifted = jnp.where(pos >= k, pltpu.roll(acc, k, 0), fill)
        acc = jnp.maximum(acc, shifted) if use_max else acc + shifted
        k *= 2
    return acc


def _chunks_per_step(nch):
    return max(k for k in range(1, CHUNKS_PER_STEP + 1) if nch % k == 0)


def _rope(x, cos, sin_signed):
    lane = lax.broadcasted_iota(jnp.int32, x.shape, 1) % A_HEAD_DIM
    partner = jnp.where(lane < A_HEAD_DIM // 2,
                        pltpu.roll(x, LANES - A_HEAD_DIM // 2, 1),
                        pltpu.roll(x, A_HEAD_DIM // 2, 1))
    return x * cos + partner * sin_signed


def _mixer_body(sinks_ref, x_ref, cos_ref, sin_ref, wmain_ref, wgate_ref, gbias_ref, gnorm_ref, wout_ref,
                ln1g_ref, ln1b_ref, cn0_ref, m0_ref, kc_ref, vc_ref,
                h_ref, cnout_ref, mout_ref, kout_ref, vout_ref,
                q_sc, kb_sc, kf_sc, vaug_sc, o_sc, qa_sc, ka_sc, ve_sc, vo_sc,
                wi_sc, em_sc, wk_sc, dec_sc, d_sc, hh_sc, mix_sc, cn_sc, m_sc,
                *, ch, tm, nt, has_cache, rows_out, alpha):
    t = pl.program_id(1)
    nch = tm // ch
    band = WINDOW + ch
    left = lax.broadcasted_iota(jnp.int32, (1, LANES), 1) < A_HEAD_DIM
    ones_left = jnp.where(left, 1.0, 0.0).astype(BF16)
    ones_right = jnp.where(left, 0.0, 1.0).astype(BF16)

    def store_values(row_slice, nrows, va):
        swapped = pltpu.roll(va, A_HEAD_DIM, 1)
        ve_sc[0, row_slice, 0:LANES] = jnp.where(left, va, 0.0).astype(BF16)
        vo_sc[0, row_slice, 0:LANES] = jnp.where(left, 0.0, swapped).astype(BF16)
        ve_sc[1, row_slice, 0:LANES] = jnp.where(left, swapped, 0.0).astype(BF16)
        vo_sc[1, row_slice, 0:LANES] = jnp.where(left, 0.0, va).astype(BF16)
        for j in range(A_KV_HEADS):
            ve_sc[j, row_slice, LANES:2 * LANES] = jnp.broadcast_to(ones_left, (nrows, LANES))
            vo_sc[j, row_slice, LANES:2 * LANES] = jnp.broadcast_to(ones_right, (nrows, LANES))

    @pl.when(t == 0)
    def _load_state():
        cn_sc[...] = cn0_ref[0]
        m_sc[...] = m0_ref[0]
        kc = kc_ref[0].astype(BF16)
        for j in range(A_KV_HEADS):
            ka_sc[j, 0:WINDOW, :] = kc[:, j * A_HEAD_DIM:(j + 1) * A_HEAD_DIM]
        store_values(slice(0, WINDOW), WINDOW, vc_ref[0])

    if nt > 1:
        @pl.when(t > 0)
        def _slide_window():
            for j in range(A_KV_HEADS):
                ka_sc[j, 0:WINDOW, :] = ka_sc[j, tm:tm + WINDOW, :]
                ve_sc[j, 0:WINDOW, :] = ve_sc[j, tm:tm + WINDOW, :]
                vo_sc[j, 0:WINDOW, :] = vo_sc[j, tm:tm + WINDOW, :]

    xb = x_ref[0].astype(BF16)

    def proj(lo, hi):
        return jnp.dot(xb, wmain_ref[:, lo:hi], preferred_element_type=F32)

    gates = jnp.dot(xb, wgate_ref[...], preferred_element_type=F32) + gbias_ref[...]
    cos = cos_ref[...]
    sin_signed = sin_ref[...]
    attn_base = 4 * M_WIDTH

    def project_queries(pair):
        wide = proj(attn_base + 2 * pair * LANES, attn_base + (2 * pair + 2) * LANES)
        for p in (2 * pair, 2 * pair + 1):
            slab = _rope(wide[:, (p % 2) * LANES:(p % 2 + 1) * LANES], cos, sin_signed)
            slab = (slab * (A_HEAD_DIM ** -0.5)).astype(BF16)
            for e in range(2):
                pos = (p % 2) + 2 * e
                for c in range(nch):
                    qa_sc[p // 2, c, pos * ch:(pos + 1) * ch, :] = (
                        slab[c * ch:(c + 1) * ch, e * A_HEAD_DIM:(e + 1) * A_HEAD_DIM])

    def project_keys_values():
        kv = proj(attn_base + A_WIDTH, attn_base + A_WIDTH + 2 * KV_WIDTH)
        ka = _rope(kv[:, 0:KV_WIDTH], cos, sin_signed)
        va = kv[:, KV_WIDTH:2 * KV_WIDTH]
        kab = ka.astype(BF16)
        for j in range(A_KV_HEADS):
            ka_sc[j, WINDOW:WINDOW + tm, :] = kab[:, j * A_HEAD_DIM:(j + 1) * A_HEAD_DIM]
        store_values(slice(WINDOW, WINDOW + tm), tm, va)
        kout_ref[0] = ka[tm - rows_out:tm, :]
        vout_ref[0] = va[tm - rows_out:tm, :]

    project_queries(0)

    log_i = pltpu.roll(gates, GATE_LANE, 1)
    log_f = _log_sigmoid(gates)
    b_incl = _chunk_scan(log_f, ch)
    tail = _chunk_scan(log_f, ch, reverse=True) - log_f
    g_row = tail + log_i
    r_val = log_i - b_incl

    project_queries(1)

    r_cummax = _chunk_scan(r_val, ch, use_max=True)
    pos = lax.broadcasted_iota(jnp.int32, (nch, ch, LANES), 1)
    b_last = jnp.max(jnp.where(pos == ch - 1, b_incl.reshape(nch, ch, LANES), NEG_INF), axis=1)
    g_max = jnp.max(g_row.reshape(nch, ch, LANES), axis=1)
    m_run = m_sc[...]
    m_starts = []
    for c in range(nch):
        m_starts.append(m_run)
        m_run = jnp.maximum(b_last[c:c + 1, :] + m_run, g_max[c:c + 1, :])
    m_sc[...] = m_run
    m_start = jnp.concatenate(m_starts, axis=0)
    m_next = jnp.concatenate(m_starts[1:] + [m_run], axis=0)
    decay = jnp.exp(b_last + m_start - m_next)

    project_keys_values()

    def per_row(a):
        return jnp.broadcast_to(a[:, None, :], (nch, ch, LANES)).reshape(tm, LANES)

    m_rows = per_row(m_start)
    mx = jnp.maximum(m_rows, r_cummax)
    w_inter = jnp.exp(m_rows - mx)
    inv_floor = jnp.exp(-(b_incl + mx))
    w_state = jnp.exp(g_row - per_row(m_next))
    r_t = jnp.transpose(r_val)
    causal = (lax.broadcasted_iota(jnp.int32, (ch, ch), 0) >= lax.broadcasted_iota(jnp.int32, (ch, ch), 1))

    def spread_head(hd):
        ln = GATE_LANE + hd
        wi_sc[hd] = jnp.broadcast_to(w_inter[:, ln:ln + 1], (tm, LANES))
        em_sc[hd] = jnp.broadcast_to(inv_floor[:, ln:ln + 1], (tm, LANES))
        wk_sc[hd] = jnp.broadcast_to(w_state[:, ln:ln + 1], (tm, LANES))
        dec_sc[hd] = jnp.broadcast_to(decay[:, ln:ln + 1], (nch, LANES))
        mx_b = jnp.broadcast_to(mx[:, ln:ln + 1], (tm, ch))
        for c in range(nch):
            r_row = r_t[ln:ln + 1, c * ch:(c + 1) * ch]
            d_sc[hd, c] = jnp.where(causal, jnp.exp(r_row - mx_b[c * ch:(c + 1) * ch, :]), 0.0)

    q_sc[...] = proj(0, M_WIDTH).astype(BF16)
    spread_head(0)
    kf = proj(M_WIDTH, 2 * M_WIDTH) * (M_HEAD_DIM ** -0.5)
    kf_sc[...] = kf
    kb_sc[...] = kf.astype(BF16)
    spread_head(1)
    vm = proj(2 * M_WIDTH, 3 * M_WIDTH).astype(BF16)
    for hd in range(M_HEADS):
        vaug_sc[hd, :, 0:M_HEAD_DIM] = vm[:, hd * M_HEAD_DIM:(hd + 1) * M_HEAD_DIM]
        vaug_sc[hd, :, M_HEAD_DIM:2 * M_HEAD_DIM] = jnp.ones((tm, M_HEAD_DIM), BF16)
    spread_head(2)
    spread_head(3)
    o_sc[...] = jax.nn.sigmoid(proj(3 * M_WIDTH, 4 * M_WIDTH))

    band_pos = lax.broadcasted_iota(jnp.int32, (1, band), 1)
    hcols = [slice(hd * M_HEAD_DIM, (hd + 1) * M_HEAD_DIM) for hd in range(M_HEADS)]

    def head_norm(hh_rows, gate_rows, mix_rows):
        hh = [hh_sc[hh_rows, hcols[hd]] for hd in range(M_HEADS)]
        mu = [jnp.mean(a, axis=-1, keepdims=True) for a in hh]
        dlt = [hh[hd] - mu[hd] for hd in range(M_HEADS)]
        var = [jnp.mean(a * a, axis=-1, keepdims=True) for a in dlt]
        for hd in range(M_HEADS):
            hn = dlt[hd] * lax.rsqrt(var[hd] + LN_EPS) * gnorm_ref[:, hcols[hd]]
            mix_sc[mix_rows, hcols[hd]] = (o_sc[gate_rows, hcols[hd]] * hn).astype(BF16)

    nsub = _chunks_per_step(nch)
    step = nsub * ch
    assert hh_sc.shape[0] == step + tm
    hh_sc[0:step, :] = jnp.zeros((step, M_WIDTH), F32)
    nt_dims = (((1,), (1,)), ((), ()))
    tn_dims = (((0,), (0,)), ((), ()))
    heads = range(M_HEADS)
    groups = range(A_KV_HEADS)
    subs = range(nsub)
    stack_order = [i for i in range(A_GROUP) if i % 2 == 0] + [i for i in range(A_GROUP) if i % 2 == 1]
    sink_cols = [jnp.concatenate([jnp.full((ch, 1), sinks_ref[A_GROUP * j + i], F32) for i in stack_order], axis=0)
                 for j in groups]

    def loop_step(i, carry):
        base_row = pl.multiple_of(i * step, step)
        r0 = [pl.multiple_of(base_row + u * ch, ch) for u in subs]
        rows = [pl.ds(r0[u], ch) for u in subs]
        out_rows = [pl.ds(r0[u] + step, ch) for u in subs]
        cidx = [i * nsub + u for u in subs]

        qb = [[q_sc[rows[u], hcols[hd]] for hd in heads] for u in subs]
        vaug = [[vaug_sc[hd, rows[u], :] for hd in heads] for u in subs]
        kw = [[(kf_sc[rows[u], hcols[hd]] * wk_sc[hd, rows[u], :]).astype(BF16) for hd in heads] for u in subs]
        upd = [[lax.dot_general(kw[u][hd], vaug[u][hd], tn_dims, preferred_element_type=F32) for hd in heads]
               for u in subs]
        qk = [[lax.dot_general(qb[u][hd], kb_sc[rows[u], hcols[hd]], nt_dims, preferred_element_type=F32)
               for hd in heads] for u in subs]

        kband = [[ka_sc[j, pl.ds(r0[u], band), :] for j in range(A_KV_HEADS)] for u in subs]
        sc = [[lax.dot_general(qa_sc[j, cidx[u]], kband[u][j], nt_dims, preferred_element_type=F32)
               for j in groups] for u in subs]
        if not has_cache:
            for u in subs:
                first_valid = WINDOW - (t * nch + cidx[u]) * ch
                mask_bias = jnp.where(band_pos >= first_valid, 0.0, NEG_INF)
                sc[u] = [a + mask_bias for a in sc[u]]

        head_norm(pl.ds(base_row, step), pl.ds(pl.multiple_of(jnp.maximum(base_row - step, 0), step), step),
                  pl.ds(base_row, step))

        state = [[cn_sc[hd] for hd in heads]]
        for u in subs:
            nxt = []
            for hd in heads:
                dec = dec_sc[hd, pl.ds(cidx[u], 1), :]
                nxt.append(jnp.concatenate([dec, dec], axis=1) * state[u][hd] + upd[u][hd])
            state.append(nxt)
        inter = [[jnp.dot(qb[u][hd], state[u][hd].astype(BF16), preferred_element_type=F32) for hd in heads]
                 for u in subs]
        for hd in heads:
            cn_sc[hd] = state[nsub][hd]

        s = [[(qk[u][hd] * d_sc[hd, cidx[u]]).astype(BF16) for hd in heads] for u in subs]
        mx_col = [[jnp.maximum(jnp.max(sc[u][j], axis=-1, keepdims=True), sink_cols[j]) for j in groups]
                  for u in subs]
        intra = [[jnp.dot(s[u][hd], vaug[u][hd], preferred_element_type=F32) for hd in heads]
                 for u in subs]
        mx_wide = [[jnp.broadcast_to(mx_col[u][j], (A_GROUP * ch, band)) for j in groups] for u in subs]
        pr = [[jnp.exp(sc[u][j] - mx_wide[u][j]).astype(BF16) for j in groups] for u in subs]
        sink_term = [[jnp.exp(sink_cols[j] - mx_wide[u][j][:, 0:LANES]) for j in groups] for u in subs]
        half = (A_GROUP // 2) * ch
        part = [[jnp.dot(pr[u][j][0:half], ve_sc[j, pl.ds(r0[u], band), :], preferred_element_type=F32)
                 + jnp.dot(pr[u][j][half:2 * half], vo_sc[j, pl.ds(r0[u], band), :], preferred_element_type=F32)
                 for j in groups] for u in subs]

        for u in subs:
            for hd in heads:
                wi = wi_sc[hd, rows[u], :]
                num = intra[u][hd][:, 0:M_HEAD_DIM] + wi * inter[u][hd][:, 0:M_HEAD_DIM]
                den = intra[u][hd][:, M_HEAD_DIM:] + wi * inter[u][hd][:, M_HEAD_DIM:]
                hh_sc[out_rows[u], hcols[hd]] = num / jnp.maximum(jnp.abs(den), em_sc[hd, rows[u], :])
        for u in subs:
            for j in groups:
                acc = part[u][j]
                dn = acc[:, LANES:] + jnp.where(left, sink_term[u][j][0:half], sink_term[u][j][half:2 * half])
                out = (acc[:, 0:LANES] / dn).astype(BF16)
                for k in range(A_GROUP // 2):
                    p = j * (A_GROUP // 2) + k
                    mix_sc[out_rows[u], M_WIDTH + p * LANES:M_WIDTH + (p + 1) * LANES] = (
                        out[k * ch:(k + 1) * ch, :])
        return carry

    lax.fori_loop(0, nch // nsub, loop_step, 0)
    head_norm(slice(tm, tm + step), slice(tm - step, tm), slice(tm, tm + step))

    piece = min(tm, OUT_PIECE)
    mixed = [jnp.dot(mix_sc[step + lo:step + lo + piece, :], wout_ref[...], preferred_element_type=F32)
             for lo in range(0, tm, piece)]
    for i, lo in enumerate(range(0, tm, piece)):
        h_ref[0, lo:lo + piece, :] = _layer_norm(alpha * x_ref[0, lo:lo + piece, :] + mixed[i],
                                                 ln1g_ref[...], ln1b_ref[...])

    @pl.when(t == nt - 1)
    def _store_state():
        cnout_ref[0] = cn_sc[...]
        mout_ref[0] = m_sc[...]


def _ffn_body(h_ref, prev_ref, wup_ref, wconv_ref, bconv_ref, wdown_ref, ln2g_ref, ln2b_ref,
              y_ref, rows_ref, carry_sc, acc_sc, perm_sc, *, tm, nt, alpha):
    t = pl.program_id(1)

    @pl.when(t == 0)
    def _load_carry():
        carry_sc[...] = prev_ref[0]

    nv = tm // SUBLANES
    pitch = nv + 1
    nl = D_MODEL // LANES
    for k in range(nl):
        for s in range(SUBLANES):
            perm_sc[k, s * pitch:s * pitch + nv, :] = h_ref[0, s * nv:(s + 1) * nv, k * LANES:(k + 1) * LANES]
    hf = jnp.concatenate(
        [jnp.concatenate([perm_sc[k, pl.ds(a, SUBLANES, stride=pitch), :] for k in range(nl)], axis=1)
         for a in range(nv)], axis=0)
    hb = hf.astype(BF16)
    nblk = D_FF // FF_BLOCK
    first_sublane = lax.broadcasted_iota(jnp.int32, (SUBLANES, FF_BLOCK), 0) == 0

    def up_project(blk, half):
        return jnp.dot(hb, wup_ref[:, half * D_FF + blk * FF_BLOCK:half * D_FF + (blk + 1) * FF_BLOCK],
                       preferred_element_type=F32)

    def down_project(act, blk):
        down = jnp.dot(act, wdown_ref[blk * FF_BLOCK:(blk + 1) * FF_BLOCK, :], preferred_element_type=F32)
        if blk == 0:
            acc_sc[...] = down
        else:
            acc_sc[...] += down

    def conv(u, blk, half):
        cols = slice(half * D_FF + blk * FF_BLOCK, half * D_FF + (blk + 1) * FF_BLOCK)
        prev = carry_sc[:, cols]
        last1 = u[tm - SUBLANES:tm, :]
        last2 = u[tm - 2 * SUBLANES:tm - SUBLANES, :]
        lead1 = jnp.where(first_sublane, prev[SUBLANES - 1:SUBLANES, :], pltpu.roll(last1, 1, 0))
        lead2 = jnp.where(first_sublane, prev[SUBLANES - 2:SUBLANES - 1, :], pltpu.roll(last2, 1, 0))
        carry_sc[SUBLANES - 2:SUBLANES - 1, cols] = last2[SUBLANES - 1:SUBLANES, :]
        carry_sc[SUBLANES - 1:SUBLANES, cols] = last1[SUBLANES - 1:SUBLANES, :]
        back1 = jnp.concatenate([lead1, u[0:tm - SUBLANES, :]], axis=0)
        older = [u[0:tm - 2 * SUBLANES, :]] if nv > 2 else []
        back2 = jnp.concatenate([lead2, lead1] + older, axis=0)
        w = wconv_ref[:, cols]
        return bconv_ref[:, cols] + back2 * w[0:1, :] + back1 * w[1:2, :] + u * w[2:3, :]

    u_val, u_gate = up_project(0, 0), up_project(0, 1)
    act_prev = None
    for blk in range(nblk):
        more = blk + 1 < nblk
        if more:
            u_val_next = up_project(blk + 1, 0)
        val = conv(u_val, blk, 0)
        if more:
            u_gate_next = up_project(blk + 1, 1)
        gate = conv(u_gate, blk, 1)
        act = (gate * jax.nn.sigmoid(gate) * val).astype(BF16)
        if act_prev is not None:
            down_project(act_prev, blk - 1)
        act_prev = act
        if more:
            u_val, u_gate = u_val_next, u_gate_next
    down_project(act_prev, nblk - 1)

    y = _layer_norm(alpha * hf + acc_sc[...], ln2g_ref[...], ln2b_ref[...])
    for k in range(nl):
        for a in range(nv):
            perm_sc[k, pl.ds(a, SUBLANES, stride=pitch), :] = (
                y[a * SUBLANES:(a + 1) * SUBLANES, k * LANES:(k + 1) * LANES])
        for s in range(SUBLANES):
            y_ref[0, s * nv:(s + 1) * nv, k * LANES:(k + 1) * LANES] = perm_sc[k, s * pitch:s * pitch + nv, :]

    @pl.when(t == nt - 1)
    def _store_rows():
        rows_ref[0] = carry_sc[...]


def _const_spec(shape):
    zeros = (0,) * len(shape)
    return pl.BlockSpec(shape, lambda b, t: zeros, pipeline_mode=pl.Buffered(1))


def _batch_spec(shape):
    zeros = (0,) * (len(shape) - 1)
    return pl.BlockSpec(shape, lambda b, t: (b,) + zeros)


def _mixer_call(x, pos0, cn0, m0, kc, vc, wts, *, ch, tm, has_cache, alpha):
    bsz, seq, _ = x.shape
    nt = seq // tm
    nch = tm // ch
    rows_out = min(WINDOW, seq)
    assert seq % tm == 0 and tm % ch == 0 and rows_out <= tm and (nt == 1 or tm >= WINDOW)
    step = _chunks_per_step(nch) * ch

    half = A_HEAD_DIM // 2
    inv = ROPE_THETA ** (-np.arange(half, dtype=np.float64) / half)
    ang = (pos0 + np.arange(seq, dtype=np.float64))[:, None] * inv[None, :]
    cos, sin = np.cos(ang), np.sin(ang)
    cos_t = jnp.asarray(np.tile(cos, (1, LANES // half)), F32)
    sin_t = jnp.asarray(np.tile(np.concatenate([-sin, sin], axis=-1), (1, LANES // A_HEAD_DIM)), F32)

    body = functools.partial(_mixer_body, ch=ch, tm=tm, nt=nt, has_cache=has_cache, rows_out=rows_out,
                             alpha=alpha)
    tile3 = pl.BlockSpec((1, tm, D_MODEL), lambda b, t: (b, t, 0))
    state_shape = (1, M_HEADS, M_HEAD_DIM, 2 * M_HEAD_DIM)
    in_specs = [
        pl.BlockSpec(memory_space=pltpu.SMEM),
        tile3,
        pl.BlockSpec((tm, LANES), lambda b, t: (t, 0)),
        pl.BlockSpec((tm, LANES), lambda b, t: (t, 0)),
        _const_spec((D_MODEL, MAIN_WIDTH)),
        _const_spec((D_MODEL, LANES)),
        _const_spec((1, LANES)),
        _const_spec((1, M_WIDTH)),
        _const_spec((M_WIDTH + A_WIDTH, D_MODEL)),
        _const_spec((1, D_MODEL)),
        _const_spec((1, D_MODEL)),
        _batch_spec(state_shape),
        _batch_spec((1, 1, LANES)),
        _batch_spec((1, WINDOW, KV_WIDTH)),
        _batch_spec((1, WINDOW, KV_WIDTH)),
    ]
    out_specs = [
        tile3,
        _batch_spec(state_shape),
        _batch_spec((1, 1, LANES)),
        _batch_spec((1, rows_out, KV_WIDTH)),
        _batch_spec((1, rows_out, KV_WIDTH)),
    ]
    out_shape = [
        jax.ShapeDtypeStruct((bsz, seq, D_MODEL), F32),
        jax.ShapeDtypeStruct((bsz,) + state_shape[1:], F32),
        jax.ShapeDtypeStruct((bsz, 1, LANES), F32),
        jax.ShapeDtypeStruct((bsz, rows_out, KV_WIDTH), F32),
        jax.ShapeDtypeStruct((bsz, rows_out, KV_WIDTH), F32),
    ]
    scratch = [
        pltpu.VMEM((tm, M_WIDTH), BF16),
        pltpu.VMEM((tm, M_WIDTH), BF16),
        pltpu.VMEM((tm, M_WIDTH), F32),
        pltpu.VMEM((M_HEADS, tm, 2 * M_HEAD_DIM), BF16),
        pltpu.VMEM((tm, M_WIDTH), F32),
        pltpu.VMEM((A_KV_HEADS, nch, A_GROUP * ch, A_HEAD_DIM), BF16),
        pltpu.VMEM((A_KV_HEADS, WINDOW + tm, A_HEAD_DIM), BF16),
        pltpu.VMEM((A_KV_HEADS, WINDOW + tm, 2 * LANES), BF16),
        pltpu.VMEM((A_KV_HEADS, WINDOW + tm, 2 * LANES), BF16),
        pltpu.VMEM((M_HEADS, tm, LANES), F32),
        pltpu.VMEM((M_HEADS, tm, LANES), F32),
        pltpu.VMEM((M_HEADS, tm, LANES), F32),
        pltpu.VMEM((M_HEADS, nch, LANES), F32),
        pltpu.VMEM((M_HEADS, nch, ch, ch), F32),
        pltpu.VMEM((step + tm, M_WIDTH), F32),
        pltpu.VMEM((step + tm, M_WIDTH + A_WIDTH), BF16),
        pltpu.VMEM((M_HEADS, M_HEAD_DIM, 2 * M_HEAD_DIM), F32),
        pltpu.VMEM((1, LANES), F32),
    ]
    return pl.pallas_call(
        body,
        grid=(bsz, nt),
        in_specs=in_specs,
        out_specs=out_specs,
        out_shape=out_shape,
        scratch_shapes=scratch,
        compiler_params=pltpu.CompilerParams(dimension_semantics=("arbitrary", "arbitrary"),
                                             vmem_limit_bytes=VMEM_LIMIT_BYTES),
    )(wts["sinks"], x, cos_t, sin_t, wts["w_main"], wts["w_gate"], wts["gate_bias"], wts["g_norm"],
      wts["w_out"], wts["ln1_g"], wts["ln1_b"], cn0, m0, kc, vc)


def _ffn_call(h, conv_prev, wts, *, tm, alpha):
    bsz, seq, _ = h.shape
    nt = seq // tm
    assert seq % tm == 0 and tm % SUBLANES == 0
    body = functools.partial(_ffn_body, tm=tm, nt=nt, alpha=alpha)
    nl = D_MODEL // LANES
    tile3 = pl.BlockSpec((1, tm, D_MODEL), lambda b, t: (b, t, 0))
    return pl.pallas_call(
        body,
        grid=(bsz, nt),
        in_specs=[
            tile3,
            _batch_spec((1, SUBLANES, UP_WIDTH)),
            _const_spec((D_MODEL, UP_WIDTH)),
            _const_spec((CONV_W, UP_WIDTH)),
            _const_spec((1, UP_WIDTH)),
            _const_spec((D_FF, D_MODEL)),
            _const_spec((1, D_MODEL)),
            _const_spec((1, D_MODEL)),
        ],
        out_specs=[tile3, _batch_spec((1, SUBLANES, UP_WIDTH))],
        out_shape=[jax.ShapeDtypeStruct((bsz, seq, D_MODEL), F32),
                   jax.ShapeDtypeStruct((bsz, SUBLANES, UP_WIDTH), F32)],
        scratch_shapes=[pltpu.VMEM((SUBLANES, UP_WIDTH), F32), pltpu.VMEM((tm, D_MODEL), F32),
                        pltpu.VMEM((nl, SUBLANES * (tm // SUBLANES + 1), LANES), F32)],
        compiler_params=pltpu.CompilerParams(dimension_semantics=("arbitrary", "arbitrary"),
                                             vmem_limit_bytes=VMEM_LIMIT_BYTES),
    )(h, conv_prev, wts["w_up"], wts["w_conv"], wts["b_conv"], wts["w_down"], wts["ln2_g"], wts["ln2_b"])


def _prep_weights(w_in, b_igate, b_fgate, g_mlstm_norm, attn_sinks, w_out, ln1_g, ln1_b,
                  w_up, w_conv, b_conv, w_down, ln2_g, ln2_b):
    gate_lo = 4 * M_WIDTH
    gate_hi = gate_lo + 2 * M_HEADS
    w_main = jnp.concatenate([w_in[:, :gate_lo], w_in[:, gate_hi:]], axis=1).astype(BF16)
    w_gate = jnp.pad(w_in[:, gate_lo:gate_hi], ((0, 0), (0, LANES - 2 * M_HEADS))).astype(BF16)
    gate_bias = jnp.pad(jnp.concatenate([b_igate, b_fgate]).astype(F32), (0, LANES - 2 * M_HEADS))
    row = lambda a: a.astype(F32).reshape(1, -1)
    return dict(
        sinks=attn_sinks.astype(F32), w_main=w_main, w_gate=w_gate, gate_bias=row(gate_bias),
        g_norm=row(g_mlstm_norm), w_out=w_out.astype(BF16), ln1_g=row(ln1_g), ln1_b=row(ln1_b),
        w_up=w_up.astype(BF16), w_conv=w_conv.astype(F32), b_conv=row(b_conv), w_down=w_down.astype(BF16),
        ln2_g=row(ln2_g), ln2_b=row(ln2_b))


def _stream_layer(x, pos0, c0, n0, m0, kc, vc, conv_prev, wts, *, ch, tm, has_cache, alpha):
    bsz, seq, _ = x.shape
    n_rep = jnp.broadcast_to(n0[..., None], (bsz, M_HEADS, M_HEAD_DIM, M_HEAD_DIM))
    cn0 = jnp.concatenate([c0, n_rep], axis=-1).astype(F32)
    m0 = jnp.pad(m0.astype(F32), ((0, 0), (GATE_LANE, LANES - GATE_LANE - M_HEADS))).reshape(bsz, 1, LANES)
    kc = kc.reshape(bsz, WINDOW, KV_WIDTH)
    vc = vc.reshape(bsz, WINDOW, KV_WIDTH)
    prev8 = jnp.pad(conv_prev, ((0, 0), (SUBLANES - (CONV_W - 1), 0), (0, 0)))
    h, cn1, m1, k_rows, v_rows = _mixer_call(x, pos0, cn0, m0, kc, vc, wts, ch=ch, tm=tm,
                                             has_cache=has_cache, alpha=alpha)
    y, rows8 = _ffn_call(h, prev8, wts, tm=min(seq, FFN_ROWS), alpha=alpha)
    rows_out = k_rows.shape[1]
    state = (cn1[..., :M_HEAD_DIM], cn1[..., M_HEAD_DIM], m1[:, 0, GATE_LANE:GATE_LANE + M_HEADS],
             k_rows.reshape(bsz, rows_out, A_KV_HEADS, A_HEAD_DIM),
             v_rows.reshape(bsz, rows_out, A_KV_HEADS, A_HEAD_DIM),
             rows8[:, SUBLANES - (CONV_W - 1):, :])
    return y, state


def kernel(x_prompt, x_sample, state_mlstm_C, state_mlstm_n, state_mlstm_m, cache_swa_k, cache_swa_v,
           state_conv, w_in, b_igate, b_fgate, g_mlstm_norm, attn_sinks, w_out, ln1_g, ln1_b,
           w_up, w_conv, b_conv, w_down, ln2_g, ln2_b):
    depth = w_in.shape[0]
    alpha = (2 * depth) ** 0.25
    bp, tp = x_prompt.shape[0], x_prompt.shape[1]
    bs, ts = x_sample.shape[0], x_sample.shape[1]
    tm_p = min(tp, MIXER_ROWS)
    hp, hs = x_prompt, x_sample
    sp, ss = [], []
    for l in range(depth):
        wts = _prep_weights(w_in[l], b_igate[l], b_fgate[l], g_mlstm_norm[l], attn_sinks[l], w_out[l],
                            ln1_g[l], ln1_b[l], w_up[l], w_conv[l], b_conv[l], w_down[l], ln2_g[l], ln2_b[l])
        hp, st_p = _stream_layer(
            hp, 0,
            jnp.zeros((bp, M_HEADS, M_HEAD_DIM, M_HEAD_DIM), F32),
            jnp.zeros((bp, M_HEADS, M_HEAD_DIM), F32),
            jnp.zeros((bp, M_HEADS), F32),
            jnp.zeros((bp, WINDOW, A_KV_HEADS, A_HEAD_DIM), F32),
            jnp.zeros((bp, WINDOW, A_KV_HEADS, A_HEAD_DIM), F32),
            jnp.zeros((bp, CONV_W - 1, UP_WIDTH), F32),
            wts, ch=min(CHUNK, tp), tm=tm_p, has_cache=False, alpha=alpha)
        hs, st_s = _stream_layer(
            hs, PAST_LEN, state_mlstm_C[l], state_mlstm_n[l], state_mlstm_m[l],
            cache_swa_k[l], cache_swa_v[l], state_conv[l],
            wts, ch=ts, tm=ts, has_cache=True, alpha=alpha)
        sp.append(st_p)
        ss.append(st_s)
    P = [jnp.stack([s[i] for s in sp]) for i in range(6)]
    S = [jnp.stack([s[i] for s in ss]) for i in range(6)]
    return (hp, hs, P[0], P[1], P[2], P[3], P[4], P[5], S[0], S[1], S[2], S[3], S[4], S[5])
```

```python
import functools

import jax
import jax.numpy as jnp
import numpy as np
from jax import lax
from jax.experimental import pallas as pl
from jax.experimental.pallas import tpu as pltpu

D_MODEL = 1024
M_HEADS = 4
M_HEAD_DIM = 128
M_WIDTH = M_HEADS * M_HEAD_DIM
A_Q_HEADS = 8
A_KV_HEADS = 2
A_HEAD_DIM = 64
A_GROUP = A_Q_HEADS // A_KV_HEADS
A_WIDTH = A_Q_HEADS * A_HEAD_DIM
KV_WIDTH = A_KV_HEADS * A_HEAD_DIM
WINDOW = 128
CHUNK = 64
ROPE_THETA = 10000.0
D_FF = 2816
UP_WIDTH = 2 * D_FF
CONV_W = 3
LN_EPS = 1e-5
PAST_LEN = 2048

LANES = 128
SUBLANES = 8
GATE_LANE = M_HEADS
MAIN_WIDTH = 4 * M_WIDTH + A_WIDTH + 2 * KV_WIDTH
FF_BLOCK = 256
DOWN_GROUP = 4
OUT_PIECE = 256
CHUNKS_PER_STEP = 2
MIXER_ROWS = 512
FFN_ROWS = 1024
assert M_HEADS == A_Q_HEADS // 2
VMEM_LIMIT_BYTES = 56 * 1024 * 1024

F32 = jnp.float32
BF16 = jnp.bfloat16
NEG_INF = float("-inf")


def _layer_norm(y, g, b):
    mu = jnp.mean(y, axis=-1, keepdims=True)
    d = y - mu
    var = jnp.mean(d * d, axis=-1, keepdims=True)
    return d * lax.rsqrt(var + LN_EPS) * g + b


def _log_sigmoid(x):
    return jnp.minimum(x, 0.0) - jnp.log1p(jnp.exp(-jnp.abs(x)))


def _chunk_scan(x, group, *, reverse=False, use_max=False):
    rows = x.shape[0]
    pos = lax.broadcasted_iota(jnp.int32, x.shape, 0) % group
    fill = NEG_INF if use_max else 0.0
    acc = x
    k = 1
    while k < group:
        if reverse:
            shifted = jnp.where(pos < group - k, pltpu.roll(acc, rows - k, 0), fill)
        else:
            shifted = jnp.where(pos >= k, pltpu.roll(acc, k, 0), fill)
        acc = jnp.maximum(acc, shifted) if use_max else acc + shifted
        k *= 2
    return acc


def _chunks_per_step(nch):
    return max(k for k in range(1, CHUNKS_PER_STEP + 1) if nch % k == 0)


def _rope(x, cos, sin_signed):
    lane = lax.broadcasted_iota(jnp.int32, x.shape, 1) % A_HEAD_DIM
    partner = jnp.where(lane < A_HEAD_DIM // 2,
                        pltpu.roll(x, LANES - A_HEAD_DIM // 2, 1),
                        pltpu.roll(x, A_HEAD_DIM // 2, 1))
    return x * cos + partner * sin_signed


def _mixer_body(sinks_ref, x_ref, cos_ref, sin_ref, wmain_ref, wgate_ref, gbias_ref, gnorm_ref, wout_ref,
                ln1g_ref, ln1b_ref, cn0_ref, m0_ref, kc_ref, vc_ref,
                h_ref, cnout_ref, mout_ref, kout_ref, vout_ref,
                q_sc, kb_sc, kf_sc, vaug_sc, o_sc, qa_sc, ka_sc, ve_sc, vo_sc,
                wi_sc, em_sc, wk_sc, dec_sc, d_sc, hh_sc, mix_sc, cn_sc, m_sc,
                *, ch, tm, nt, has_cache, rows_out, alpha):
    t = pl.program_id(1)
    nch = tm // ch
    band = WINDOW + ch
    left = lax.broadcasted_iota(jnp.int32, (1, LANES), 1) < A_HEAD_DIM
    ones_left = jnp.where(left, 1.0, 0.0).astype(BF16)
    ones_right = jnp.where(left, 0.0, 1.0).astype(BF16)

    def store_values(row_slice, nrows, va):
        swapped = pltpu.roll(va, A_HEAD_DIM, 1)
        ve_sc[0, row_slice, 0:LANES] = jnp.where(left, va, 0.0).astype(BF16)
        vo_sc[0, row_slice, 0:LANES] = jnp.where(left, 0.0, swapped).astype(BF16)
        ve_sc[1, row_slice, 0:LANES] = jnp.where(left, swapped, 0.0).astype(BF16)
        vo_sc[1, row_slice, 0:LANES] = jnp.where(left, 0.0, va).astype(BF16)
        for j in range(A_KV_HEADS):
            ve_sc[j, row_slice, LANES:2 * LANES] = jnp.broadcast_to(ones_left, (nrows, LANES))
            vo_sc[j, row_slice, LANES:2 * LANES] = jnp.broadcast_to(ones_right, (nrows, LANES))

    @pl.when(t == 0)
    def _load_state():
        cn_sc[...] = cn0_ref[0]
        m_sc[...] = m0_ref[0]
        kc = kc_ref[0].astype(BF16)
        for j in range(A_KV_HEADS):
            ka_sc[j, 0:WINDOW, :] = kc[:, j * A_HEAD_DIM:(j + 1) * A_HEAD_DIM]
        store_values(slice(0, WINDOW), WINDOW, vc_ref[0])

    if nt > 1:
        @pl.when(t > 0)
        def _slide_window():
            for j in range(A_KV_HEADS):
                ka_sc[j, 0:WINDOW, :] = ka_sc[j, tm:tm + WINDOW, :]
                ve_sc[j, 0:WINDOW, :] = ve_sc[j, tm:tm + WINDOW, :]
                vo_sc[j, 0:WINDOW, :] = vo_sc[j, tm:tm + WINDOW, :]

    xb = x_ref[0].astype(BF16)

    def proj(lo, hi):
        return jnp.dot(xb, wmain_ref[:, lo:hi], preferred_element_type=F32)

    gates = jnp.dot(xb, wgate_ref[...], preferred_element_type=F32) + gbias_ref[...]
    cos = cos_ref[...]
    sin_signed = sin_ref[...]
    attn_base = 4 * M_WIDTH

    def project_queries(pair):
        wide = proj(attn_base + 2 * pair * LANES, attn_base + (2 * pair + 2) * LANES)
        for p in (2 * pair, 2 * pair + 1):
            slab = _rope(wide[:, (p % 2) * LANES:(p % 2 + 1) * LANES], cos, sin_signed)
            slab = (slab * (A_HEAD_DIM ** -0.5)).astype(BF16)
            for e in range(2):
                pos = (p % 2) + 2 * e
                for c in range(nch):
                    qa_sc[p // 2, c, pos * ch:(pos + 1) * ch, :] = (
                        slab[c * ch:(c + 1) * ch, e * A_HEAD_DIM:(e + 1) * A_HEAD_DIM])

    def project_keys_values():
        kv = proj(attn_base + A_WIDTH, attn_base + A_WIDTH + 2 * KV_WIDTH)
        ka = _rope(kv[:, 0:KV_WIDTH], cos, sin_signed)
        va = kv[:, KV_WIDTH:2 * KV_WIDTH]
        kab = ka.astype(BF16)
        for j in range(A_KV_HEADS):
            ka_sc[j, WINDOW:WINDOW + tm, :] = kab[:, j * A_HEAD_DIM:(j + 1) * A_HEAD_DIM]
        store_values(slice(WINDOW, WINDOW + tm), tm, va)
        kout_ref[0] = ka[tm - rows_out:tm, :]
        vout_ref[0] = va[tm - rows_out:tm, :]

    project_queries(0)

    log_i = pltpu.roll(gates, GATE_LANE, 1)
    log_f = _log_sigmoid(gates)
    b_incl = _chunk_scan(log_f, ch)
    tail = _chunk_scan(log_f, ch, reverse=True) - log_f
    g_row = tail + log_i
    r_val = log_i - b_incl

    project_queries(1)

    r_cummax = _chunk_scan(r_val, ch, use_max=True)
    pos = lax.broadcasted_iota(jnp.int32, (nch, ch, LANES), 1)
    b_last = jnp.max(jnp.where(pos == ch - 1, b_incl.reshape(nch, ch, LANES), NEG_INF), axis=1)
    g_max = jnp.max(g_row.reshape(nch, ch, LANES), axis=1)
    m_run = m_sc[...]
    m_starts = []
    for c in range(nch):
        m_starts.append(m_run)
        m_run = jnp.maximum(b_last[c:c + 1, :] + m_run, g_max[c:c + 1, :])
    m_sc[...] = m_run
    m_start = jnp.concatenate(m_starts, axis=0)
    m_next = jnp.concatenate(m_starts[1:] + [m_run], axis=0)
    decay = jnp.exp(b_last + m_start - m_next)

    project_keys_values()

    def per_row(a):
        return jnp.broadcast_to(a[:, None, :], (nch, ch, LANES)).reshape(tm, LANES)

    m_rows = per_row(m_start)
    mx = jnp.maximum(m_rows, r_cummax)
    w_inter = jnp.exp(m_rows - mx)
    inv_floor = jnp.exp(-(b_incl + mx))
    w_state = jnp.exp(g_row - per_row(m_next))
    r_t = jnp.transpose(r_val)
    causal = (lax.broadcasted_iota(jnp.int32, (ch, ch), 0) >= lax.broadcasted_iota(jnp.int32, (ch, ch), 1))

    def spread_head(hd):
        ln = GATE_LANE + hd
        wi_sc[hd] = jnp.broadcast_to(w_inter[:, ln:ln + 1], (tm, LANES))
        em_sc[hd] = jnp.broadcast_to(inv_floor[:, ln:ln + 1], (tm, LANES))
        wk_sc[hd] = jnp.broadcast_to(w_state[:, ln:ln + 1], (tm, LANES))
        dec_sc[hd] = jnp.broadcast_to(decay[:, ln:ln + 1], (nch, LANES))
        mx_b = jnp.broadcast_to(mx[:, ln:ln + 1], (tm, ch))
        for c in range(nch):
            r_row = r_t[ln:ln + 1, c * ch:(c + 1) * ch]
            d_sc[hd, c] = jnp.where(causal, jnp.exp(r_row - mx_b[c * ch:(c + 1) * ch, :]), 0.0)

    q_sc[...] = proj(0, M_WIDTH).astype(BF16)
    spread_head(0)
    kf = proj(M_WIDTH, 2 * M_WIDTH) * (M_HEAD_DIM ** -0.5)
    kf_sc[...] = kf
    kb_sc[...] = kf.astype(BF16)
    spread_head(1)
    vm = proj(2 * M_WIDTH, 3 * M_WIDTH).astype(BF16)
    for hd in range(M_HEADS):
        vaug_sc[hd, :, 0:M_HEAD_DIM] = vm[:, hd * M_HEAD_DIM:(hd + 1) * M_HEAD_DIM]
        vaug_sc[hd, :, M_HEAD_DIM:2 * M_HEAD_DIM] = jnp.ones((tm, M_HEAD_DIM), BF16)
    spread_head(2)
    spread_head(3)
    o_sc[...] = jax.nn.sigmoid(proj(3 * M_WIDTH, 4 * M_WIDTH))

    band_pos = lax.broadcasted_iota(jnp.int32, (1, band), 1)
    hcols = [slice(hd * M_HEAD_DIM, (hd + 1) * M_HEAD_DIM) for hd in range(M_HEADS)]

    def head_norm(hh_rows, gate_rows, mix_rows):
        hh = [hh_sc[hh_rows, hcols[hd]] for hd in range(M_HEADS)]
        mu = [jnp.mean(a, axis=-1, keepdims=True) for a in hh]
        dlt = [hh[hd] - mu[hd] for hd in range(M_HEADS)]
        var = [jnp.mean(a * a, axis=-1, keepdims=True) for a in dlt]
        for hd in range(M_HEADS):
            hn = dlt[hd] * lax.rsqrt(var[hd] + LN_EPS) * gnorm_ref[:, hcols[hd]]
            mix_sc[mix_rows, hcols[hd]] = (o_sc[gate_rows, hcols[hd]] * hn).astype(BF16)

    nsub = _chunks_per_step(nch)
    step = nsub * ch
    assert hh_sc.shape[0] == step + tm
    hh_sc[0:step, :] = jnp.zeros((step, M_WIDTH), F32)
    nt_dims = (((1,), (1,)), ((), ()))
    tn_dims = (((0,), (0,)), ((), ()))
    heads = range(M_HEADS)
    groups = range(A_KV_HEADS)
    subs = range(nsub)
    stack_order = [i for i in range(A_GROUP) if i % 2 == 0] + [i for i in range(A_GROUP) if i % 2 == 1]
    sink_cols = [jnp.concatenate([jnp.full((ch, 1), sinks_ref[A_GROUP * j + i], F32) for i in stack_order], axis=0)
                 for j in groups]

    def loop_step(i, carry):
        base_row = pl.multiple_of(i * step, step)
        r0 = [pl.multiple_of(base_row + u * ch, ch) for u in subs]
        rows = [pl.ds(r0[u], ch) for u in subs]
        out_rows = [pl.ds(r0[u] + step, ch) for u in subs]
        cidx = [i * nsub + u for u in subs]

        qb = [[q_sc[rows[u], hcols[hd]] for hd in heads] for u in subs]
        vaug = [[vaug_sc[hd, rows[u], :] for hd in heads] for u in subs]
        kw = [[(kf_sc[rows[u], hcols[hd]] * wk_sc[hd, rows[u], :]).astype(BF16) for hd in heads] for u in subs]
        upd = [[lax.dot_general(kw[u][hd], vaug[u][hd], tn_dims, preferred_element_type=F32) for hd in heads]
               for u in subs]
        qk = [[lax.dot_general(qb[u][hd], kb_sc[rows[u], hcols[hd]], nt_dims, preferred_element_type=F32)
               for hd in heads] for u in subs]

        kband = [[ka_sc[j, pl.ds(r0[u], band), :] for j in range(A_KV_HEADS)] for u in subs]
        sc = [[lax.dot_general(qa_sc[j, cidx[u]], kband[u][j], nt_dims, preferred_element_type=F32)
               for j in groups] for u in subs]
        if not has_cache:
            for u in subs:
                first_valid = WINDOW - (t * nch + cidx[u]) * ch
                mask_bias = jnp.where(band_pos >= first_valid, 0.0, NEG_INF)
                sc[u] = [a + mask_bias for a in sc[u]]

        head_norm(pl.ds(base_row, step), pl.ds(pl.multiple_of(jnp.maximum(base_row - step, 0), step), step),
                  pl.ds(base_row, step))

        state = [[cn_sc[hd] for hd in heads]]
        for u in subs:
            nxt = []
            for hd in heads:
                dec = dec_sc[hd, pl.ds(cidx[u], 1), :]
                nxt.append(jnp.concatenate([dec, dec], axis=1) * state[u][hd] + upd[u][hd])
            state.append(nxt)
        inter = [[jnp.dot(qb[u][hd], state[u][hd].astype(BF16), preferred_element_type=F32) for hd in heads]
                 for u in subs]
        for hd in heads:
            cn_sc[hd] = state[nsub][hd]

        s = [[(qk[u][hd] * d_sc[hd, cidx[u]]).astype(BF16) for hd in heads] for u in subs]
        mx_col = [[jnp.maximum(jnp.max(sc[u][j], axis=-1, keepdims=True), sink_cols[j]) for j in groups]
                  for u in subs]
        intra = [[jnp.dot(s[u][hd], vaug[u][hd], preferred_element_type=F32) for hd in heads]
                 for u in subs]
        mx_wide = [[jnp.broadcast_to(mx_col[u][j], (A_GROUP * ch, band)) for j in groups] for u in subs]
        pr = [[jnp.exp(sc[u][j] - mx_wide[u][j]).astype(BF16) for j in groups] for u in subs]
        sink_term = [[jnp.exp(sink_cols[j] - mx_wide[u][j][:, 0:LANES]) for j in groups] for u in subs]
        half = (A_GROUP // 2) * ch
        part = [[jnp.dot(pr[u][j][0:half], ve_sc[j, pl.ds(r0[u], band), :], preferred_element_type=F32)
                 + jnp.dot(pr[u][j][half:2 * half], vo_sc[j, pl.ds(r0[u], band), :], preferred_element_type=F32)
                 for j in groups] for u in subs]

        for u in subs:
            for hd in heads:
                wi = wi_sc[hd, rows[u], :]
                num = intra[u][hd][:, 0:M_HEAD_DIM] + wi * inter[u][hd][:, 0:M_HEAD_DIM]
                den = intra[u][hd][:, M_HEAD_DIM:] + wi * inter[u][hd][:, M_HEAD_DIM:]
                hh_sc[out_rows[u], hcols[hd]] = num / jnp.maximum(jnp.abs(den), em_sc[hd, rows[u], :])
        for u in subs:
            for j in groups:
                acc = part[u][j]
                dn = acc[:, LANES:] + jnp.where(left, sink_term[u][j][0:half], sink_term[u][j][half:2 * half])
                out = (acc[:, 0:LANES] / dn).astype(BF16)
                for k in range(A_GROUP // 2):
                    p = j * (A_GROUP // 2) + k
                    mix_sc[out_rows[u], M_WIDTH + p * LANES:M_WIDTH + (p + 1) * LANES] = (
                        out[k * ch:(k + 1) * ch, :])
        return carry

    lax.fori_loop(0, nch // nsub, loop_step, 0)
    head_norm(slice(tm, tm + step), slice(tm - step, tm), slice(tm, tm + step))

    piece = min(tm, OUT_PIECE)
    mixed = [jnp.dot(mix_sc[step + lo:step + lo + piece, :], wout_ref[...], preferred_element_type=F32)
             for lo in range(0, tm, piece)]
    for i, lo in enumerate(range(0, tm, piece)):
        h_ref[0, lo:lo + piece, :] = _layer_norm(alpha * x_ref[0, lo:lo + piece, :] + mixed[i],
                                                 ln1g_ref[...], ln1b_ref[...])

    @pl.when(t == nt - 1)
    def _store_state():
        cnout_ref[0] = cn_sc[...]
        mout_ref[0] = m_sc[...]


def _ffn_body(h_ref, prev_ref, wup_ref, wconv_ref, bconv_ref, wdown_ref, ln2g_ref, ln2b_ref,
              y_ref, rows_ref, carry_sc, acc_sc, perm_sc, *, tm, nt, alpha):
    t = pl.program_id(1)

    @pl.when(t == 0)
    def _load_carry():
        carry_sc[...] = prev_ref[0]

    nv = tm // SUBLANES
    pitch = nv + 1
    nl = D_MODEL // LANES
    for k in range(nl):
        for s in range(SUBLANES):
            perm_sc[k, s * pitch:s * pitch + nv, :] = h_ref[0, s * nv:(s + 1) * nv, k * LANES:(k + 1) * LANES]
    hf = jnp.concatenate(
        [jnp.concatenate([perm_sc[k, pl.ds(a, SUBLANES, stride=pitch), :] for k in range(nl)], axis=1)
         for a in range(nv)], axis=0)
    hb = hf.astype(BF16)
    nblk = D_FF // FF_BLOCK
    first_sublane = lax.broadcasted_iota(jnp.int32, (SUBLANES, FF_BLOCK), 0) == 0

    def up_project(blk, half):
        return jnp.dot(hb, wup_ref[:, half * D_FF + blk * FF_BLOCK:half * D_FF + (blk + 1) * FF_BLOCK],
                       preferred_element_type=F32)

    def down_project(acts, first_blk):
        act = acts[0] if len(acts) == 1 else jnp.concatenate(acts, axis=1)
        lo = first_blk * FF_BLOCK
        down = jnp.dot(act, wdown_ref[lo:lo + len(acts) * FF_BLOCK, :], preferred_element_type=F32)
        if first_blk == 0:
            acc_sc[...] = down
        else:
            acc_sc[...] += down

    def conv(u, blk, half):
        cols = slice(half * D_FF + blk * FF_BLOCK, half * D_FF + (blk + 1) * FF_BLOCK)
        prev = carry_sc[:, cols]
        last1 = u[tm - SUBLANES:tm, :]
        last2 = u[tm - 2 * SUBLANES:tm - SUBLANES, :]
        lead1 = jnp.where(first_sublane, prev[SUBLANES - 1:SUBLANES, :], pltpu.roll(last1, 1, 0))
        lead2 = jnp.where(first_sublane, prev[SUBLANES - 2:SUBLANES - 1, :], pltpu.roll(last2, 1, 0))
        carry_sc[SUBLANES - 2:SUBLANES - 1, cols] = last2[SUBLANES - 1:SUBLANES, :]
        carry_sc[SUBLANES - 1:SUBLANES, cols] = last1[SUBLANES - 1:SUBLANES, :]
        back1 = jnp.concatenate([lead1, u[0:tm - SUBLANES, :]], axis=0)
        older = [u[0:tm - 2 * SUBLANES, :]] if nv > 2 else []
        back2 = jnp.concatenate([lead2, lead1] + older, axis=0)
        w = wconv_ref[:, cols]
        return bconv_ref[:, cols] + back2 * w[0:1, :] + back1 * w[1:2, :] + u * w[2:3, :]

    u_val, u_gate = up_project(0, 0), up_project(0, 1)
    ready, pending = None, []
    for blk in range(nblk):
        more = blk + 1 < nblk
        if more:
            u_val_next = up_project(blk + 1, 0)
        val = conv(u_val, blk, 0)
        if more:
            u_gate_next = up_project(blk + 1, 1)
        gate = conv(u_gate, blk, 1)
        act = (gate * jax.nn.sigmoid(gate) * val).astype(BF16)
        if ready is not None:
            down_project(*ready)
            ready = None
        pending.append(act)
        if len(pending) == DOWN_GROUP or not more:
            ready = (pending, blk + 1 - len(pending))
            pending = []
        if more:
            u_val, u_gate = u_val_next, u_gate_next
    down_project(*ready)

    y = _layer_norm(alpha * hf + acc_sc[...], ln2g_ref[...], ln2b_ref[...])
    for k in range(nl):
        for a in range(nv):
            perm_sc[k, pl.ds(a, SUBLANES, stride=pitch), :] = (
                y[a * SUBLANES:(a + 1) * SUBLANES, k * LANES:(k + 1) * LANES])
        for s in range(SUBLANES):
            y_ref[0, s * nv:(s + 1) * nv, k * LANES:(k + 1) * LANES] = perm_sc[k, s * pitch:s * pitch + nv, :]

    @pl.when(t == nt - 1)
    def _store_rows():
        rows_ref[0] = carry_sc[...]


def _const_spec(shape):
    zeros = (0,) * len(shape)
    return pl.BlockSpec(shape, lambda b, t: zeros, pipeline_mode=pl.Buffered(1))


def _batch_spec(shape):
    zeros = (0,) * (len(shape) - 1)
    return pl.BlockSpec(shape, lambda b, t: (b,) + zeros)


def _mixer_call(x, pos0, cn0, m0, kc, vc, wts, *, ch, tm, has_cache, alpha):
    bsz, seq, _ = x.shape
    nt = seq // tm
    nch = tm // ch
    rows_out = min(WINDOW, seq)
    assert seq % tm == 0 and tm % ch == 0 and rows_out <= tm and (nt == 1 or tm >= WINDOW)
    step = _chunks_per_step(nch) * ch

    half = A_HEAD_DIM // 2
    inv = ROPE_THETA ** (-np.arange(half, dtype=np.float64) / half)
    ang = (pos0 + np.arange(seq, dtype=np.float64))[:, None] * inv[None, :]
    cos, sin = np.cos(ang), np.sin(ang)
    cos_t = jnp.asarray(np.tile(cos, (1, LANES // half)), F32)
    sin_t = jnp.asarray(np.tile(np.concatenate([-sin, sin], axis=-1), (1, LANES // A_HEAD_DIM)), F32)

    body = functools.partial(_mixer_body, ch=ch, tm=tm, nt=nt, has_cache=has_cache, rows_out=rows_out,
                             alpha=alpha)
    tile3 = pl.BlockSpec((1, tm, D_MODEL), lambda b, t: (b, t, 0))
    state_shape = (1, M_HEADS, M_HEAD_DIM, 2 * M_HEAD_DIM)
    in_specs = [
        pl.BlockSpec(memory_space=pltpu.SMEM),
        tile3,
        pl.BlockSpec((tm, LANES), lambda b, t: (t, 0)),
        pl.BlockSpec((tm, LANES), lambda b, t: (t, 0)),
        _const_spec((D_MODEL, MAIN_WIDTH)),
        _const_spec((D_MODEL, LANES)),
        _const_spec((1, LANES)),
        _const_spec((1, M_WIDTH)),
        _const_spec((M_WIDTH + A_WIDTH, D_MODEL)),
        _const_spec((1, D_MODEL)),
        _const_spec((1, D_MODEL)),
        _batch_spec(state_shape),
        _batch_spec((1, 1, LANES)),
        _batch_spec((1, WINDOW, KV_WIDTH)),
        _batch_spec((1, WINDOW, KV_WIDTH)),
    ]
    out_specs = [
        tile3,
        _batch_spec(state_shape),
        _batch_spec((1, 1, LANES)),
        _batch_spec((1, rows_out, KV_WIDTH)),
        _batch_spec((1, rows_out, KV_WIDTH)),
    ]
    out_shape = [
        jax.ShapeDtypeStruct((bsz, seq, D_MODEL), F32),
        jax.ShapeDtypeStruct((bsz,) + state_shape[1:], F32),
        jax.ShapeDtypeStruct((bsz, 1, LANES), F32),
        jax.ShapeDtypeStruct((bsz, rows_out, KV_WIDTH), F32),
        jax.ShapeDtypeStruct((bsz, rows_out, KV_WIDTH), F32),
    ]
    scratch = [
        pltpu.VMEM((tm, M_WIDTH), BF16),
        pltpu.VMEM((tm, M_WIDTH), BF16),
        pltpu.VMEM((tm, M_WIDTH), F32),
        pltpu.VMEM((M_HEADS, tm, 2 * M_HEAD_DIM), BF16),
        pltpu.VMEM((tm, M_WIDTH), F32),
        pltpu.VMEM((A_KV_HEADS, nch, A_GROUP * ch, A_HEAD_DIM), BF16),
        pltpu.VMEM((A_KV_HEADS, WINDOW + tm, A_HEAD_DIM), BF16),
        pltpu.VMEM((A_KV_HEADS, WINDOW + tm, 2 * LANES), BF16),
        pltpu.VMEM((A_KV_HEADS, WINDOW + tm, 2 * LANES), BF16),
        pltpu.VMEM((M_HEADS, tm, LANES), F32),
        pltpu.VMEM((M_HEADS, tm, LANES), F32),
        pltpu.VMEM((M_HEADS, tm, LANES), F32),
        pltpu.VMEM((M_HEADS, nch, LANES), F32),
        pltpu.VMEM((M_HEADS, nch, ch, ch), F32),
        pltpu.VMEM((step + tm, M_WIDTH), F32),
        pltpu.VMEM((step + tm, M_WIDTH + A_WIDTH), BF16),
        pltpu.VMEM((M_HEADS, M_HEAD_DIM, 2 * M_HEAD_DIM), F32),
        pltpu.VMEM((1, LANES), F32),
    ]
    return pl.pallas_call(
        body,
        grid=(bsz, nt),
        in_specs=in_specs,
        out_specs=out_specs,
        out_shape=out_shape,
        scratch_shapes=scratch,
        compiler_params=pltpu.CompilerParams(dimension_semantics=("arbitrary", "arbitrary"),
                                             vmem_limit_bytes=VMEM_LIMIT_BYTES),
    )(wts["sinks"], x, cos_t, sin_t, wts["w_main"], wts["w_gate"], wts["gate_bias"], wts["g_norm"],
      wts["w_out"], wts["ln1_g"], wts["ln1_b"], cn0, m0, kc, vc)


def _ffn_call(h, conv_prev, wts, *, tm, alpha):
    bsz, seq, _ = h.shape
    nt = seq // tm
    assert seq % tm == 0 and tm % SUBLANES == 0
    body = functools.partial(_ffn_body, tm=tm, nt=nt, alpha=alpha)
    nl = D_MODEL // LANES
    tile3 = pl.BlockSpec((1, tm, D_MODEL), lambda b, t: (b, t, 0))
    return pl.pallas_call(
        body,
        grid=(bsz, nt),
        in_specs=[
            tile3,
            _batch_spec((1, SUBLANES, UP_WIDTH)),
            _const_spec((D_MODEL, UP_WIDTH)),
            _const_spec((CONV_W, UP_WIDTH)),
            _const_spec((1, UP_WIDTH)),
            _const_spec((D_FF, D_MODEL)),
            _const_spec((1, D_MODEL)),
            _const_spec((1, D_MODEL)),
        ],
        out_specs=[tile3, _batch_spec((1, SUBLANES, UP_WIDTH))],
        out_shape=[jax.ShapeDtypeStruct((bsz, seq, D_MODEL), F32),
                   jax.ShapeDtypeStruct((bsz, SUBLANES, UP_WIDTH), F32)],
        scratch_shapes=[pltpu.VMEM((SUBLANES, UP_WIDTH), F32), pltpu.VMEM((tm, D_MODEL), F32),
                        pltpu.VMEM((nl, SUBLANES * (tm // SUBLANES + 1), LANES), F32)],
        compiler_params=pltpu.CompilerParams(dimension_semantics=("arbitrary", "arbitrary"),
                                             vmem_limit_bytes=VMEM_LIMIT_BYTES),
    )(h, conv_prev, wts["w_up"], wts["w_conv"], wts["b_conv"], wts["w_down"], wts["ln2_g"], wts["ln2_b"])


def _prep_weights(w_in, b_igate, b_fgate, g_mlstm_norm, attn_sinks, w_out, ln1_g, ln1_b,
                  w_up, w_conv, b_conv, w_down, ln2_g, ln2_b):
    gate_lo = 4 * M_WIDTH
    gate_hi = gate_lo + 2 * M_HEADS
    w_main = jnp.concatenate([w_in[:, :gate_lo], w_in[:, gate_hi:]], axis=1).astype(BF16)
    w_gate = jnp.pad(w_in[:, gate_lo:gate_hi], ((0, 0), (0, LANES - 2 * M_HEADS))).astype(BF16)
    gate_bias = jnp.pad(jnp.concatenate([b_igate, b_fgate]).astype(F32), (0, LANES - 2 * M_HEADS))
    row = lambda a: a.astype(F32).reshape(1, -1)
    return dict(
        sinks=attn_sinks.astype(F32), w_main=w_main, w_gate=w_gate, gate_bias=row(gate_bias),
        g_norm=row(g_mlstm_norm), w_out=w_out.astype(BF16), ln1_g=row(ln1_g), ln1_b=row(ln1_b),
        w_up=w_up.astype(BF16), w_conv=w_conv.astype(F32), b_conv=row(b_conv), w_down=w_down.astype(BF16),
        ln2_g=row(ln2_g), ln2_b=row(ln2_b))


def _stream_layer(x, pos0, c0, n0, m0, kc, vc, conv_prev, wts, *, ch, tm, has_cache, alpha):
    bsz, seq, _ = x.shape
    n_rep = jnp.broadcast_to(n0[..., None], (bsz, M_HEADS, M_HEAD_DIM, M_HEAD_DIM))
    cn0 = jnp.concatenate([c0, n_rep], axis=-1).astype(F32)
    m0 = jnp.pad(m0.astype(F32), ((0, 0), (GATE_LANE, LANES - GATE_LANE - M_HEADS))).reshape(bsz, 1, LANES)
    kc = kc.reshape(bsz, WINDOW, KV_WIDTH)
    vc = vc.reshape(bsz, WINDOW, KV_WIDTH)
    prev8 = jnp.pad(conv_prev, ((0, 0), (SUBLANES - (CONV_W - 1), 0), (0, 0)))
    h, cn1, m1, k_rows, v_rows = _mixer_call(x, pos0, cn0, m0, kc, vc, wts, ch=ch, tm=tm,
                                             has_cache=has_cache, alpha=alpha)
    y, rows8 = _ffn_call(h, prev8, wts, tm=min(seq, FFN_ROWS), alpha=alpha)
    rows_out = k_rows.shape[1]
    state = (cn1[..., :M_HEAD_DIM], cn1[..., M_HEAD_DIM], m1[:, 0, GATE_LANE:GATE_LANE + M_HEADS],
             k_rows.reshape(bsz, rows_out, A_KV_HEADS, A_HEAD_DIM),
             v_rows.reshape(bsz, rows_out, A_KV_HEADS, A_HEAD_DIM),
             rows8[:, SUBLANES - (CONV_W - 1):, :])
    return y, state


def kernel(x_prompt, x_sample, state_mlstm_C, state_mlstm_n, state_mlstm_m, cache_swa_k, cache_swa_v,
           state_conv, w_in, b_igate, b_fgate, g_mlstm_norm, attn_sinks, w_out, ln1_g, ln1_b,
           w_up, w_conv, b_conv, w_down, ln2_g, ln2_b):
    depth = w_in.shape[0]
    alpha = (2 * depth) ** 0.25
    bp, tp = x_prompt.shape[0], x_prompt.shape[1]
    bs, ts = x_sample.shape[0], x_sample.shape[1]
    tm_p = min(tp, MIXER_ROWS)
    hp, hs = x_prompt, x_sample
    sp, ss = [], []
    for l in range(depth):
        wts = _prep_weights(w_in[l], b_igate[l], b_fgate[l], g_mlstm_norm[l], attn_sinks[l], w_out[l],
                            ln1_g[l], ln1_b[l], w_up[l], w_conv[l], b_conv[l], w_down[l], ln2_g[l], ln2_b[l])
        hp, st_p = _stream_layer(
            hp, 0,
            jnp.zeros((bp, M_HEADS, M_HEAD_DIM, M_HEAD_DIM), F32),
            jnp.zeros((bp, M_HEADS, M_HEAD_DIM), F32),
            jnp.zeros((bp, M_HEADS), F32),
            jnp.zeros((bp, WINDOW, A_KV_HEADS, A_HEAD_DIM), F32),
            jnp.zeros((bp, WINDOW, A_KV_HEADS, A_HEAD_DIM), F32),
            jnp.zeros((bp, CONV_W - 1, UP_WIDTH), F32),
            wts, ch=min(CHUNK, tp), tm=tm_p, has_cache=False, alpha=alpha)
        hs, st_s = _stream_layer(
            hs, PAST_LEN, state_mlstm_C[l], state_mlstm_n[l], state_mlstm_m[l],
            cache_swa_k[l], cache_swa_v[l], state_conv[l],
            wts, ch=ts, tm=ts, has_cache=True, alpha=alpha)
        sp.append(st_p)
        ss.append(st_s)
    P = [jnp.stack([s[i] for s in sp]) for i in range(6)]
    S = [jnp.stack([s[i] for s in ss]) for i in range(6)]
    return (hp, hs, P[0], P[1], P[2], P[3], P[4], P[5], S[0], S[1], S[2], S[3], S[4], S[5])
```

```python
import functools

import jax
import jax.numpy as jnp
import numpy as np
from jax import lax
from jax.experimental import pallas as pl
from jax.experimental.pallas import tpu as pltpu

D_MODEL = 1024
M_HEADS = 4
M_HEAD_DIM = 128
M_WIDTH = M_HEADS * M_HEAD_DIM
A_Q_HEADS = 8
A_KV_HEADS = 2
A_HEAD_DIM = 64
A_GROUP = A_Q_HEADS // A_KV_HEADS
A_WIDTH = A_Q_HEADS * A_HEAD_DIM
KV_WIDTH = A_KV_HEADS * A_HEAD_DIM
WINDOW = 128
CHUNK = 64
ROPE_THETA = 10000.0
D_FF = 2816
UP_WIDTH = 2 * D_FF
CONV_W = 3
LN_EPS = 1e-5
PAST_LEN = 2048

LANES = 128
SUBLANES = 8
GATE_LANE = M_HEADS
MAIN_WIDTH = 4 * M_WIDTH + A_WIDTH + 2 * KV_WIDTH
FF_BLOCK = 256
DOWN_GROUP = 4
OUT_PIECE = 256
CHUNKS_PER_STEP = 2
MIXER_ROWS = 512
FFN_ROWS = 1024
assert M_HEADS == A_Q_HEADS // 2
VMEM_LIMIT_BYTES = 56 * 1024 * 1024

F32 = jnp.float32
BF16 = jnp.bfloat16
NEG_INF = float("-inf")


def _layer_norm(y, g, b):
    mu = jnp.mean(y, axis=-1, keepdims=True)
    d = y - mu
    var = jnp.mean(d * d, axis=-1, keepdims=True)
    return d * lax.rsqrt(var + LN_EPS) * g + b


def _log_sigmoid(x):
    return jnp.minimum(x, 0.0) - jnp.log1p(jnp.exp(-jnp.abs(x)))


def _chunk_scan(x, group, *, axis=0, reverse=False, use_max=False):
    steps = x.shape[axis]
    pos = lax.broadcasted_iota(jnp.int32, x.shape, axis) % group
    fill = NEG_INF if use_max else 0.0
    acc = x
    k = 1
    while k < group:
        if reverse:
            shifted = jnp.where(pos < group - k, pltpu.roll(acc, steps - k, axis), fill)
        else:
            shifted = jnp.where(pos >= k, pltpu.roll(acc, k, axis), fill)
        acc = jnp.maximum(acc, shifted) if use_max else acc + shifted
        k *= 2
    return acc


def _chunks_per_step(nch):
    return max(k for k in range(1, CHUNKS_PER_STEP + 1) if nch % k == 0)


def _rope(x, cos, sin_signed):
    lane = lax.broadcasted_iota(jnp.int32, x.shape, 1) % A_HEAD_DIM
    partner = jnp.where(lane < A_HEAD_DIM // 2,
                        pltpu.roll(x, LANES - A_HEAD_DIM // 2, 1),
                        pltpu.roll(x, A_HEAD_DIM // 2, 1))
    return x * cos + partner * sin_signed


def _mixer_body(sinks_ref, x_ref, cos_ref, sin_ref, wmain_ref, wgate_ref, gbias_ref, gnorm_ref, wout_ref,
                ln1g_ref, ln1b_ref, cn0_ref, m0_ref, kc_ref, vc_ref,
                h_ref, cnout_ref, mout_ref, kout_ref, vout_ref,
                q_sc, kb_sc, kf_sc, vaug_sc, o_sc, qa_sc, ka_sc, ve_sc, vo_sc,
                wi_sc, em_sc, wk_sc, dec_sc, d_sc, hh_sc, mix_sc, cn_sc, m_sc,
                *, ch, tm, nt, has_cache, rows_out, alpha):
    t = pl.program_id(1)
    nch = tm // ch
    band = WINDOW + ch
    left = lax.broadcasted_iota(jnp.int32, (1, LANES), 1) < A_HEAD_DIM
    ones_left = jnp.where(left, 1.0, 0.0).astype(BF16)
    ones_right = jnp.where(left, 0.0, 1.0).astype(BF16)

    def store_values(row_slice, nrows, va):
        swapped = pltpu.roll(va, A_HEAD_DIM, 1)
        ve_sc[0, row_slice, 0:LANES] = jnp.where(left, va, 0.0).astype(BF16)
        vo_sc[0, row_slice, 0:LANES] = jnp.where(left, 0.0, swapped).astype(BF16)
        ve_sc[1, row_slice, 0:LANES] = jnp.where(left, swapped, 0.0).astype(BF16)
        vo_sc[1, row_slice, 0:LANES] = jnp.where(left, 0.0, va).astype(BF16)
        for j in range(A_KV_HEADS):
            ve_sc[j, row_slice, LANES:2 * LANES] = jnp.broadcast_to(ones_left, (nrows, LANES))
            vo_sc[j, row_slice, LANES:2 * LANES] = jnp.broadcast_to(ones_right, (nrows, LANES))

    @pl.when(t == 0)
    def _load_state():
        cn_sc[...] = cn0_ref[0]
        m_sc[...] = m0_ref[0]
        kc = kc_ref[0].astype(BF16)
        for j in range(A_KV_HEADS):
            ka_sc[j, 0:WINDOW, :] = kc[:, j * A_HEAD_DIM:(j + 1) * A_HEAD_DIM]
        store_values(slice(0, WINDOW), WINDOW, vc_ref[0])

    if nt > 1:
        @pl.when(t > 0)
        def _slide_window():
            for j in range(A_KV_HEADS):
                ka_sc[j, 0:WINDOW, :] = ka_sc[j, tm:tm + WINDOW, :]
                ve_sc[j, 0:WINDOW, :] = ve_sc[j, tm:tm + WINDOW, :]
                vo_sc[j, 0:WINDOW, :] = vo_sc[j, tm:tm + WINDOW, :]

    xb = x_ref[0].astype(BF16)

    def proj(lo, hi):
        return jnp.dot(xb, wmain_ref[:, lo:hi], preferred_element_type=F32)

    gates = jnp.dot(xb, wgate_ref[...], preferred_element_type=F32) + gbias_ref[...]
    cos = cos_ref[...]
    sin_signed = sin_ref[...]
    attn_base = 4 * M_WIDTH

    def project_queries(pair):
        wide = proj(attn_base + 2 * pair * LANES, attn_base + (2 * pair + 2) * LANES)
        for p in (2 * pair, 2 * pair + 1):
            slab = _rope(wide[:, (p % 2) * LANES:(p % 2 + 1) * LANES], cos, sin_signed)
            slab = (slab * (A_HEAD_DIM ** -0.5)).astype(BF16)
            for e in range(2):
                pos = (p % 2) + 2 * e
                for c in range(nch):
                    qa_sc[p // 2, c, pos * ch:(pos + 1) * ch, :] = (
                        slab[c * ch:(c + 1) * ch, e * A_HEAD_DIM:(e + 1) * A_HEAD_DIM])

    def project_keys_values():
        kv = proj(attn_base + A_WIDTH, attn_base + A_WIDTH + 2 * KV_WIDTH)
        ka = _rope(kv[:, 0:KV_WIDTH], cos, sin_signed)
        va = kv[:, KV_WIDTH:2 * KV_WIDTH]
        kab = ka.astype(BF16)
        for j in range(A_KV_HEADS):
            ka_sc[j, WINDOW:WINDOW + tm, :] = kab[:, j * A_HEAD_DIM:(j + 1) * A_HEAD_DIM]
        store_values(slice(WINDOW, WINDOW + tm), tm, va)
        kout_ref[0] = ka[tm - rows_out:tm, :]
        vout_ref[0] = va[tm - rows_out:tm, :]

    project_queries(0)

    log_i = pltpu.roll(gates, GATE_LANE, 1)
    log_f = _log_sigmoid(gates)
    b_incl = _chunk_scan(log_f, ch)
    tail = _chunk_scan(log_f, ch, reverse=True) - log_f
    g_row = tail + log_i
    r_val = log_i - b_incl

    project_queries(1)

    r_cummax = _chunk_scan(r_val, ch, use_max=True)
    pos = lax.broadcasted_iota(jnp.int32, (nch, ch, LANES), 1)
    b_last = jnp.max(jnp.where(pos == ch - 1, b_incl.reshape(nch, ch, LANES), NEG_INF), axis=1)
    g_max = jnp.max(g_row.reshape(nch, ch, LANES), axis=1)
    m_run = m_sc[...]
    m_starts = []
    for c in range(nch):
        m_starts.append(m_run)
        m_run = jnp.maximum(b_last[c:c + 1, :] + m_run, g_max[c:c + 1, :])
    m_sc[...] = m_run
    m_start = jnp.concatenate(m_starts, axis=0)
    m_next = jnp.concatenate(m_starts[1:] + [m_run], axis=0)
    decay = jnp.exp(b_last + m_start - m_next)

    project_keys_values()

    def per_row(a):
        return jnp.broadcast_to(a[:, None, :], (nch, ch, LANES)).reshape(tm, LANES)

    m_rows = per_row(m_start)
    mx = jnp.maximum(m_rows, r_cummax)
    w_inter = jnp.exp(m_rows - mx)
    inv_floor = jnp.exp(-(b_incl + mx))
    w_state = jnp.exp(g_row - per_row(m_next))
    r_t = jnp.transpose(r_val)
    causal = (lax.broadcasted_iota(jnp.int32, (ch, ch), 0) >= lax.broadcasted_iota(jnp.int32, (ch, ch), 1))

    def spread_head(hd):
        ln = GATE_LANE + hd
        wi_sc[hd] = jnp.broadcast_to(w_inter[:, ln:ln + 1], (tm, LANES))
        em_sc[hd] = jnp.broadcast_to(inv_floor[:, ln:ln + 1], (tm, LANES))
        wk_sc[hd] = jnp.broadcast_to(w_state[:, ln:ln + 1], (tm, LANES))
        dec_sc[hd] = jnp.broadcast_to(decay[:, ln:ln + 1], (nch, LANES))
        mx_b = jnp.broadcast_to(mx[:, ln:ln + 1], (tm, ch))
        for c in range(nch):
            r_row = r_t[ln:ln + 1, c * ch:(c + 1) * ch]
            d_sc[hd, c] = jnp.where(causal, jnp.exp(r_row - mx_b[c * ch:(c + 1) * ch, :]), 0.0)

    q_sc[...] = proj(0, M_WIDTH).astype(BF16)
    spread_head(0)
    kf = proj(M_WIDTH, 2 * M_WIDTH) * (M_HEAD_DIM ** -0.5)
    kf_sc[...] = kf
    kb_sc[...] = kf.astype(BF16)
    spread_head(1)
    vm = proj(2 * M_WIDTH, 3 * M_WIDTH).astype(BF16)
    for hd in range(M_HEADS):
        vaug_sc[hd, :, 0:M_HEAD_DIM] = vm[:, hd * M_HEAD_DIM:(hd + 1) * M_HEAD_DIM]
        vaug_sc[hd, :, M_HEAD_DIM:2 * M_HEAD_DIM] = jnp.ones((tm, M_HEAD_DIM), BF16)
    spread_head(2)
    spread_head(3)
    o_sc[...] = jax.nn.sigmoid(proj(3 * M_WIDTH, 4 * M_WIDTH))

    band_pos = lax.broadcasted_iota(jnp.int32, (1, band), 1)
    hcols = [slice(hd * M_HEAD_DIM, (hd + 1) * M_HEAD_DIM) for hd in range(M_HEADS)]

    def head_norm(hh_rows, gate_rows, mix_rows):
        hh = [hh_sc[hh_rows, hcols[hd]] for hd in range(M_HEADS)]
        mu = [jnp.mean(a, axis=-1, keepdims=True) for a in hh]
        dlt = [hh[hd] - mu[hd] for hd in range(M_HEADS)]
        var = [jnp.mean(a * a, axis=-1, keepdims=True) for a in dlt]
        for hd in range(M_HEADS):
            hn = dlt[hd] * lax.rsqrt(var[hd] + LN_EPS) * gnorm_ref[:, hcols[hd]]
            mix_sc[mix_rows, hcols[hd]] = (o_sc[gate_rows, hcols[hd]] * hn).astype(BF16)

    nsub = _chunks_per_step(nch)
    step = nsub * ch
    assert hh_sc.shape[0] == step + tm
    hh_sc[0:step, :] = jnp.zeros((step, M_WIDTH), F32)
    nt_dims = (((1,), (1,)), ((), ()))
    tn_dims = (((0,), (0,)), ((), ()))
    heads = range(M_HEADS)
    groups = range(A_KV_HEADS)
    subs = range(nsub)
    stack_order = [i for i in range(A_GROUP) if i % 2 == 0] + [i for i in range(A_GROUP) if i % 2 == 1]
    sink_cols = [jnp.concatenate([jnp.full((ch, 1), sinks_ref[A_GROUP * j + i], F32) for i in stack_order], axis=0)
                 for j in groups]

    def loop_step(i, carry):
        base_row = pl.multiple_of(i * step, step)
        r0 = [pl.multiple_of(base_row + u * ch, ch) for u in subs]
        rows = [pl.ds(r0[u], ch) for u in subs]
        out_rows = [pl.ds(r0[u] + step, ch) for u in subs]
        cidx = [i * nsub + u for u in subs]

        qb = [[q_sc[rows[u], hcols[hd]] for hd in heads] for u in subs]
        vaug = [[vaug_sc[hd, rows[u], :] for hd in heads] for u in subs]
        kw = [[(kf_sc[rows[u], hcols[hd]] * wk_sc[hd, rows[u], :]).astype(BF16) for hd in heads] for u in subs]
        upd = [[lax.dot_general(kw[u][hd], vaug[u][hd], tn_dims, preferred_element_type=F32) for hd in heads]
               for u in subs]
        qk = [[lax.dot_general(qb[u][hd], kb_sc[rows[u], hcols[hd]], nt_dims, preferred_element_type=F32)
               for hd in heads] for u in subs]

        kband = [[ka_sc[j, pl.ds(r0[u], band), :] for j in range(A_KV_HEADS)] for u in subs]
        sc = [[lax.dot_general(qa_sc[j, cidx[u]], kband[u][j], nt_dims, preferred_element_type=F32)
               for j in groups] for u in subs]
        if not has_cache:
            for u in subs:
                first_valid = WINDOW - (t * nch + cidx[u]) * ch
                mask_bias = jnp.where(band_pos >= first_valid, 0.0, NEG_INF)
                sc[u] = [a + mask_bias for a in sc[u]]

        head_norm(pl.ds(base_row, step), pl.ds(pl.multiple_of(jnp.maximum(base_row - step, 0), step), step),
                  pl.ds(base_row, step))

        state = [[cn_sc[hd] for hd in heads]]
        for u in subs:
            nxt = []
            for hd in heads:
                dec = dec_sc[hd, pl.ds(cidx[u], 1), :]
                nxt.append(jnp.concatenate([dec, dec], axis=1) * state[u][hd] + upd[u][hd])
            state.append(nxt)
        inter = [[jnp.dot(qb[u][hd], state[u][hd].astype(BF16), preferred_element_type=F32) for hd in heads]
                 for u in subs]
        for hd in heads:
            cn_sc[hd] = state[nsub][hd]

        s = [[(qk[u][hd] * d_sc[hd, cidx[u]]).astype(BF16) for hd in heads] for u in subs]
        mx_col = [[jnp.maximum(jnp.max(sc[u][j], axis=-1, keepdims=True), sink_cols[j]) for j in groups]
                  for u in subs]
        intra = [[jnp.dot(s[u][hd], vaug[u][hd], preferred_element_type=F32) for hd in heads]
                 for u in subs]
        mx_wide = [[jnp.broadcast_to(mx_col[u][j], (A_GROUP * ch, band)) for j in groups] for u in subs]
        pr = [[jnp.exp(sc[u][j] - mx_wide[u][j]).astype(BF16) for j in groups] for u in subs]
        sink_term = [[jnp.exp(sink_cols[j] - mx_wide[u][j][:, 0:LANES]) for j in groups] for u in subs]
        half = (A_GROUP // 2) * ch
        part = [[jnp.dot(pr[u][j][0:half], ve_sc[j, pl.ds(r0[u], band), :], preferred_element_type=F32)
                 + jnp.dot(pr[u][j][half:2 * half], vo_sc[j, pl.ds(r0[u], band), :], preferred_element_type=F32)
                 for j in groups] for u in subs]

        for u in subs:
            for hd in heads:
                wi = wi_sc[hd, rows[u], :]
                num = intra[u][hd][:, 0:M_HEAD_DIM] + wi * inter[u][hd][:, 0:M_HEAD_DIM]
                den = intra[u][hd][:, M_HEAD_DIM:] + wi * inter[u][hd][:, M_HEAD_DIM:]
                hh_sc[out_rows[u], hcols[hd]] = num / jnp.maximum(jnp.abs(den), em_sc[hd, rows[u], :])
        for u in subs:
            for j in groups:
                acc = part[u][j]
                dn = acc[:, LANES:] + jnp.where(left, sink_term[u][j][0:half], sink_term[u][j][half:2 * half])
                out = (acc[:, 0:LANES] / dn).astype(BF16)
                for k in range(A_GROUP // 2):
                    p = j * (A_GROUP // 2) + k
                    mix_sc[out_rows[u], M_WIDTH + p * LANES:M_WIDTH + (p + 1) * LANES] = (
                        out[k * ch:(k + 1) * ch, :])
        return carry

    lax.fori_loop(0, nch // nsub, loop_step, 0)
    head_norm(slice(tm, tm + step), slice(tm - step, tm), slice(tm, tm + step))

    piece = min(tm, OUT_PIECE)
    mixed = [jnp.dot(mix_sc[step + lo:step + lo + piece, :], wout_ref[...], preferred_element_type=F32)
             for lo in range(0, tm, piece)]
    for i, lo in enumerate(range(0, tm, piece)):
        h_ref[0, lo:lo + piece, :] = _layer_norm(alpha * x_ref[0, lo:lo + piece, :] + mixed[i],
                                                 ln1g_ref[...], ln1b_ref[...])

    @pl.when(t == nt - 1)
    def _store_state():
        cnout_ref[0] = cn_sc[...]
        mout_ref[0] = m_sc[...]


def _ffn_body(h_ref, prev_ref, wup_ref, wconv_ref, bconv_ref, wdown_ref, ln2g_ref, ln2b_ref,
              y_ref, rows_ref, carry_sc, acc_sc, perm_sc, *, tm, nt, alpha, packed):
    t = pl.program_id(1)

    @pl.when(t == 0)
    def _load_carry():
        carry_sc[...] = prev_ref[0]

    nv = tm // SUBLANES
    pitch = nv + 1
    nl = D_MODEL // LANES
    for k in range(nl):
        for s in range(SUBLANES):
            perm_sc[k, s * pitch:s * pitch + nv, :] = h_ref[0, s * nv:(s + 1) * nv, k * LANES:(k + 1) * LANES]
    hf = jnp.concatenate(
        [jnp.concatenate([perm_sc[k, pl.ds(a, SUBLANES, stride=pitch), :] for k in range(nl)], axis=1)
         for a in range(nv)], axis=0)
    hb = hf.astype(BF16)
    nblk = D_FF // FF_BLOCK
    first_sublane = lax.broadcasted_iota(jnp.int32, (SUBLANES, FF_BLOCK), 0) == 0

    def up_project(blk, half):
        return jnp.dot(hb, wup_ref[:, half * D_FF + blk * FF_BLOCK:half * D_FF + (blk + 1) * FF_BLOCK],
                       preferred_element_type=F32)

    def down_project(acts, first_blk):
        act = acts[0] if len(acts) == 1 else jnp.concatenate(acts, axis=1)
        lo = first_blk * FF_BLOCK
        down = jnp.dot(act, wdown_ref[lo:lo + len(acts) * FF_BLOCK, :], preferred_element_type=F32)
        if first_blk == 0:
            acc_sc[...] = down
        else:
            acc_sc[...] += down

    def conv(u, blk, half):
        cols = slice(half * D_FF + blk * FF_BLOCK, half * D_FF + (blk + 1) * FF_BLOCK)
        last1 = u[tm - SUBLANES:tm, :]
        last2 = u[tm - 2 * SUBLANES:tm - SUBLANES, :]
        if packed:
            lead2, lead1 = carry_sc[0, :, cols], carry_sc[1, :, cols]
            carry_sc[0, :, cols] = last2
            carry_sc[1, :, cols] = last1
        else:
            turned2, turned1 = pltpu.roll(last2, 1, 0), pltpu.roll(last1, 1, 0)
            lead2 = jnp.where(first_sublane, carry_sc[0, :, cols], turned2)
            lead1 = jnp.where(first_sublane, carry_sc[1, :, cols], turned1)
            carry_sc[0, :, cols] = turned2
            carry_sc[1, :, cols] = turned1
        back1 = jnp.concatenate([lead1, u[0:tm - SUBLANES, :]], axis=0)
        older = [u[0:tm - 2 * SUBLANES, :]] if nv > 2 else []
        back2 = jnp.concatenate([lead2, lead1] + older, axis=0)
        w = wconv_ref[:, cols]
        return bconv_ref[:, cols] + back2 * w[0:1, :] + back1 * w[1:2, :] + u * w[2:3, :]

    u_val, u_gate = up_project(0, 0), up_project(0, 1)
    ready, pending = None, []
    for blk in range(nblk):
        more = blk + 1 < nblk
        if more:
            u_val_next = up_project(blk + 1, 0)
        val = conv(u_val, blk, 0)
        if more:
            u_gate_next = up_project(blk + 1, 1)
        gate = conv(u_gate, blk, 1)
        act = (gate * jax.nn.sigmoid(gate) * val).astype(BF16)
        if ready is not None:
            down_project(*ready)
            ready = None
        pending.append(act)
        if len(pending) == DOWN_GROUP or not more:
            ready = (pending, blk + 1 - len(pending))
            pending = []
        if more:
            u_val, u_gate = u_val_next, u_gate_next
    down_project(*ready)

    y = _layer_norm(alpha * hf + acc_sc[...], ln2g_ref[...], ln2b_ref[...])
    for k in range(nl):
        for a in range(nv):
            perm_sc[k, pl.ds(a, SUBLANES, stride=pitch), :] = (
                y[a * SUBLANES:(a + 1) * SUBLANES, k * LANES:(k + 1) * LANES])
        for s in range(SUBLANES):
            y_ref[0, s * nv:(s + 1) * nv, k * LANES:(k + 1) * LANES] = perm_sc[k, s * pitch:s * pitch + nv, :]

    @pl.when(t == nt - 1)
    def _store_rows():
        rows_ref[0] = carry_sc[...]


def _const_spec(shape):
    zeros = (0,) * len(shape)
    return pl.BlockSpec(shape, lambda b, t: zeros, pipeline_mode=pl.Buffered(1))


def _batch_spec(shape):
    zeros = (0,) * (len(shape) - 1)
    return pl.BlockSpec(shape, lambda b, t: (b,) + zeros)


def _mixer_call(x, pos0, cn0, m0, kc, vc, wts, *, ch, tm, has_cache, alpha):
    bsz, seq, _ = x.shape
    nt = seq // tm
    nch = tm // ch
    rows_out = min(WINDOW, seq)
    assert seq % tm == 0 and tm % ch == 0 and rows_out <= tm and (nt == 1 or tm >= WINDOW)
    step = _chunks_per_step(nch) * ch

    half = A_HEAD_DIM // 2
    inv = ROPE_THETA ** (-np.arange(half, dtype=np.float64) / half)
    ang = (pos0 + np.arange(seq, dtype=np.float64))[:, None] * inv[None, :]
    cos, sin = np.cos(ang), np.sin(ang)
    cos_t = jnp.asarray(np.tile(cos, (1, LANES // half)), F32)
    sin_t = jnp.asarray(np.tile(np.concatenate([-sin, sin], axis=-1), (1, LANES // A_HEAD_DIM)), F32)

    body = functools.partial(_mixer_body, ch=ch, tm=tm, nt=nt, has_cache=has_cache, rows_out=rows_out,
                             alpha=alpha)
    tile3 = pl.BlockSpec((1, tm, D_MODEL), lambda b, t: (b, t, 0))
    state_shape = (1, M_HEADS, M_HEAD_DIM, 2 * M_HEAD_DIM)
    in_specs = [
        pl.BlockSpec(memory_space=pltpu.SMEM),
        tile3,
        pl.BlockSpec((tm, LANES), lambda b, t: (t, 0)),
        pl.BlockSpec((tm, LANES), lambda b, t: (t, 0)),
        _const_spec((D_MODEL, MAIN_WIDTH)),
        _const_spec((D_MODEL, LANES)),
        _const_spec((1, LANES)),
        _const_spec((1, M_WIDTH)),
        _const_spec((M_WIDTH + A_WIDTH, D_MODEL)),
        _const_spec((1, D_MODEL)),
        _const_spec((1, D_MODEL)),
        _batch_spec(state_shape),
        _batch_spec((1, 1, LANES)),
        _batch_spec((1, WINDOW, KV_WIDTH)),
        _batch_spec((1, WINDOW, KV_WIDTH)),
    ]
    out_specs = [
        tile3,
        _batch_spec(state_shape),
        _batch_spec((1, 1, LANES)),
        _batch_spec((1, rows_out, KV_WIDTH)),
        _batch_spec((1, rows_out, KV_WIDTH)),
    ]
    out_shape = [
        jax.ShapeDtypeStruct((bsz, seq, D_MODEL), F32),
        jax.ShapeDtypeStruct((bsz,) + state_shape[1:], F32),
        jax.ShapeDtypeStruct((bsz, 1, LANES), F32),
        jax.ShapeDtypeStruct((bsz, rows_out, KV_WIDTH), F32),
        jax.ShapeDtypeStruct((bsz, rows_out, KV_WIDTH), F32),
    ]
    scratch = [
        pltpu.VMEM((tm, M_WIDTH), BF16),
        pltpu.VMEM((tm, M_WIDTH), BF16),
        pltpu.VMEM((tm, M_WIDTH), F32),
        pltpu.VMEM((M_HEADS, tm, 2 * M_HEAD_DIM), BF16),
        pltpu.VMEM((tm, M_WIDTH), F32),
        pltpu.VMEM((A_KV_HEADS, nch, A_GROUP * ch, A_HEAD_DIM), BF16),
        pltpu.VMEM((A_KV_HEADS, WINDOW + tm, A_HEAD_DIM), BF16),
        pltpu.VMEM((A_KV_HEADS, WINDOW + tm, 2 * LANES), BF16),
        pltpu.VMEM((A_KV_HEADS, WINDOW + tm, 2 * LANES), BF16),
        pltpu.VMEM((M_HEADS, tm, LANES), F32),
        pltpu.VMEM((M_HEADS, tm, LANES), F32),
        pltpu.VMEM((M_HEADS, tm, LANES), F32),
        pltpu.VMEM((M_HEADS, nch, LANES), F32),
        pltpu.VMEM((M_HEADS, nch, ch, ch), F32),
        pltpu.VMEM((step + tm, M_WIDTH), F32),
        pltpu.VMEM((step + tm, M_WIDTH + A_WIDTH), BF16),
        pltpu.VMEM((M_HEADS, M_HEAD_DIM, 2 * M_HEAD_DIM), F32),
        pltpu.VMEM((1, LANES), F32),
    ]
    return pl.pallas_call(
        body,
        grid=(bsz, nt),
        in_specs=in_specs,
        out_specs=out_specs,
        out_shape=out_shape,
        scratch_shapes=scratch,
        compiler_params=pltpu.CompilerParams(dimension_semantics=("arbitrary", "arbitrary"),
                                             vmem_limit_bytes=VMEM_LIMIT_BYTES),
    )(wts["sinks"], x, cos_t, sin_t, wts["w_main"], wts["w_gate"], wts["gate_bias"], wts["g_norm"],
      wts["w_out"], wts["ln1_g"], wts["ln1_b"], cn0, m0, kc, vc)


def _ffn_call(h, conv_prev, wts, *, alpha):
    nbatch, nrows, _ = h.shape
    packed = nbatch == SUBLANES and nbatch * nrows <= FFN_ROWS and nrows >= CONV_W - 1
    if packed:
        h = h.reshape(1, nbatch * nrows, D_MODEL)
        prev = jnp.transpose(conv_prev, (1, 0, 2))[None]
    else:
        prev = jnp.broadcast_to(conv_prev[:, :, None, :], (nbatch, CONV_W - 1, SUBLANES, UP_WIDTH))
    bsz, seq, _ = h.shape
    tm = min(seq, FFN_ROWS)
    nt = seq // tm
    assert seq % tm == 0 and tm % (2 * SUBLANES) == 0
    body = functools.partial(_ffn_body, tm=tm, nt=nt, alpha=alpha, packed=packed)
    nl = D_MODEL // LANES
    tile3 = pl.BlockSpec((1, tm, D_MODEL), lambda b, t: (b, t, 0))
    carry_shape = (CONV_W - 1, SUBLANES, UP_WIDTH)
    y, rows = pl.pallas_call(
        body,
        grid=(bsz, nt),
        in_specs=[
            tile3,
            _batch_spec((1,) + carry_shape),
            _const_spec((D_MODEL, UP_WIDTH)),
            _const_spec((CONV_W, UP_WIDTH)),
            _const_spec((1, UP_WIDTH)),
            _const_spec((D_FF, D_MODEL)),
            _const_spec((1, D_MODEL)),
            _const_spec((1, D_MODEL)),
        ],
        out_specs=[tile3, _batch_spec((1,) + carry_shape)],
        out_shape=[jax.ShapeDtypeStruct((bsz, seq, D_MODEL), F32),
                   jax.ShapeDtypeStruct((bsz,) + carry_shape, F32)],
        scratch_shapes=[pltpu.VMEM(carry_shape, F32), pltpu.VMEM((tm, D_MODEL), F32),
                        pltpu.VMEM((nl, SUBLANES * (tm // SUBLANES + 1), LANES), F32)],
        compiler_params=pltpu.CompilerParams(dimension_semantics=("arbitrary", "arbitrary"),
                                             vmem_limit_bytes=VMEM_LIMIT_BYTES),
    )(h, prev, wts["w_up"], wts["w_conv"], wts["b_conv"], wts["w_down"], wts["ln2_g"], wts["ln2_b"])
    if packed:
        return y.reshape(nbatch, nrows, D_MODEL), jnp.transpose(rows[0], (1, 0, 2))
    return y, rows[:, :, 0, :]


def _prep_weights(w_in, b_igate, b_fgate, g_mlstm_norm, attn_sinks, w_out, ln1_g, ln1_b,
                  w_up, w_conv, b_conv, w_down, ln2_g, ln2_b):
    gate_lo = 4 * M_WIDTH
    gate_hi = gate_lo + 2 * M_HEADS
    w_main = jnp.concatenate([w_in[:, :gate_lo], w_in[:, gate_hi:]], axis=1).astype(BF16)
    w_gate = jnp.pad(w_in[:, gate_lo:gate_hi], ((0, 0), (0, LANES - 2 * M_HEADS))).astype(BF16)
    gate_bias = jnp.pad(jnp.concatenate([b_igate, b_fgate]).astype(F32), (0, LANES - 2 * M_HEADS))
    row = lambda a: a.astype(F32).reshape(1, -1)
    return dict(
        sinks=attn_sinks.astype(F32), w_main=w_main, w_gate=w_gate, gate_bias=row(gate_bias),
        g_norm=row(g_mlstm_norm), w_out=w_out.astype(BF16), ln1_g=row(ln1_g), ln1_b=row(ln1_b),
        w_up=w_up.astype(BF16), w_conv=w_conv.astype(F32), b_conv=row(b_conv), w_down=w_down.astype(BF16),
        ln2_g=row(ln2_g), ln2_b=row(ln2_b))


def _stream_layer(x, pos0, c0, n0, m0, kc, vc, conv_prev, wts, *, ch, tm, has_cache, alpha):
    bsz, seq, _ = x.shape
    n_rep = jnp.broadcast_to(n0[..., None], (bsz, M_HEADS, M_HEAD_DIM, M_HEAD_DIM))
    cn0 = jnp.concatenate([c0, n_rep], axis=-1).astype(F32)
    m0 = jnp.pad(m0.astype(F32), ((0, 0), (GATE_LANE, LANES - GATE_LANE - M_HEADS))).reshape(bsz, 1, LANES)
    kc = kc.reshape(bsz, WINDOW, KV_WIDTH)
    vc = vc.reshape(bsz, WINDOW, KV_WIDTH)
    h, cn1, m1, k_rows, v_rows = _mixer_call(x, pos0, cn0, m0, kc, vc, wts, ch=ch, tm=tm,
                                             has_cache=has_cache, alpha=alpha)
    y, conv_rows = _ffn_call(h, conv_prev.astype(F32), wts, alpha=alpha)
    rows_out = k_rows.shape[1]
    state = (cn1[..., :M_HEAD_DIM], cn1[..., M_HEAD_DIM], m1[:, 0, GATE_LANE:GATE_LANE + M_HEADS],
             k_rows.reshape(bsz, rows_out, A_KV_HEADS, A_HEAD_DIM),
             v_rows.reshape(bsz, rows_out, A_KV_HEADS, A_HEAD_DIM),
             conv_rows)
    return y, state


def kernel(x_prompt, x_sample, state_mlstm_C, state_mlstm_n, state_mlstm_m, cache_swa_k, cache_swa_v,
           state_conv, w_in, b_igate, b_fgate, g_mlstm_norm, attn_sinks, w_out, ln1_g, ln1_b,
           w_up, w_conv, b_conv, w_down, ln2_g, ln2_b):
    depth = w_in.shape[0]
    alpha = (2 * depth) ** 0.25
    bp, tp = x_prompt.shape[0], x_prompt.shape[1]
    bs, ts = x_sample.shape[0], x_sample.shape[1]
    tm_p = min(tp, MIXER_ROWS)
    hp, hs = x_prompt, x_sample
    sp, ss = [], []
    for l in range(depth):
        wts = _prep_weights(w_in[l], b_igate[l], b_fgate[l], g_mlstm_norm[l], attn_sinks[l], w_out[l],
                            ln1_g[l], ln1_b[l], w_up[l], w_conv[l], b_conv[l], w_down[l], ln2_g[l], ln2_b[l])
        hp, st_p = _stream_layer(
            hp, 0,
            jnp.zeros((bp, M_HEADS, M_HEAD_DIM, M_HEAD_DIM), F32),
            jnp.zeros((bp, M_HEADS, M_HEAD_DIM), F32),
            jnp.zeros((bp, M_HEADS), F32),
            jnp.zeros((bp, WINDOW, A_KV_HEADS, A_HEAD_DIM), F32),
            jnp.zeros((bp, WINDOW, A_KV_HEADS, A_HEAD_DIM), F32),
            jnp.zeros((bp, CONV_W - 1, UP_WIDTH), F32),
            wts, ch=min(CHUNK, tp), tm=tm_p, has_cache=False, alpha=alpha)
        hs, st_s = _stream_layer(
            hs, PAST_LEN, state_mlstm_C[l], state_mlstm_n[l], state_mlstm_m[l],
            cache_swa_k[l], cache_swa_v[l], state_conv[l],
            wts, ch=ts, tm=ts, has_cache=True, alpha=alpha)
        sp.append(st_p)
        ss.append(st_s)
    P = [jnp.stack([s[i] for s in sp]) for i in range(6)]
    S = [jnp.stack([s[i] for s in ss]) for i in range(6)]
    return (hp, hs, P[0], P[1], P[2], P[3], P[4], P[5], S[0], S[1], S[2], S[3], S[4], S[5])
```

```python
import functools

import jax
import jax.numpy as jnp
import numpy as np
from jax import lax
from jax.experimental import pallas as pl
from jax.experimental.pallas import tpu as pltpu

D_MODEL = 1024
M_HEADS = 4
M_HEAD_DIM = 128
M_WIDTH = M_HEADS * M_HEAD_DIM
A_Q_HEADS = 8
A_KV_HEADS = 2
A_HEAD_DIM = 64
A_GROUP = A_Q_HEADS // A_KV_HEADS
A_WIDTH = A_Q_HEADS * A_HEAD_DIM
KV_WIDTH = A_KV_HEADS * A_HEAD_DIM
WINDOW = 128
CHUNK = 64
ROPE_THETA = 10000.0
D_FF = 2816
UP_WIDTH = 2 * D_FF
CONV_W = 3
LN_EPS = 1e-5
PAST_LEN = 2048

LANES = 128
SUBLANES = 8
GATE_LANE = M_HEADS
MAIN_WIDTH = 4 * M_WIDTH + A_WIDTH + 2 * KV_WIDTH
FF_BLOCK = 256
DOWN_GROUP = 4
OUT_PIECE = 256
CHUNKS_PER_STEP = 2
MIXER_ROWS = 512
FFN_ROWS = 1024
assert M_HEADS == A_Q_HEADS // 2
VMEM_LIMIT_BYTES = 56 * 1024 * 1024

F32 = jnp.float32
BF16 = jnp.bfloat16
NEG_INF = float("-inf")


def _layer_norm(y, g, b):
    mu = jnp.mean(y, axis=-1, keepdims=True)
    d = y - mu
    var = jnp.mean(d * d, axis=-1, keepdims=True)
    return d * lax.rsqrt(var + LN_EPS) * g + b


def _log_sigmoid(x):
    return jnp.minimum(x, 0.0) - jnp.log1p(jnp.exp(-jnp.abs(x)))


def _chunk_scan(x, group, *, axis=0, reverse=False, use_max=False):
    steps = x.shape[axis]
    pos = lax.broadcasted_iota(jnp.int32, x.shape, axis) % group
    fill = NEG_INF if use_max else 0.0
    acc = x
    k = 1
    while k < group:
        if reverse:
            shifted = jnp.where(pos < group - k, pltpu.roll(acc, steps - k, axis), fill)
        else:
            shifted = jnp.where(pos >= k, pltpu.roll(acc, k, axis), fill)
        acc = jnp.maximum(acc, shifted) if use_max else acc + shifted
        k *= 2
    return acc


def _chunks_per_step(nch):
    return max(k for k in range(1, CHUNKS_PER_STEP + 1) if nch % k == 0)


def _rope(x, cos, sin_signed):
    lane = lax.broadcasted_iota(jnp.int32, x.shape, 1) % A_HEAD_DIM
    partner = jnp.where(lane < A_HEAD_DIM // 2,
                        pltpu.roll(x, LANES - A_HEAD_DIM // 2, 1),
                        pltpu.roll(x, A_HEAD_DIM // 2, 1))
    return x * cos + partner * sin_signed


def _mixer_body(sinks_ref, x_ref, cos_ref, sin_ref, wmain_ref, wgate_ref, gbias_ref, gnorm_ref, wout_ref,
                ln1g_ref, ln1b_ref, cn0_ref, m0_ref, kc_ref, vc_ref,
                h_ref, cnout_ref, mout_ref, kout_ref, vout_ref,
                q_sc, kb_sc, kf_sc, vaug_sc, o_sc, qa_sc, ka_sc, ve_sc, vo_sc,
                wi_sc, em_sc, wk_sc, dec_sc, d_sc, hh_sc, mix_sc, cn_sc, m_sc,
                *, ch, tm, nt, has_cache, rows_out, alpha):
    t = pl.program_id(1)
    nch = tm // ch
    nsub = _chunks_per_step(nch)
    step = nsub * ch
    nstep = tm // step
    band = WINDOW + ch
    left = lax.broadcasted_iota(jnp.int32, (1, LANES), 1) < A_HEAD_DIM
    ones_left = jnp.where(left, 1.0, 0.0).astype(BF16)
    ones_right = jnp.where(left, 0.0, 1.0).astype(BF16)

    def store_values(row_slice, nrows, va):
        swapped = pltpu.roll(va, A_HEAD_DIM, 1)
        ve_sc[0, row_slice, 0:LANES] = jnp.where(left, va, 0.0).astype(BF16)
        vo_sc[0, row_slice, 0:LANES] = jnp.where(left, 0.0, swapped).astype(BF16)
        ve_sc[1, row_slice, 0:LANES] = jnp.where(left, swapped, 0.0).astype(BF16)
        vo_sc[1, row_slice, 0:LANES] = jnp.where(left, 0.0, va).astype(BF16)
        for j in range(A_KV_HEADS):
            ve_sc[j, row_slice, LANES:2 * LANES] = jnp.broadcast_to(ones_left, (nrows, LANES))
            vo_sc[j, row_slice, LANES:2 * LANES] = jnp.broadcast_to(ones_right, (nrows, LANES))

    @pl.when(t == 0)
    def _load_state():
        cn_sc[...] = cn0_ref[0]
        m_sc[...] = m0_ref[0]
        kc = kc_ref[0].astype(BF16)
        for j in range(A_KV_HEADS):
            ka_sc[j, 0:WINDOW, :] = kc[:, j * A_HEAD_DIM:(j + 1) * A_HEAD_DIM]
        store_values(slice(0, WINDOW), WINDOW, vc_ref[0])

    if nt > 1:
        @pl.when(t > 0)
        def _slide_window():
            for j in range(A_KV_HEADS):
                ka_sc[j, 0:WINDOW, :] = ka_sc[j, tm:tm + WINDOW, :]
                ve_sc[j, 0:WINDOW, :] = ve_sc[j, tm:tm + WINDOW, :]
                vo_sc[j, 0:WINDOW, :] = vo_sc[j, tm:tm + WINDOW, :]

    xb = x_ref[0].astype(BF16)

    def proj(lo, hi):
        return jnp.dot(xb, wmain_ref[:, lo:hi], preferred_element_type=F32)

    gates = jnp.dot(xb, wgate_ref[...], preferred_element_type=F32) + gbias_ref[...]
    cos = cos_ref[...]
    sin_signed = sin_ref[...]
    attn_base = 4 * M_WIDTH

    def project_queries(pair):
        wide = proj(attn_base + 2 * pair * LANES, attn_base + (2 * pair + 2) * LANES)
        for p in (2 * pair, 2 * pair + 1):
            slab = _rope(wide[:, (p % 2) * LANES:(p % 2 + 1) * LANES], cos, sin_signed)
            slab = (slab * (A_HEAD_DIM ** -0.5)).astype(BF16)
            for e in range(2):
                pos = (p % 2) + 2 * e
                for c in range(nch):
                    qa_sc[p // 2, c, pos * ch:(pos + 1) * ch, :] = (
                        slab[c * ch:(c + 1) * ch, e * A_HEAD_DIM:(e + 1) * A_HEAD_DIM])

    def project_keys_values():
        kv = proj(attn_base + A_WIDTH, attn_base + A_WIDTH + 2 * KV_WIDTH)
        ka = _rope(kv[:, 0:KV_WIDTH], cos, sin_signed)
        va = kv[:, KV_WIDTH:2 * KV_WIDTH]
        kab = ka.astype(BF16)
        for j in range(A_KV_HEADS):
            ka_sc[j, WINDOW:WINDOW + tm, :] = kab[:, j * A_HEAD_DIM:(j + 1) * A_HEAD_DIM]
        store_values(slice(WINDOW, WINDOW + tm), tm, va)
        kout_ref[0] = ka[tm - rows_out:tm, :]
        vout_ref[0] = va[tm - rows_out:tm, :]

    project_queries(0)

    log_i = pltpu.roll(gates, GATE_LANE, 1)
    log_f = _log_sigmoid(gates)
    b_incl = _chunk_scan(log_f, step)
    tail = _chunk_scan(log_f, step, reverse=True) - log_f
    g_row = tail + log_i
    r_val = log_i - b_incl

    project_queries(1)

    r_cummax = _chunk_scan(r_val, step, use_max=True)
    pos = lax.broadcasted_iota(jnp.int32, (nstep, step, LANES), 1)
    b_last = jnp.max(jnp.where(pos == step - 1, b_incl.reshape(nstep, step, LANES), NEG_INF), axis=1)
    g_max = jnp.max(g_row.reshape(nstep, step, LANES), axis=1)
    m_run = m_sc[...]
    m_starts = []
    for c in range(nstep):
        m_starts.append(m_run)
        m_run = jnp.maximum(b_last[c:c + 1, :] + m_run, g_max[c:c + 1, :])
    m_sc[...] = m_run
    m_start = jnp.concatenate(m_starts, axis=0)
    m_next = jnp.concatenate(m_starts[1:] + [m_run], axis=0)
    decay = jnp.exp(b_last + m_start - m_next)

    project_keys_values()

    def per_row(a):
        return jnp.broadcast_to(a[:, None, :], (nstep, step, LANES)).reshape(tm, LANES)

    m_rows = per_row(m_start)
    mx = jnp.maximum(m_rows, r_cummax)
    w_inter = jnp.exp(m_rows - mx)
    inv_floor = jnp.exp(-(b_incl + mx))
    w_state = jnp.exp(g_row - per_row(m_next))
    r_t = jnp.transpose(r_val)
    causal = (lax.broadcasted_iota(jnp.int32, (step, step), 0)
              >= lax.broadcasted_iota(jnp.int32, (step, step), 1))

    def spread_head(hd):
        ln = GATE_LANE + hd
        wi_sc[hd] = jnp.broadcast_to(w_inter[:, ln:ln + 1], (tm, LANES))
        em_sc[hd] = jnp.broadcast_to(inv_floor[:, ln:ln + 1], (tm, LANES))
        wk_sc[hd] = jnp.broadcast_to(w_state[:, ln:ln + 1], (tm, LANES))
        dec_sc[hd] = jnp.broadcast_to(decay[:, ln:ln + 1], (nstep, LANES))
        mx_b = jnp.broadcast_to(mx[:, ln:ln + 1], (tm, step))
        for c in range(nstep):
            r_row = r_t[ln:ln + 1, c * step:(c + 1) * step]
            d_sc[hd, c] = jnp.where(causal, jnp.exp(r_row - mx_b[c * step:(c + 1) * step, :]), 0.0)

    q_sc[...] = proj(0, M_WIDTH).astype(BF16)
    spread_head(0)
    kf = proj(M_WIDTH, 2 * M_WIDTH) * (M_HEAD_DIM ** -0.5)
    kf_sc[...] = kf
    kb_sc[...] = kf.astype(BF16)
    spread_head(1)
    vm = proj(2 * M_WIDTH, 3 * M_WIDTH).astype(BF16)
    for hd in range(M_HEADS):
        vaug_sc[hd, :, 0:M_HEAD_DIM] = vm[:, hd * M_HEAD_DIM:(hd + 1) * M_HEAD_DIM]
        vaug_sc[hd, :, M_HEAD_DIM:2 * M_HEAD_DIM] = jnp.ones((tm, M_HEAD_DIM), BF16)
    spread_head(2)
    spread_head(3)
    o_sc[...] = jax.nn.sigmoid(proj(3 * M_WIDTH, 4 * M_WIDTH))

    band_pos = lax.broadcasted_iota(jnp.int32, (1, band), 1)
    hcols = [slice(hd * M_HEAD_DIM, (hd + 1) * M_HEAD_DIM) for hd in range(M_HEADS)]

    def head_norm(hh_rows, gate_rows, mix_rows):
        hh = [hh_sc[hh_rows, hcols[hd]] for hd in range(M_HEADS)]
        mu = [jnp.mean(a, axis=-1, keepdims=True) for a in hh]
        dlt = [hh[hd] - mu[hd] for hd in range(M_HEADS)]
        var = [jnp.mean(a * a, axis=-1, keepdims=True) for a in dlt]
        for hd in range(M_HEADS):
            hn = dlt[hd] * lax.rsqrt(var[hd] + LN_EPS) * gnorm_ref[:, hcols[hd]]
            mix_sc[mix_rows, hcols[hd]] = (o_sc[gate_rows, hcols[hd]] * hn).astype(BF16)

    assert hh_sc.shape[0] == step + tm
    hh_sc[0:step, :] = jnp.zeros((step, M_WIDTH), F32)
    nt_dims = (((1,), (1,)), ((), ()))
    tn_dims = (((0,), (0,)), ((), ()))
    heads = range(M_HEADS)
    groups = range(A_KV_HEADS)
    subs = range(nsub)
    stack_order = [i for i in range(A_GROUP) if i % 2 == 0] + [i for i in range(A_GROUP) if i % 2 == 1]
    sink_cols = [jnp.concatenate([jnp.full((ch, 1), sinks_ref[A_GROUP * j + i], F32) for i in stack_order], axis=0)
                 for j in groups]

    def loop_step(i, carry):
        base_row = pl.multiple_of(i * step, step)
        r0 = [pl.multiple_of(base_row + u * ch, ch) for u in subs]
        rows = [pl.ds(r0[u], ch) for u in subs]
        out_rows = [pl.ds(r0[u] + step, ch) for u in subs]
        cidx = [i * nsub + u for u in subs]

        mrows = pl.ds(base_row, step)
        qb = [q_sc[mrows, hcols[hd]] for hd in heads]
        vaug = [vaug_sc[hd, mrows, :] for hd in heads]
        kw = [(kf_sc[mrows, hcols[hd]] * wk_sc[hd, mrows, :]).astype(BF16) for hd in heads]
        upd = [lax.dot_general(kw[hd], vaug[hd], tn_dims, preferred_element_type=F32) for hd in heads]
        qk = [lax.dot_general(qb[hd], kb_sc[mrows, hcols[hd]], nt_dims, preferred_element_type=F32)
              for hd in heads]
        state = [cn_sc[hd] for hd in heads]
        inter = [jnp.dot(qb[hd], state[hd].astype(BF16), preferred_element_type=F32)
                 for hd in heads]

        kband = [[ka_sc[j, pl.ds(r0[u], band), :] for j in range(A_KV_HEADS)] for u in subs]
        sc = [[lax.dot_general(qa_sc[j, cidx[u]], kband[u][j], nt_dims, preferred_element_type=F32)
               for j in groups] for u in subs]
        if not has_cache:
            for u in subs:
                first_valid = WINDOW - (t * nch + cidx[u]) * ch
                mask_bias = jnp.where(band_pos >= first_valid, 0.0, NEG_INF)
                sc[u] = [a + mask_bias for a in sc[u]]

        head_norm(pl.ds(base_row, step), pl.ds(pl.multiple_of(jnp.maximum(base_row - step, 0), step), step),
                  pl.ds(base_row, step))

        s = [(qk[hd] * d_sc[hd, i]).astype(BF16) for hd in heads]
        mx_col = [[jnp.maximum(jnp.max(sc[u][j], axis=-1, keepdims=True), sink_cols[j]) for j in groups]
                  for u in subs]
        intra = [jnp.dot(s[hd], vaug[hd], preferred_element_type=F32) for hd in heads]
        mx_wide = [[jnp.broadcast_to(mx_col[u][j], (A_GROUP * ch, band)) for j in groups] for u in subs]
        pr = [[jnp.exp(sc[u][j] - mx_wide[u][j]).astype(BF16) for j in groups] for u in subs]
        sink_term = [[jnp.exp(sink_cols[j] - mx_wide[u][j][:, 0:LANES]) for j in groups] for u in subs]
        half = (A_GROUP // 2) * ch
        part = [[jnp.dot(pr[u][j][0:half], ve_sc[j, pl.ds(r0[u], band), :], preferred_element_type=F32)
                 + jnp.dot(pr[u][j][half:2 * half], vo_sc[j, pl.ds(r0[u], band), :], preferred_element_type=F32)
                 for j in groups] for u in subs]

        for hd in heads:
            wi = wi_sc[hd, mrows, :]
            num = intra[hd][:, 0:M_HEAD_DIM] + wi * inter[hd][:, 0:M_HEAD_DIM]
            den = intra[hd][:, M_HEAD_DIM:] + wi * inter[hd][:, M_HEAD_DIM:]
            hh_sc[pl.ds(base_row + step, step), hcols[hd]] = num / jnp.maximum(jnp.abs(den), em_sc[hd, mrows, :])
            dec = dec_sc[hd, pl.ds(i, 1), :]
            cn_sc[hd] = jnp.concatenate([dec, dec], axis=1) * state[hd] + upd[hd]
        for u in subs:
            for j in groups:
                acc = part[u][j]
                dn = acc[:, LANES:] + jnp.where(left, sink_term[u][j][0:half], sink_term[u][j][half:2 * half])
                out = (acc[:, 0:LANES] / dn).astype(BF16)
                for k in range(A_GROUP // 2):
                    p = j * (A_GROUP // 2) + k
                    mix_sc[out_rows[u], M_WIDTH + p * LANES:M_WIDTH + (p + 1) * LANES] = (
                        out[k * ch:(k + 1) * ch, :])
        return carry

    lax.fori_loop(0, nch // nsub, loop_step, 0)
    head_norm(slice(tm, tm + step), slice(tm - step, tm), slice(tm, tm + step))

    piece = min(tm, OUT_PIECE)
    mixed = [jnp.dot(mix_sc[step + lo:step + lo + piece, :], wout_ref[...], preferred_element_type=F32)
             for lo in range(0, tm, piece)]
    for i, lo in enumerate(range(0, tm, piece)):
        h_ref[0, lo:lo + piece, :] = _layer_norm(alpha * x_ref[0, lo:lo + piece, :] + mixed[i],
                                                 ln1g_ref[...], ln1b_ref[...])

    @pl.when(t == nt - 1)
    def _store_state():
        cnout_ref[0] = cn_sc[...]
        mout_ref[0] = m_sc[...]


def _ffn_body(h_ref, prev_ref, wup_ref, wconv_ref, bconv_ref, wdown_ref, ln2g_ref, ln2b_ref,
              y_ref, rows_ref, carry_sc, acc_sc, perm_sc, *, tm, nt, alpha, packed):
    t = pl.program_id(1)

    @pl.when(t == 0)
    def _load_carry():
        carry_sc[...] = prev_ref[0]

    nv = tm // SUBLANES
    pitch = nv + 1
    nl = D_MODEL // LANES
    for k in range(nl):
        for s in range(SUBLANES):
            perm_sc[k, s * pitch:s * pitch + nv, :] = h_ref[0, s * nv:(s + 1) * nv, k * LANES:(k + 1) * LANES]
    hf = jnp.concatenate(
        [jnp.concatenate([perm_sc[k, pl.ds(a, SUBLANES, stride=pitch), :] for k in range(nl)], axis=1)
         for a in range(nv)], axis=0)
    hb = hf.astype(BF16)
    nblk = D_FF // FF_BLOCK
    first_sublane = lax.broadcasted_iota(jnp.int32, (SUBLANES, FF_BLOCK), 0) == 0

    def up_project(blk, half):
        return jnp.dot(hb, wup_ref[:, half * D_FF + blk * FF_BLOCK:half * D_FF + (blk + 1) * FF_BLOCK],
                       preferred_element_type=F32)

    def down_project(acts, first_blk):
        act = acts[0] if len(acts) == 1 else jnp.concatenate(acts, axis=1)
        lo = first_blk * FF_BLOCK
        down = jnp.dot(act, wdown_ref[lo:lo + len(acts) * FF_BLOCK, :], preferred_element_type=F32)
        if first_blk == 0:
            acc_sc[...] = down
        else:
            acc_sc[...] += down

    def conv(u, blk, half):
        cols = slice(half * D_FF + blk * FF_BLOCK, half * D_FF + (blk + 1) * FF_BLOCK)
        last1 = u[tm - SUBLANES:tm, :]
        last2 = u[tm - 2 * SUBLANES:tm - SUBLANES, :]
        if packed:
            lead2, lead1 = carry_sc[0, :, cols], carry_sc[1, :, cols]
            carry_sc[0, :, cols] = last2
            carry_sc[1, :, cols] = last1
        else:
            turned2, turned1 = pltpu.roll(last2, 1, 0), pltpu.roll(last1, 1, 0)
            lead2 = jnp.where(first_sublane, carry_sc[0, :, cols], turned2)
            lead1 = jnp.where(first_sublane, carry_sc[1, :, cols], turned1)
            carry_sc[0, :, cols] = turned2
            carry_sc[1, :, cols] = turned1
        back1 = jnp.concatenate([lead1, u[0:tm - SUBLANES, :]], axis=0)
        older = [u[0:tm - 2 * SUBLANES, :]] if nv > 2 else []
        back2 = jnp.concatenate([lead2, lead1] + older, axis=0)
        w = wconv_ref[:, cols]
        return bconv_ref[:, cols] + back2 * w[0:1, :] + back1 * w[1:2, :] + u * w[2:3, :]

    u_val, u_gate = up_project(0, 0), up_project(0, 1)
    ready, pending = None, []
    for blk in range(nblk):
        more = blk + 1 < nblk
        if more:
            u_val_next = up_project(blk + 1, 0)
        val = conv(u_val, blk, 0)
        if more:
            u_gate_next = up_project(blk + 1, 1)
        gate = conv(u_gate, blk, 1)
        act = (gate * jax.nn.sigmoid(gate) * val).astype(BF16)
        if ready is not None:
            down_project(*ready)
            ready = None
        pending.append(act)
        if len(pending) == DOWN_GROUP or not more:
            ready = (pending, blk + 1 - len(pending))
            pending = []
        if more:
            u_val, u_gate = u_val_next, u_gate_next
    down_project(*ready)

    y = _layer_norm(alpha * hf + acc_sc[...], ln2g_ref[...], ln2b_ref[...])
    for k in range(nl):
        for a in range(nv):
            perm_sc[k, pl.ds(a, SUBLANES, stride=pitch), :] = (
                y[a * SUBLANES:(a + 1) * SUBLANES, k * LANES:(k + 1) * LANES])
        for s in range(SUBLANES):
            y_ref[0, s * nv:(s + 1) * nv, k * LANES:(k + 1) * LANES] = perm_sc[k, s * pitch:s * pitch + nv, :]

    @pl.when(t == nt - 1)
    def _store_rows():
        rows_ref[0] = carry_sc[...]


def _const_spec(shape):
    zeros = (0,) * len(shape)
    return pl.BlockSpec(shape, lambda b, t: zeros, pipeline_mode=pl.Buffered(1))


def _batch_spec(shape):
    zeros = (0,) * (len(shape) - 1)
    return pl.BlockSpec(shape, lambda b, t: (b,) + zeros)


def _mixer_call(x, pos0, cn0, m0, kc, vc, wts, *, ch, tm, has_cache, alpha):
    bsz, seq, _ = x.shape
    nt = seq // tm
    nch = tm // ch
    rows_out = min(WINDOW, seq)
    assert seq % tm == 0 and tm % ch == 0 and rows_out <= tm and (nt == 1 or tm >= WINDOW)
    step = _chunks_per_step(nch) * ch

    half = A_HEAD_DIM // 2
    inv = ROPE_THETA ** (-np.arange(half, dtype=np.float64) / half)
    ang = (pos0 + np.arange(seq, dtype=np.float64))[:, None] * inv[None, :]
    cos, sin = np.cos(ang), np.sin(ang)
    cos_t = jnp.asarray(np.tile(cos, (1, LANES // half)), F32)
    sin_t = jnp.asarray(np.tile(np.concatenate([-sin, sin], axis=-1), (1, LANES // A_HEAD_DIM)), F32)

    body = functools.partial(_mixer_body, ch=ch, tm=tm, nt=nt, has_cache=has_cache, rows_out=rows_out,
                             alpha=alpha)
    tile3 = pl.BlockSpec((1, tm, D_MODEL), lambda b, t: (b, t, 0))
    state_shape = (1, M_HEADS, M_HEAD_DIM, 2 * M_HEAD_DIM)
    in_specs = [
        pl.BlockSpec(memory_space=pltpu.SMEM),
        tile3,
        pl.BlockSpec((tm, LANES), lambda b, t: (t, 0)),
        pl.BlockSpec((tm, LANES), lambda b, t: (t, 0)),
        _const_spec((D_MODEL, MAIN_WIDTH)),
        _const_spec((D_MODEL, LANES)),
        _const_spec((1, LANES)),
        _const_spec((1, M_WIDTH)),
        _const_spec((M_WIDTH + A_WIDTH, D_MODEL)),
        _const_spec((1, D_MODEL)),
        _const_spec((1, D_MODEL)),
        _batch_spec(state_shape),
        _batch_spec((1, 1, LANES)),
        _batch_spec((1, WINDOW, KV_WIDTH)),
        _batch_spec((1, WINDOW, KV_WIDTH)),
    ]
    out_specs = [
        tile3,
        _batch_spec(state_shape),
        _batch_spec((1, 1, LANES)),
        _batch_spec((1, rows_out, KV_WIDTH)),
        _batch_spec((1, rows_out, KV_WIDTH)),
    ]
    out_shape = [
        jax.ShapeDtypeStruct((bsz, seq, D_MODEL), F32),
        jax.ShapeDtypeStruct((bsz,) + state_shape[1:], F32),
        jax.ShapeDtypeStruct((bsz, 1, LANES), F32),
        jax.ShapeDtypeStruct((bsz, rows_out, KV_WIDTH), F32),
        jax.ShapeDtypeStruct((bsz, rows_out, KV_WIDTH), F32),
    ]
    scratch = [
        pltpu.VMEM((tm, M_WIDTH), BF16),
        pltpu.VMEM((tm, M_WIDTH), BF16),
        pltpu.VMEM((tm, M_WIDTH), F32),
        pltpu.VMEM((M_HEADS, tm, 2 * M_HEAD_DIM), BF16),
        pltpu.VMEM((tm, M_WIDTH), F32),
        pltpu.VMEM((A_KV_HEADS, nch, A_GROUP * ch, A_HEAD_DIM), BF16),
        pltpu.VMEM((A_KV_HEADS, WINDOW + tm, A_HEAD_DIM), BF16),
        pltpu.VMEM((A_KV_HEADS, WINDOW + tm, 2 * LANES), BF16),
        pltpu.VMEM((A_KV_HEADS, WINDOW + tm, 2 * LANES), BF16),
        pltpu.VMEM((M_HEADS, tm, LANES), F32),
        pltpu.VMEM((M_HEADS, tm, LANES), F32),
        pltpu.VMEM((M_HEADS, tm, LANES), F32),
        pltpu.VMEM((M_HEADS, tm // step, LANES), F32),
        pltpu.VMEM((M_HEADS, tm // step, step, step), F32),
        pltpu.VMEM((step + tm, M_WIDTH), F32),
        pltpu.VMEM((step + tm, M_WIDTH + A_WIDTH), BF16),
        pltpu.VMEM((M_HEADS, M_HEAD_DIM, 2 * M_HEAD_DIM), F32),
        pltpu.VMEM((1, LANES), F32),
    ]
    return pl.pallas_call(
        body,
        grid=(bsz, nt),
        in_specs=in_specs,
        out_specs=out_specs,
        out_shape=out_shape,
        scratch_shapes=scratch,
        compiler_params=pltpu.CompilerParams(dimension_semantics=("arbitrary", "arbitrary"),
                                             vmem_limit_bytes=VMEM_LIMIT_BYTES),
    )(wts["sinks"], x, cos_t, sin_t, wts["w_main"], wts["w_gate"], wts["gate_bias"], wts["g_norm"],
      wts["w_out"], wts["ln1_g"], wts["ln1_b"], cn0, m0, kc, vc)


def _ffn_call(h, conv_prev, wts, *, alpha):
    nbatch, nrows, _ = h.shape
    packed = nbatch == SUBLANES and nbatch * nrows <= FFN_ROWS and nrows >= CONV_W - 1
    if packed:
        h = h.reshape(1, nbatch * nrows, D_MODEL)
        prev = jnp.transpose(conv_prev, (1, 0, 2))[None]
    else:
        prev = jnp.broadcast_to(conv_prev[:, :, None, :], (nbatch, CONV_W - 1, SUBLANES, UP_WIDTH))
    bsz, seq, _ = h.shape
    tm = min(seq, FFN_ROWS)
    nt = seq // tm
    assert seq % tm == 0 and tm % (2 * SUBLANES) == 0
    body = functools.partial(_ffn_body, tm=tm, nt=nt, alpha=alpha, packed=packed)
    nl = D_MODEL // LANES
    tile3 = pl.BlockSpec((1, tm, D_MODEL), lambda b, t: (b, t, 0))
    carry_shape = (CONV_W - 1, SUBLANES, UP_WIDTH)
    y, rows = pl.pallas_call(
        body,
        grid=(bsz, nt),
        in_specs=[
            tile3,
            _batch_spec((1,) + carry_shape),
            _const_spec((D_MODEL, UP_WIDTH)),
            _const_spec((CONV_W, UP_WIDTH)),
            _const_spec((1, UP_WIDTH)),
            _const_spec((D_FF, D_MODEL)),
            _const_spec((1, D_MODEL)),
            _const_spec((1, D_MODEL)),
        ],
        out_specs=[tile3, _batch_spec((1,) + carry_shape)],
        out_shape=[jax.ShapeDtypeStruct((bsz, seq, D_MODEL), F32),
                   jax.ShapeDtypeStruct((bsz,) + carry_shape, F32)],
        scratch_shapes=[pltpu.VMEM(carry_shape, F32), pltpu.VMEM((tm, D_MODEL), F32),
                        pltpu.VMEM((nl, SUBLANES * (tm // SUBLANES + 1), LANES), F32)],
        compiler_params=pltpu.CompilerParams(dimension_semantics=("arbitrary", "arbitrary"),
                                             vmem_limit_bytes=VMEM_LIMIT_BYTES),
    )(h, prev, wts["w_up"], wts["w_conv"], wts["b_conv"], wts["w_down"], wts["ln2_g"], wts["ln2_b"])
    if packed:
        return y.reshape(nbatch, nrows, D_MODEL), jnp.transpose(rows[0], (1, 0, 2))
    return y, rows[:, :, 0, :]


def _prep_weights(w_in, b_igate, b_fgate, g_mlstm_norm, attn_sinks, w_out, ln1_g, ln1_b,
                  w_up, w_conv, b_conv, w_down, ln2_g, ln2_b):
    gate_lo = 4 * M_WIDTH
    gate_hi = gate_lo + 2 * M_HEADS
    w_main = jnp.concatenate([w_in[:, :gate_lo], w_in[:, gate_hi:]], axis=1).astype(BF16)
    w_gate = jnp.pad(w_in[:, gate_lo:gate_hi], ((0, 0), (0, LANES - 2 * M_HEADS))).astype(BF16)
    gate_bias = jnp.pad(jnp.concatenate([b_igate, b_fgate]).astype(F32), (0, LANES - 2 * M_HEADS))
    row = lambda a: a.astype(F32).reshape(1, -1)
    return dict(
        sinks=attn_sinks.astype(F32), w_main=w_main, w_gate=w_gate, gate_bias=row(gate_bias),
        g_norm=row(g_mlstm_norm), w_out=w_out.astype(BF16), ln1_g=row(ln1_g), ln1_b=row(ln1_b),
        w_up=w_up.astype(BF16), w_conv=w_conv.astype(F32), b_conv=row(b_conv), w_down=w_down.astype(BF16),
        ln2_g=row(ln2_g), ln2_b=row(ln2_b))


def _stream_layer(x, pos0, c0, n0, m0, kc, vc, conv_prev, wts, *, ch, tm, has_cache, alpha):
    bsz, seq, _ = x.shape
    n_rep = jnp.broadcast_to(n0[..., None], (bsz, M_HEADS, M_HEAD_DIM, M_HEAD_DIM))
    cn0 = jnp.concatenate([c0, n_rep], axis=-1).astype(F32)
    m0 = jnp.pad(m0.astype(F32), ((0, 0), (GATE_LANE, LANES - GATE_LANE - M_HEADS))).reshape(bsz, 1, LANES)
    kc = kc.reshape(bsz, WINDOW, KV_WIDTH)
    vc = vc.reshape(bsz, WINDOW, KV_WIDTH)
    h, cn1, m1, k_rows, v_rows = _mixer_call(x, pos0, cn0, m0, kc, vc, wts, ch=ch, tm=tm,
                                             has_cache=has_cache, alpha=alpha)
    y, conv_rows = _ffn_call(h, conv_prev.astype(F32), wts, alpha=alpha)
    rows_out = k_rows.shape[1]
    state = (cn1[..., :M_HEAD_DIM], cn1[..., M_HEAD_DIM], m1[:, 0, GATE_LANE:GATE_LANE + M_HEADS],
             k_rows.reshape(bsz, rows_out, A_KV_HEADS, A_HEAD_DIM),
             v_rows.reshape(bsz, rows_out, A_KV_HEADS, A_HEAD_DIM),
             conv_rows)
    return y, state


def kernel(x_prompt, x_sample, state_mlstm_C, state_mlstm_n, state_mlstm_m, cache_swa_k, cache_swa_v,
           state_conv, w_in, b_igate, b_fgate, g_mlstm_norm, attn_sinks, w_out, ln1_g, ln1_b,
           w_up, w_conv, b_conv, w_down, ln2_g, ln2_b):
    depth = w_in.shape[0]
    alpha = (2 * depth) ** 0.25
    bp, tp = x_prompt.shape[0], x_prompt.shape[1]
    bs, ts = x_sample.shape[0], x_sample.shape[1]
    tm_p = min(tp, MIXER_ROWS)
    hp, hs = x_prompt, x_sample
    sp, ss = [], []
    for l in range(depth):
        wts = _prep_weights(w_in[l], b_igate[l], b_fgate[l], g_mlstm_norm[l], attn_sinks[l], w_out[l],
                            ln1_g[l], ln1_b[l], w_up[l], w_conv[l], b_conv[l], w_down[l], ln2_g[l], ln2_b[l])
        hp, st_p = _stream_layer(
            hp, 0,
            jnp.zeros((bp, M_HEADS, M_HEAD_DIM, M_HEAD_DIM), F32),
            jnp.zeros((bp, M_HEADS, M_HEAD_DIM), F32),
            jnp.zeros((bp, M_HEADS), F32),
            jnp.zeros((bp, WINDOW, A_KV_HEADS, A_HEAD_DIM), F32),
            jnp.zeros((bp, WINDOW, A_KV_HEADS, A_HEAD_DIM), F32),
            jnp.zeros((bp, CONV_W - 1, UP_WIDTH), F32),
            wts, ch=min(CHUNK, tp), tm=tm_p, has_cache=False, alpha=alpha)
        hs, st_s = _stream_layer(
            hs, PAST_LEN, state_mlstm_C[l], state_mlstm_n[l], state_mlstm_m[l],
            cache_swa_k[l], cache_swa_v[l], state_conv[l],
            wts, ch=ts, tm=ts, has_cache=True, alpha=alpha)
        sp.append(st_p)
        ss.append(st_s)
    P = [jnp.stack([s[i] for s in sp]) for i in range(6)]
    S = [jnp.stack([s[i] for s in ss]) for i in range(6)]
    return (hp, hs, P[0], P[1], P[2], P[3], P[4], P[5], S[0], S[1], S[2], S[3], S[4], S[5])
```

```python
import functools

import jax
import jax.numpy as jnp
import numpy as np
from jax import lax
from jax.experimental import pallas as pl
from jax.experimental.pallas import tpu as pltpu

D_MODEL = 1024
M_HEADS = 4
M_HEAD_DIM = 128
M_WIDTH = M_HEADS * M_HEAD_DIM
A_Q_HEADS = 8
A_KV_HEADS = 2
A_HEAD_DIM = 64
A_GROUP = A_Q_HEADS // A_KV_HEADS
A_WIDTH = A_Q_HEADS * A_HEAD_DIM
KV_WIDTH = A_KV_HEADS * A_HEAD_DIM
WINDOW = 128
CHUNK = 64
ROPE_THETA = 10000.0
D_FF = 2816
UP_WIDTH = 2 * D_FF
CONV_W = 3
LN_EPS = 1e-5
PAST_LEN = 2048

LANES = 128
SUBLANES = 8
GATE_LANE = M_HEADS
MAIN_WIDTH = 4 * M_WIDTH + A_WIDTH + 2 * KV_WIDTH
FF_BLOCK = 256
DOWN_GROUP = 4
OUT_PIECE = 256
CHUNKS_PER_STEP = 2
MIXER_ROWS = 512
FFN_ROWS = 1024
assert M_HEADS == A_Q_HEADS // 2
VMEM_LIMIT_BYTES = 56 * 1024 * 1024

F32 = jnp.float32
BF16 = jnp.bfloat16
NEG_INF = float("-inf")


def _layer_norm(y, g, b):
    mu = jnp.mean(y, axis=-1, keepdims=True)
    d = y - mu
    var = jnp.mean(d * d, axis=-1, keepdims=True)
    return d * lax.rsqrt(var + LN_EPS) * g + b


def _log_sigmoid(x):
    return jnp.minimum(x, 0.0) - jnp.log1p(jnp.exp(-jnp.abs(x)))


def _chunk_scan(x, group, *, axis=0, reverse=False, use_max=False):
    steps = x.shape[axis]
    pos = lax.broadcasted_iota(jnp.int32, x.shape, axis) % group
    fill = NEG_INF if use_max else 0.0
    acc = x
    k = 1
    while k < group:
        if reverse:
            shifted = jnp.where(pos < group - k, pltpu.roll(acc, steps - k, axis), fill)
        else:
            shifted = jnp.where(pos >= k, pltpu.roll(acc, k, axis), fill)
        acc = jnp.maximum(acc, shifted) if use_max else acc + shifted
        k *= 2
    return acc


def _chunks_per_step(nch):
    return max(k for k in range(1, CHUNKS_PER_STEP + 1) if nch % k == 0)


def _rope(x, cos, sin_signed):
    lane = lax.broadcasted_iota(jnp.int32, x.shape, 1) % A_HEAD_DIM
    partner = jnp.where(lane < A_HEAD_DIM // 2,
                        pltpu.roll(x, LANES - A_HEAD_DIM // 2, 1),
                        pltpu.roll(x, A_HEAD_DIM // 2, 1))
    return x * cos + partner * sin_signed


def _mixer_body(sinks_ref, x_ref, cos_ref, sin_ref, wmain_ref, wgate_ref, gbias_ref, gnorm_ref, wout_ref,
                ln1g_ref, ln1b_ref, cn0_ref, m0_ref, kc_ref, vc_ref,
                h_ref, cnout_ref, mout_ref, kout_ref, vout_ref,
                q_sc, kb_sc, kf_sc, vaug_sc, o_sc, qa_sc, ka_sc, ve_sc, vo_sc,
                wi_sc, em_sc, wk_sc, dec_sc, d_sc, hh_sc, mix_sc, cn_sc, m_sc,
                *, ch, tm, nt, has_cache, rows_out, alpha):
    t = pl.program_id(1)
    nch = tm // ch
    nsub = _chunks_per_step(nch)
    step = nsub * ch
    nstep = tm // step
    band = WINDOW + ch
    left = lax.broadcasted_iota(jnp.int32, (1, LANES), 1) < A_HEAD_DIM
    ones_left = jnp.where(left, 1.0, 0.0).astype(BF16)
    ones_right = jnp.where(left, 0.0, 1.0).astype(BF16)

    def store_values(row_slice, nrows, va):
        swapped = pltpu.roll(va, A_HEAD_DIM, 1)
        ve_sc[0, row_slice, 0:LANES] = jnp.where(left, va, 0.0).astype(BF16)
        vo_sc[0, row_slice, 0:LANES] = jnp.where(left, 0.0, swapped).astype(BF16)
        ve_sc[1, row_slice, 0:LANES] = jnp.where(left, swapped, 0.0).astype(BF16)
        vo_sc[1, row_slice, 0:LANES] = jnp.where(left, 0.0, va).astype(BF16)
        for j in range(A_KV_HEADS):
            ve_sc[j, row_slice, LANES:2 * LANES] = jnp.broadcast_to(ones_left, (nrows, LANES))
            vo_sc[j, row_slice, LANES:2 * LANES] = jnp.broadcast_to(ones_right, (nrows, LANES))

    @pl.when(t == 0)
    def _load_state():
        cn_sc[...] = cn0_ref[0]
        m_sc[...] = m0_ref[0]
        kc = kc_ref[0].astype(BF16)
        for j in range(A_KV_HEADS):
            ka_sc[j, 0:WINDOW, :] = kc[:, j * A_HEAD_DIM:(j + 1) * A_HEAD_DIM]
        store_values(slice(0, WINDOW), WINDOW, vc_ref[0])

    if nt > 1:
        @pl.when(t > 0)
        def _slide_window():
            for j in range(A_KV_HEADS):
                ka_sc[j, 0:WINDOW, :] = ka_sc[j, tm:tm + WINDOW, :]
                ve_sc[j, 0:WINDOW, :] = ve_sc[j, tm:tm + WINDOW, :]
                vo_sc[j, 0:WINDOW, :] = vo_sc[j, tm:tm + WINDOW, :]

    xb = x_ref[0].astype(BF16)

    def proj(lo, hi):
        return jnp.dot(xb, wmain_ref[:, lo:hi], preferred_element_type=F32)

    gates = jnp.dot(xb, wgate_ref[...], preferred_element_type=F32) + gbias_ref[...]
    cos = cos_ref[...]
    sin_signed = sin_ref[...]
    attn_base = 4 * M_WIDTH

    def project_queries(pair):
        wide = proj(attn_base + 2 * pair * LANES, attn_base + (2 * pair + 2) * LANES)
        for p in (2 * pair, 2 * pair + 1):
            slab = _rope(wide[:, (p % 2) * LANES:(p % 2 + 1) * LANES], cos, sin_signed)
            slab = (slab * (A_HEAD_DIM ** -0.5)).astype(BF16)
            for e in range(2):
                pos = (p % 2) + 2 * e
                for c in range(nch):
                    qa_sc[p // 2, c, pos * ch:(pos + 1) * ch, :] = (
                        slab[c * ch:(c + 1) * ch, e * A_HEAD_DIM:(e + 1) * A_HEAD_DIM])

    def project_keys_values():
        kv = proj(attn_base + A_WIDTH, attn_base + A_WIDTH + 2 * KV_WIDTH)
        ka = _rope(kv[:, 0:KV_WIDTH], cos, sin_signed)
        va = kv[:, KV_WIDTH:2 * KV_WIDTH]
        kab = ka.astype(BF16)
        for j in range(A_KV_HEADS):
            ka_sc[j, WINDOW:WINDOW + tm, :] = kab[:, j * A_HEAD_DIM:(j + 1) * A_HEAD_DIM]
        store_values(slice(WINDOW, WINDOW + tm), tm, va)
        kout_ref[0] = ka[tm - rows_out:tm, :]
        vout_ref[0] = va[tm - rows_out:tm, :]

    project_queries(0)

    log_i = pltpu.roll(gates, GATE_LANE, 1)
    log_f = _log_sigmoid(gates)
    b_incl = _chunk_scan(log_f, step)
    tail = _chunk_scan(log_f, step, reverse=True) - log_f
    g_row = tail + log_i
    r_val = log_i - b_incl

    project_queries(1)

    r_cummax = _chunk_scan(r_val, step, use_max=True)
    pos = lax.broadcasted_iota(jnp.int32, (nstep, step, LANES), 1)
    b_last = jnp.max(jnp.where(pos == step - 1, b_incl.reshape(nstep, step, LANES), NEG_INF), axis=1)
    g_max = jnp.max(g_row.reshape(nstep, step, LANES), axis=1)
    m_run = m_sc[...]
    m_starts = []
    for c in range(nstep):
        m_starts.append(m_run)
        m_run = jnp.maximum(b_last[c:c + 1, :] + m_run, g_max[c:c + 1, :])
    m_sc[...] = m_run
    m_start = jnp.concatenate(m_starts, axis=0)
    m_next = jnp.concatenate(m_starts[1:] + [m_run], axis=0)
    decay = jnp.exp(b_last + m_start - m_next)

    project_keys_values()

    def per_row(a):
        return jnp.broadcast_to(a[:, None, :], (nstep, step, LANES)).reshape(tm, LANES)

    m_rows = per_row(m_start)
    mx = jnp.maximum(m_rows, r_cummax)
    w_inter = jnp.exp(m_rows - mx)
    inv_floor = jnp.exp(-(b_incl + mx))
    w_state = jnp.exp(g_row - per_row(m_next))
    r_t = jnp.transpose(r_val)
    causal = (lax.broadcasted_iota(jnp.int32, (step, step), 0)
              >= lax.broadcasted_iota(jnp.int32, (step, step), 1))

    def spread_head(hd):
        ln = GATE_LANE + hd
        wi_sc[hd] = jnp.broadcast_to(w_inter[:, ln:ln + 1], (tm, LANES))
        em_sc[hd] = jnp.broadcast_to(inv_floor[:, ln:ln + 1], (tm, LANES))
        wk_sc[hd] = jnp.broadcast_to(w_state[:, ln:ln + 1], (tm, LANES))
        dec_sc[hd] = jnp.broadcast_to(decay[:, ln:ln + 1], (nstep, LANES))
        mx_b = jnp.broadcast_to(mx[:, ln:ln + 1], (tm, step))
        for c in range(nstep):
            r_row = r_t[ln:ln + 1, c * step:(c + 1) * step]
            d_sc[hd, c] = jnp.where(causal, jnp.exp(r_row - mx_b[c * step:(c + 1) * step, :]), 0.0)

    q_sc[...] = proj(0, M_WIDTH).astype(BF16)
    spread_head(0)
    kf = proj(M_WIDTH, 2 * M_WIDTH) * (M_HEAD_DIM ** -0.5)
    kf_sc[...] = kf
    kb_sc[...] = kf.astype(BF16)
    spread_head(1)
    vm = proj(2 * M_WIDTH, 3 * M_WIDTH).astype(BF16)
    for hd in range(M_HEADS):
        vaug_sc[hd, :, 0:M_HEAD_DIM] = vm[:, hd * M_HEAD_DIM:(hd + 1) * M_HEAD_DIM]
        vaug_sc[hd, :, M_HEAD_DIM:2 * M_HEAD_DIM] = jnp.ones((tm, M_HEAD_DIM), BF16)
    spread_head(2)
    spread_head(3)
    o_sc[...] = jax.nn.sigmoid(proj(3 * M_WIDTH, 4 * M_WIDTH))

    band_pos = lax.broadcasted_iota(jnp.int32, (1, band), 1)
    hcols = [slice(hd * M_HEAD_DIM, (hd + 1) * M_HEAD_DIM) for hd in range(M_HEADS)]

    def head_norm(hh_rows, gate_rows, mix_rows):
        hh = [hh_sc[hh_rows, hcols[hd]] for hd in range(M_HEADS)]
        mu = [jnp.mean(a, axis=-1, keepdims=True) for a in hh]
        dlt = [hh[hd] - mu[hd] for hd in range(M_HEADS)]
        var = [jnp.mean(a * a, axis=-1, keepdims=True) for a in dlt]
        for hd in range(M_HEADS):
            hn = dlt[hd] * lax.rsqrt(var[hd] + LN_EPS) * gnorm_ref[:, hcols[hd]]
            mix_sc[mix_rows, hcols[hd]] = (o_sc[gate_rows, hcols[hd]] * hn).astype(BF16)

    assert hh_sc.shape[0] == step + tm
    hh_sc[0:step, :] = jnp.zeros((step, M_WIDTH), F32)
    nt_dims = (((1,), (1,)), ((), ()))
    tn_dims = (((0,), (0,)), ((), ()))
    heads = range(M_HEADS)
    groups = range(A_KV_HEADS)
    subs = range(nsub)
    stack_order = [i for i in range(A_GROUP) if i % 2 == 0] + [i for i in range(A_GROUP) if i % 2 == 1]
    sink_cols = [jnp.concatenate([jnp.full((ch, 1), sinks_ref[A_GROUP * j + i], F32) for i in stack_order], axis=0)
                 for j in groups]

    def loop_step(i, carry):
        base_row = i * step
        r0 = [base_row + u * ch for u in subs]
        rows = [pl.ds(r0[u], ch) for u in subs]
        out_rows = [pl.ds(r0[u] + step, ch) for u in subs]
        cidx = [i * nsub + u for u in subs]

        mrows = pl.ds(base_row, step)
        qb = [q_sc[mrows, hcols[hd]] for hd in heads]
        vaug = [vaug_sc[hd, mrows, :] for hd in heads]
        kw = [(kf_sc[mrows, hcols[hd]] * wk_sc[hd, mrows, :]).astype(BF16) for hd in heads]
        upd = [lax.dot_general(kw[hd], vaug[hd], tn_dims, preferred_element_type=F32) for hd in heads]
        qk = [lax.dot_general(qb[hd], kb_sc[mrows, hcols[hd]], nt_dims, preferred_element_type=F32)
              for hd in heads]
        state = [cn_sc[hd] for hd in heads]
        inter = [jnp.dot(qb[hd], state[hd].astype(BF16), preferred_element_type=F32)
                 for hd in heads]

        kband = [[ka_sc[j, pl.ds(r0[u], band), :] for j in range(A_KV_HEADS)] for u in subs]
        sc = [[lax.dot_general(qa_sc[j, cidx[u]], kband[u][j], nt_dims, preferred_element_type=F32)
               for j in groups] for u in subs]
        if not has_cache:
            for u in subs:
                first_valid = WINDOW - (t * nch + cidx[u]) * ch
                mask_bias = jnp.where(band_pos >= first_valid, 0.0, NEG_INF)
                sc[u] = [a + mask_bias for a in sc[u]]

        head_norm(pl.ds(base_row, step), pl.ds(max(base_row - step, 0), step), pl.ds(base_row, step))

        s = [(qk[hd] * d_sc[hd, i]).astype(BF16) for hd in heads]
        mx_col = [[jnp.maximum(jnp.max(sc[u][j], axis=-1, keepdims=True), sink_cols[j]) for j in groups]
                  for u in subs]
        intra = [jnp.dot(s[hd], vaug[hd], preferred_element_type=F32) for hd in heads]
        mx_wide = [[jnp.broadcast_to(mx_col[u][j], (A_GROUP * ch, band)) for j in groups] for u in subs]
        pr = [[jnp.exp(sc[u][j] - mx_wide[u][j]).astype(BF16) for j in groups] for u in subs]
        sink_term = [[jnp.exp(sink_cols[j] - mx_wide[u][j][:, 0:LANES]) for j in groups] for u in subs]
        half = (A_GROUP // 2) * ch
        part = [[jnp.dot(pr[u][j][0:half], ve_sc[j, pl.ds(r0[u], band), :], preferred_element_type=F32)
                 + jnp.dot(pr[u][j][half:2 * half], vo_sc[j, pl.ds(r0[u], band), :], preferred_element_type=F32)
                 for j in groups] for u in subs]

        for hd in heads:
            wi = wi_sc[hd, mrows, :]
            num = intra[hd][:, 0:M_HEAD_DIM] + wi * inter[hd][:, 0:M_HEAD_DIM]
            den = intra[hd][:, M_HEAD_DIM:] + wi * inter[hd][:, M_HEAD_DIM:]
            hh_sc[pl.ds(base_row + step, step), hcols[hd]] = num / jnp.maximum(jnp.abs(den), em_sc[hd, mrows, :])
            dec = dec_sc[hd, pl.ds(i, 1), :]
            cn_sc[hd] = jnp.concatenate([dec, dec], axis=1) * state[hd] + upd[hd]
        for u in subs:
            for j in groups:
                acc = part[u][j]
                dn = acc[:, LANES:] + jnp.where(left, sink_term[u][j][0:half], sink_term[u][j][half:2 * half])
                out = (acc[:, 0:LANES] / dn).astype(BF16)
                for k in range(A_GROUP // 2):
                    p = j * (A_GROUP // 2) + k
                    mix_sc[out_rows[u], M_WIDTH + p * LANES:M_WIDTH + (p + 1) * LANES] = (
                        out[k * ch:(k + 1) * ch, :])
        return carry

    for i in range(nstep):
        loop_step(i, 0)
    head_norm(slice(tm, tm + step), slice(tm - step, tm), slice(tm, tm + step))

    piece = min(tm, OUT_PIECE)
    mixed = [jnp.dot(mix_sc[step + lo:step + lo + piece, :], wout_ref[...], preferred_element_type=F32)
             for lo in range(0, tm, piece)]
    for i, lo in enumerate(range(0, tm, piece)):
        h_ref[0, lo:lo + piece, :] = _layer_norm(alpha * x_ref[0, lo:lo + piece, :] + mixed[i],
                                                 ln1g_ref[...], ln1b_ref[...])

    @pl.when(t == nt - 1)
    def _store_state():
        cnout_ref[0] = cn_sc[...]
        mout_ref[0] = m_sc[...]


def _ffn_body(h_ref, prev_ref, wup_ref, wconv_ref, bconv_ref, wdown_ref, ln2g_ref, ln2b_ref,
              y_ref, rows_ref, carry_sc, acc_sc, perm_sc, *, tm, nt, alpha, packed):
    t = pl.program_id(1)

    @pl.when(t == 0)
    def _load_carry():
        carry_sc[...] = prev_ref[0]

    nv = tm // SUBLANES
    pitch = nv + 1
    nl = D_MODEL // LANES
    for k in range(nl):
        for s in range(SUBLANES):
            perm_sc[k, s * pitch:s * pitch + nv, :] = h_ref[0, s * nv:(s + 1) * nv, k * LANES:(k + 1) * LANES]
    hf = jnp.concatenate(
        [jnp.concatenate([perm_sc[k, pl.ds(a, SUBLANES, stride=pitch), :] for k in range(nl)], axis=1)
         for a in range(nv)], axis=0)
    hb = hf.astype(BF16)
    nblk = D_FF // FF_BLOCK
    first_sublane = lax.broadcasted_iota(jnp.int32, (SUBLANES, FF_BLOCK), 0) == 0

    def up_project(blk, half):
        return jnp.dot(hb, wup_ref[:, half * D_FF + blk * FF_BLOCK:half * D_FF + (blk + 1) * FF_BLOCK],
                       preferred_element_type=F32)

    def down_project(acts, first_blk):
        act = acts[0] if len(acts) == 1 else jnp.concatenate(acts, axis=1)
        lo = first_blk * FF_BLOCK
        down = jnp.dot(act, wdown_ref[lo:lo + len(acts) * FF_BLOCK, :], preferred_element_type=F32)
        if first_blk == 0:
            acc_sc[...] = down
        else:
            acc_sc[...] += down

    def conv(u, blk, half):
        cols = slice(half * D_FF + blk * FF_BLOCK, half * D_FF + (blk + 1) * FF_BLOCK)
        last1 = u[tm - SUBLANES:tm, :]
        last2 = u[tm - 2 * SUBLANES:tm - SUBLANES, :]
        if packed:
            lead2, lead1 = carry_sc[0, :, cols], carry_sc[1, :, cols]
            carry_sc[0, :, cols] = last2
            carry_sc[1, :, cols] = last1
        else:
            turned2, turned1 = pltpu.roll(last2, 1, 0), pltpu.roll(last1, 1, 0)
            lead2 = jnp.where(first_sublane, carry_sc[0, :, cols], turned2)
            lead1 = jnp.where(first_sublane, carry_sc[1, :, cols], turned1)
            carry_sc[0, :, cols] = turned2
            carry_sc[1, :, cols] = turned1
        back1 = jnp.concatenate([lead1, u[0:tm - SUBLANES, :]], axis=0)
        older = [u[0:tm - 2 * SUBLANES, :]] if nv > 2 else []
        back2 = jnp.concatenate([lead2, lead1] + older, axis=0)
        w = wconv_ref[:, cols]
        return bconv_ref[:, cols] + back2 * w[0:1, :] + back1 * w[1:2, :] + u * w[2:3, :]

    u_val, u_gate = up_project(0, 0), up_project(0, 1)
    ready, pending = None, []
    for blk in range(nblk):
        more = blk + 1 < nblk
        if more:
            u_val_next = up_project(blk + 1, 0)
        val = conv(u_val, blk, 0)
        if more:
            u_gate_next = up_project(blk + 1, 1)
        gate = conv(u_gate, blk, 1)
        act = (gate * jax.nn.sigmoid(gate) * val).astype(BF16)
        if ready is not None:
            down_project(*ready)
            ready = None
        pending.append(act)
        if len(pending) == DOWN_GROUP or not more:
            ready = (pending, blk + 1 - len(pending))
            pending = []
        if more:
            u_val, u_gate = u_val_next, u_gate_next
    down_project(*ready)

    y = _layer_norm(alpha * hf + acc_sc[...], ln2g_ref[...], ln2b_ref[...])
    for k in range(nl):
        for a in range(nv):
            perm_sc[k, pl.ds(a, SUBLANES, stride=pitch), :] = (
                y[a * SUBLANES:(a + 1) * SUBLANES, k * LANES:(k + 1) * LANES])
        for s in range(SUBLANES):
            y_ref[0, s * nv:(s + 1) * nv, k * LANES:(k + 1) * LANES] = perm_sc[k, s * pitch:s * pitch + nv, :]

    @pl.when(t == nt - 1)
    def _store_rows():
        rows_ref[0] = carry_sc[...]


def _const_spec(shape):
    zeros = (0,) * len(shape)
    return pl.BlockSpec(shape, lambda b, t: zeros, pipeline_mode=pl.Buffered(1))


def _batch_spec(shape):
    zeros = (0,) * (len(shape) - 1)
    return pl.BlockSpec(shape, lambda b, t: (b,) + zeros)


def _mixer_call(x, pos0, cn0, m0, kc, vc, wts, *, ch, tm, has_cache, alpha):
    bsz, seq, _ = x.shape
    nt = seq // tm
    nch = tm // ch
    rows_out = min(WINDOW, seq)
    assert seq % tm == 0 and tm % ch == 0 and rows_out <= tm and (nt == 1 or tm >= WINDOW)
    step = _chunks_per_step(nch) * ch

    half = A_HEAD_DIM // 2
    inv = ROPE_THETA ** (-np.arange(half, dtype=np.float64) / half)
    ang = (pos0 + np.arange(seq, dtype=np.float64))[:, None] * inv[None, :]
    cos, sin = np.cos(ang), np.sin(ang)
    cos_t = jnp.asarray(np.tile(cos, (1, LANES // half)), F32)
    sin_t = jnp.asarray(np.tile(np.concatenate([-sin, sin], axis=-1), (1, LANES // A_HEAD_DIM)), F32)

    body = functools.partial(_mixer_body, ch=ch, tm=tm, nt=nt, has_cache=has_cache, rows_out=rows_out,
                             alpha=alpha)
    tile3 = pl.BlockSpec((1, tm, D_MODEL), lambda b, t: (b, t, 0))
    state_shape = (1, M_HEADS, M_HEAD_DIM, 2 * M_HEAD_DIM)
    in_specs = [
        pl.BlockSpec(memory_space=pltpu.SMEM),
        tile3,
        pl.BlockSpec((tm, LANES), lambda b, t: (t, 0)),
        pl.BlockSpec((tm, LANES), lambda b, t: (t, 0)),
        _const_spec((D_MODEL, MAIN_WIDTH)),
        _const_spec((D_MODEL, LANES)),
        _const_spec((1, LANES)),
        _const_spec((1, M_WIDTH)),
        _const_spec((M_WIDTH + A_WIDTH, D_MODEL)),
        _const_spec((1, D_MODEL)),
        _const_spec((1, D_MODEL)),
        _batch_spec(state_shape),
        _batch_spec((1, 1, LANES)),
        _batch_spec((1, WINDOW, KV_WIDTH)),
        _batch_spec((1, WINDOW, KV_WIDTH)),
    ]
    out_specs = [
        tile3,
        _batch_spec(state_shape),
        _batch_spec((1, 1, LANES)),
        _batch_spec((1, rows_out, KV_WIDTH)),
        _batch_spec((1, rows_out, KV_WIDTH)),
    ]
    out_shape = [
        jax.ShapeDtypeStruct((bsz, seq, D_MODEL), F32),
        jax.ShapeDtypeStruct((bsz,) + state_shape[1:], F32),
        jax.ShapeDtypeStruct((bsz, 1, LANES), F32),
        jax.ShapeDtypeStruct((bsz, rows_out, KV_WIDTH), F32),
        jax.ShapeDtypeStruct((bsz, rows_out, KV_WIDTH), F32),
    ]
    scratch = [
        pltpu.VMEM((tm, M_WIDTH), BF16),
        pltpu.VMEM((tm, M_WIDTH), BF16),
        pltpu.VMEM((tm, M_WIDTH), F32),
        pltpu.VMEM((M_HEADS, tm, 2 * M_HEAD_DIM), BF16),
        pltpu.VMEM((tm, M_WIDTH), F32),
        pltpu.VMEM((A_KV_HEADS, nch, A_GROUP * ch, A_HEAD_DIM), BF16),
        pltpu.VMEM((A_KV_HEADS, WINDOW + tm, A_HEAD_DIM), BF16),
        pltpu.VMEM((A_KV_HEADS, WINDOW + tm, 2 * LANES), BF16),
        pltpu.VMEM((A_KV_HEADS, WINDOW + tm, 2 * LANES), BF16),
        pltpu.VMEM((M_HEADS, tm, LANES), F32),
        pltpu.VMEM((M_HEADS, tm, LANES), F32),
        pltpu.VMEM((M_HEADS, tm, LANES), F32),
        pltpu.VMEM((M_HEADS, tm // step, LANES), F32),
        pltpu.VMEM((M_HEADS, tm // step, step, step), F32),
        pltpu.VMEM((step + tm, M_WIDTH), F32),
        pltpu.VMEM((step + tm, M_WIDTH + A_WIDTH), BF16),
        pltpu.VMEM((M_HEADS, M_HEAD_DIM, 2 * M_HEAD_DIM), F32),
        pltpu.VMEM((1, LANES), F32),
    ]
    return pl.pallas_call(
        body,
        grid=(bsz, nt),
        in_specs=in_specs,
        out_specs=out_specs,
        out_shape=out_shape,
        scratch_shapes=scratch,
        compiler_params=pltpu.CompilerParams(dimension_semantics=("arbitrary", "arbitrary"),
                                             vmem_limit_bytes=VMEM_LIMIT_BYTES),
    )(wts["sinks"], x, cos_t, sin_t, wts["w_main"], wts["w_gate"], wts["gate_bias"], wts["g_norm"],
      wts["w_out"], wts["ln1_g"], wts["ln1_b"], cn0, m0, kc, vc)


def _ffn_call(h, conv_prev, wts, *, alpha):
    nbatch, nrows, _ = h.shape
    packed = nbatch == SUBLANES and nbatch * nrows <= FFN_ROWS and nrows >= CONV_W - 1
    if packed:
        h = h.reshape(1, nbatch * nrows, D_MODEL)
        prev = jnp.transpose(conv_prev, (1, 0, 2))[None]
    else:
        prev = jnp.broadcast_to(conv_prev[:, :, None, :], (nbatch, CONV_W - 1, SUBLANES, UP_WIDTH))
    bsz, seq, _ = h.shape
    tm = min(seq, FFN_ROWS)
    nt = seq // tm
    assert seq % tm == 0 and tm % (2 * SUBLANES) == 0
    body = functools.partial(_ffn_body, tm=tm, nt=nt, alpha=alpha, packed=packed)
    nl = D_MODEL // LANES
    tile3 = pl.BlockSpec((1, tm, D_MODEL), lambda b, t: (b, t, 0))
    carry_shape = (CONV_W - 1, SUBLANES, UP_WIDTH)
    y, rows = pl.pallas_call(
        body,
        grid=(bsz, nt),
        in_specs=[
            tile3,
            _batch_spec((1,) + carry_shape),
            _const_spec((D_MODEL, UP_WIDTH)),
            _const_spec((CONV_W, UP_WIDTH)),
            _const_spec((1, UP_WIDTH)),
            _const_spec((D_FF, D_MODEL)),
            _const_spec((1, D_MODEL)),
            _const_spec((1, D_MODEL)),
        ],
        out_specs=[tile3, _batch_spec((1,) + carry_shape)],
        out_shape=[jax.ShapeDtypeStruct((bsz, seq, D_MODEL), F32),
                   jax.ShapeDtypeStruct((bsz,) + carry_shape, F32)],
        scratch_shapes=[pltpu.VMEM(carry_shape, F32), pltpu.VMEM((tm, D_MODEL), F32),
                        pltpu.VMEM((nl, SUBLANES * (tm // SUBLANES + 1), LANES), F32)],
        compiler_params=pltpu.CompilerParams(dimension_semantics=("arbitrary", "arbitrary"),
                                             vmem_limit_bytes=VMEM_LIMIT_BYTES),
    )(h, prev, wts["w_up"], wts["w_conv"], wts["b_conv"], wts["w_down"], wts["ln2_g"], wts["ln2_b"])
    if packed:
        return y.reshape(nbatch, nrows, D_MODEL), jnp.transpose(rows[0], (1, 0, 2))
    return y, rows[:, :, 0, :]


def _prep_weights(w_in, b_igate, b_fgate, g_mlstm_norm, attn_sinks, w_out, ln1_g, ln1_b,
                  w_up, w_conv, b_conv, w_down, ln2_g, ln2_b):
    gate_lo = 4 * M_WIDTH
    gate_hi = gate_lo + 2 * M_HEADS
    w_main = jnp.concatenate([w_in[:, :gate_lo], w_in[:, gate_hi:]], axis=1).astype(BF16)
    w_gate = jnp.pad(w_in[:, gate_lo:gate_hi], ((0, 0), (0, LANES - 2 * M_HEADS))).astype(BF16)
    gate_bias = jnp.pad(jnp.concatenate([b_igate, b_fgate]).astype(F32), (0, LANES - 2 * M_HEADS))
    row = lambda a: a.astype(F32).reshape(1, -1)
    return dict(
        sinks=attn_sinks.astype(F32), w_main=w_main, w_gate=w_gate, gate_bias=row(gate_bias),
        g_norm=row(g_mlstm_norm), w_out=w_out.astype(BF16), ln1_g=row(ln1_g), ln1_b=row(ln1_b),
        w_up=w_up.astype(BF16), w_conv=w_conv.astype(F32), b_conv=row(b_conv), w_down=w_down.astype(BF16),
        ln2_g=row(ln2_g), ln2_b=row(ln2_b))


def _stream_layer(x, pos0, c0, n0, m0, kc, vc, conv_prev, wts, *, ch, tm, has_cache, alpha):
    bsz, seq, _ = x.shape
    n_rep = jnp.broadcast_to(n0[..., None], (bsz, M_HEADS, M_HEAD_DIM, M_HEAD_DIM))
    cn0 = jnp.concatenate([c0, n_rep], axis=-1).astype(F32)
    m0 = jnp.pad(m0.astype(F32), ((0, 0), (GATE_LANE, LANES - GATE_LANE - M_HEADS))).reshape(bsz, 1, LANES)
    kc = kc.reshape(bsz, WINDOW, KV_WIDTH)
    vc = vc.reshape(bsz, WINDOW, KV_WIDTH)
    h, cn1, m1, k_rows, v_rows = _mixer_call(x, pos0, cn0, m0, kc, vc, wts, ch=ch, tm=tm,
                                             has_cache=has_cache, alpha=alpha)
    y, conv_rows = _ffn_call(h, conv_prev.astype(F32), wts, alpha=alpha)
    rows_out = k_rows.shape[1]
    state = (cn1[..., :M_HEAD_DIM], cn1[..., M_HEAD_DIM], m1[:, 0, GATE_LANE:GATE_LANE + M_HEADS],
             k_rows.reshape(bsz, rows_out, A_KV_HEADS, A_HEAD_DIM),
             v_rows.reshape(bsz, rows_out, A_KV_HEADS, A_HEAD_DIM),
             conv_rows)
    return y, state


def kernel(x_prompt, x_sample, state_mlstm_C, state_mlstm_n, state_mlstm_m, cache_swa_k, cache_swa_v,
           state_conv, w_in, b_igate, b_fgate, g_mlstm_norm, attn_sinks, w_out, ln1_g, ln1_b,
           w_up, w_conv, b_conv, w_down, ln2_g, ln2_b):
    depth = w_in.shape[0]
    alpha = (2 * depth) ** 0.25
    bp, tp = x_prompt.shape[0], x_prompt.shape[1]
    bs, ts = x_sample.shape[0], x_sample.shape[1]
    tm_p = min(tp, MIXER_ROWS)
    hp, hs = x_prompt, x_sample
    sp, ss = [], []
    for l in range(depth):
        wts = _prep_weights(w_in[l], b_igate[l], b_fgate[l], g_mlstm_norm[l], attn_sinks[l], w_out[l],
                            ln1_g[l], ln1_b[l], w_up[l], w_conv[l], b_conv[l], w_down[l], ln2_g[l], ln2_b[l])
        hp, st_p = _stream_layer(
            hp, 0,
            jnp.zeros((bp, M_HEADS, M_HEAD_DIM, M_HEAD_DIM), F32),
            jnp.zeros((bp, M_HEADS, M_HEAD_DIM), F32),
            jnp.zeros((bp, M_HEADS), F32),
            jnp.zeros((bp, WINDOW, A_KV_HEADS, A_HEAD_DIM), F32),
            jnp.zeros((bp, WINDOW, A_KV_HEADS, A_HEAD_DIM), F32),
            jnp.zeros((bp, CONV_W - 1, UP_WIDTH), F32),
            wts, ch=min(CHUNK, tp), tm=tm_p, has_cache=False, alpha=alpha)
        hs, st_s = _stream_layer(
            hs, PAST_LEN, state_mlstm_C[l], state_mlstm_n[l], state_mlstm_m[l],
            cache_swa_k[l], cache_swa_v[l], state_conv[l],
            wts, ch=ts, tm=ts, has_cache=True, alpha=alpha)
        sp.append(st_p)
        ss.append(st_s)
    P = [jnp.stack([s[i] for s in sp]) for i in range(6)]
    S = [jnp.stack([s[i] for s in ss]) for i in range(6)]
    return (hp, hs, P[0], P[1], P[2], P[3], P[4], P[5], S[0], S[1], S[2], S[3], S[4], S[5])
```

```python
import functools

import jax
import jax.numpy as jnp
import numpy as np
from jax import lax
from jax.experimental import pallas as pl
from jax.experimental.pallas import tpu as pltpu

D_MODEL = 1024
M_HEADS = 4
M_HEAD_DIM = 128
M_WIDTH = M_HEADS * M_HEAD_DIM
A_Q_HEADS = 8
A_KV_HEADS = 2
A_HEAD_DIM = 64
A_GROUP = A_Q_HEADS // A_KV_HEADS
A_WIDTH = A_Q_HEADS * A_HEAD_DIM
KV_WIDTH = A_KV_HEADS * A_HEAD_DIM
WINDOW = 128
CHUNK = 64
ROPE_THETA = 10000.0
D_FF = 2816
UP_WIDTH = 2 * D_FF
CONV_W = 3
LN_EPS = 1e-5
PAST_LEN = 2048

LANES = 128
SUBLANES = 8
GATE_LANE = M_HEADS
MAIN_WIDTH = 4 * M_WIDTH + A_WIDTH + 2 * KV_WIDTH
FF_BLOCK = 256
OUT_PIECE = 256
CHUNKS_PER_STEP = 2
MIXER_ROWS = 512
FFN_ROWS = 1024
assert M_HEADS == A_Q_HEADS // 2
VMEM_LIMIT_BYTES = 56 * 1024 * 1024

F32 = jnp.float32
BF16 = jnp.bfloat16
NEG_INF = float("-inf")


def _layer_norm(y, g, b):
    mu = jnp.mean(y, axis=-1, keepdims=True)
    d = y - mu
    var = jnp.mean(d * d, axis=-1, keepdims=True)
    return d * lax.rsqrt(var + LN_EPS) * g + b


def _log_sigmoid(x):
    return jnp.minimum(x, 0.0) - jnp.log1p(jnp.exp(-jnp.abs(x)))


def _chunk_scan(x, group, *, axis=0, reverse=False, use_max=False):
    steps = x.shape[axis]
    pos = lax.broadcasted_iota(jnp.int32, x.shape, axis) % group
    fill = NEG_INF if use_max else 0.0
    acc = x
    k = 1
    while k < group:
        if reverse:
            shifted = jnp.where(pos < group - k, pltpu.roll(acc, steps - k, axis), fill)
        else:
            shifted = jnp.where(pos >= k, pltpu.roll(acc, k, axis), fill)
        acc = jnp.maximum(acc, shifted) if use_max else acc + shifted
        k *= 2
    return acc


def _chunks_per_step(nch):
    return max(k for k in range(1, CHUNKS_PER_STEP + 1) if nch % k == 0)


def _rope(x, cos, sin_signed):
    lane = lax.broadcasted_iota(jnp.int32, x.shape, 1) % A_HEAD_DIM
    partner = jnp.where(lane < A_HEAD_DIM // 2,
                        pltpu.roll(x, LANES - A_HEAD_DIM // 2, 1),
                        pltpu.roll(x, A_HEAD_DIM // 2, 1))
    return x * cos + partner * sin_signed


def _mixer_body(sinks_ref, x_ref, cos_ref, sin_ref, wmain_ref, wgate_ref, gbias_ref, gnorm_ref, wout_ref,
                ln1g_ref, ln1b_ref, cn0_ref, m0_ref, kc_ref, vc_ref,
                h_ref, cnout_ref, mout_ref, kout_ref, vout_ref,
                q_sc, kb_sc, kf_sc, vaug_sc, o_sc, qa_sc, ka_sc, ve_sc, vo_sc,
                wi_sc, em_sc, wk_sc, dec_sc, d_sc, hh_sc, mix_sc, cn_sc, m_sc,
                *, ch, tm, nt, has_cache, rows_out, alpha):
    t = pl.program_id(1)
    nch = tm // ch
    nsub = _chunks_per_step(nch)
    step = nsub * ch
    nstep = tm // step
    band = WINDOW + ch
    left = lax.broadcasted_iota(jnp.int32, (1, LANES), 1) < A_HEAD_DIM
    ones_left = jnp.where(left, 1.0, 0.0).astype(BF16)
    ones_right = jnp.where(left, 0.0, 1.0).astype(BF16)

    def store_values(row_slice, nrows, va):
        swapped = pltpu.roll(va, A_HEAD_DIM, 1)
        ve_sc[0, row_slice, 0:LANES] = jnp.where(left, va, 0.0).astype(BF16)
        vo_sc[0, row_slice, 0:LANES] = jnp.where(left, 0.0, swapped).astype(BF16)
        ve_sc[1, row_slice, 0:LANES] = jnp.where(left, swapped, 0.0).astype(BF16)
        vo_sc[1, row_slice, 0:LANES] = jnp.where(left, 0.0, va).astype(BF16)
        for j in range(A_KV_HEADS):
            ve_sc[j, row_slice, LANES:2 * LANES] = jnp.broadcast_to(ones_left, (nrows, LANES))
            vo_sc[j, row_slice, LANES:2 * LANES] = jnp.broadcast_to(ones_right, (nrows, LANES))

    @pl.when(t == 0)
    def _load_state():
        cn_sc[...] = cn0_ref[0]
        m_sc[...] = m0_ref[0]
        kc = kc_ref[0].astype(BF16)
        for j in range(A_KV_HEADS):
            ka_sc[j, 0:WINDOW, :] = kc[:, j * A_HEAD_DIM:(j + 1) * A_HEAD_DIM]
        store_values(slice(0, WINDOW), WINDOW, vc_ref[0])

    if nt > 1:
        @pl.when(t > 0)
        def _slide_window():
            for j in range(A_KV_HEADS):
                ka_sc[j, 0:WINDOW, :] = ka_sc[j, tm:tm + WINDOW, :]
                ve_sc[j, 0:WINDOW, :] = ve_sc[j, tm:tm + WINDOW, :]
                vo_sc[j, 0:WINDOW, :] = vo_sc[j, tm:tm + WINDOW, :]

    xb = x_ref[0].astype(BF16)

    def proj(lo, hi):
        return jnp.dot(xb, wmain_ref[:, lo:hi], preferred_element_type=F32)

    gates = jnp.dot(xb, wgate_ref[...], preferred_element_type=F32) + gbias_ref[...]
    cos = cos_ref[...]
    sin_signed = sin_ref[...]
    attn_base = 4 * M_WIDTH

    def project_queries(pair):
        wide = proj(attn_base + 2 * pair * LANES, attn_base + (2 * pair + 2) * LANES)
        for p in (2 * pair, 2 * pair + 1):
            slab = _rope(wide[:, (p % 2) * LANES:(p % 2 + 1) * LANES], cos, sin_signed)
            slab = (slab * (A_HEAD_DIM ** -0.5)).astype(BF16)
            for e in range(2):
                pos = (p % 2) + 2 * e
                for c in range(nch):
                    qa_sc[p // 2, c, pos * ch:(pos + 1) * ch, :] = (
                        slab[c * ch:(c + 1) * ch, e * A_HEAD_DIM:(e + 1) * A_HEAD_DIM])

    def project_keys_values():
        kv = proj(attn_base + A_WIDTH, attn_base + A_WIDTH + 2 * KV_WIDTH)
        ka = _rope(kv[:, 0:KV_WIDTH], cos, sin_signed)
        va = kv[:, KV_WIDTH:2 * KV_WIDTH]
        kab = ka.astype(BF16)
        for j in range(A_KV_HEADS):
            ka_sc[j, WINDOW:WINDOW + tm, :] = kab[:, j * A_HEAD_DIM:(j + 1) * A_HEAD_DIM]
        store_values(slice(WINDOW, WINDOW + tm), tm, va)
        kout_ref[0] = ka[tm - rows_out:tm, :]
        vout_ref[0] = va[tm - rows_out:tm, :]

    project_queries(0)

    log_i = pltpu.roll(gates, GATE_LANE, 1)
    log_f = _log_sigmoid(gates)
    b_incl = _chunk_scan(log_f, step)
    tail = _chunk_scan(log_f, step, reverse=True) - log_f
    g_row = tail + log_i
    r_val = log_i - b_incl

    project_queries(1)

    r_cummax = _chunk_scan(r_val, step, use_max=True)
    pos = lax.broadcasted_iota(jnp.int32, (nstep, step, LANES), 1)
    b_last = jnp.max(jnp.where(pos == step - 1, b_incl.reshape(nstep, step, LANES), NEG_INF), axis=1)
    g_max = jnp.max(g_row.reshape(nstep, step, LANES), axis=1)
    m_run = m_sc[...]
    m_starts = []
    for c in range(nstep):
        m_starts.append(m_run)
        m_run = jnp.maximum(b_last[c:c + 1, :] + m_run, g_max[c:c + 1, :])
    m_sc[...] = m_run
    m_start = jnp.concatenate(m_starts, axis=0)
    m_next = jnp.concatenate(m_starts[1:] + [m_run], axis=0)
    decay = jnp.exp(b_last + m_start - m_next)

    project_keys_values()

    def per_row(a):
        return jnp.broadcast_to(a[:, None, :], (nstep, step, LANES)).reshape(tm, LANES)

    m_rows = per_row(m_start)
    mx = jnp.maximum(m_rows, r_cummax)
    w_inter = jnp.exp(m_rows - mx)
    inv_floor = jnp.exp(-(b_incl + mx))
    w_state = jnp.exp(g_row - per_row(m_next))
    r_t = jnp.transpose(r_val)
    causal = (lax.broadcasted_iota(jnp.int32, (step, step), 0)
              >= lax.broadcasted_iota(jnp.int32, (step, step), 1))

    def spread_head(hd):
        ln = GATE_LANE + hd
        wi_sc[hd] = jnp.broadcast_to(w_inter[:, ln:ln + 1], (tm, LANES))
        em_sc[hd] = jnp.broadcast_to(inv_floor[:, ln:ln + 1], (tm, LANES))
        wk_sc[hd] = jnp.broadcast_to(w_state[:, ln:ln + 1], (tm, LANES))
        dec_sc[hd] = jnp.broadcast_to(decay[:, ln:ln + 1], (nstep, LANES))
        mx_b = jnp.broadcast_to(mx[:, ln:ln + 1], (tm, step))
        for c in range(nstep):
            r_row = r_t[ln:ln + 1, c * step:(c + 1) * step]
            d_sc[hd, c] = jnp.where(causal, jnp.exp(r_row - mx_b[c * step:(c + 1) * step, :]), 0.0)

    q_sc[...] = proj(0, M_WIDTH).astype(BF16)
    spread_head(0)
    kf = proj(M_WIDTH, 2 * M_WIDTH) * (M_HEAD_DIM ** -0.5)
    kf_sc[...] = kf
    kb_sc[...] = kf.astype(BF16)
    spread_head(1)
    vm = proj(2 * M_WIDTH, 3 * M_WIDTH).astype(BF16)
    for hd in range(M_HEADS):
        vaug_sc[hd, :, 0:M_HEAD_DIM] = vm[:, hd * M_HEAD_DIM:(hd + 1) * M_HEAD_DIM]
        vaug_sc[hd, :, M_HEAD_DIM:2 * M_HEAD_DIM] = jnp.ones((tm, M_HEAD_DIM), BF16)
    spread_head(2)
    spread_head(3)
    o_sc[...] = jax.nn.sigmoid(proj(3 * M_WIDTH, 4 * M_WIDTH))

    band_pos = lax.broadcasted_iota(jnp.int32, (1, band), 1)
    hcols = [slice(hd * M_HEAD_DIM, (hd + 1) * M_HEAD_DIM) for hd in range(M_HEADS)]

    def head_norm(hh_rows, gate_rows, mix_rows):
        hh = [hh_sc[hh_rows, hcols[hd]] for hd in range(M_HEADS)]
        mu = [jnp.mean(a, axis=-1, keepdims=True) for a in hh]
        dlt = [hh[hd] - mu[hd] for hd in range(M_HEADS)]
        var = [jnp.mean(a * a, axis=-1, keepdims=True) for a in dlt]
        for hd in range(M_HEADS):
            hn = dlt[hd] * lax.rsqrt(var[hd] + LN_EPS) * gnorm_ref[:, hcols[hd]]
            mix_sc[mix_rows, hcols[hd]] = (o_sc[gate_rows, hcols[hd]] * hn).astype(BF16)

    assert hh_sc.shape[0] == step + tm
    hh_sc[0:step, :] = jnp.zeros((step, M_WIDTH), F32)
    nt_dims = (((1,), (1,)), ((), ()))
    tn_dims = (((0,), (0,)), ((), ()))
    heads = range(M_HEADS)
    groups = range(A_KV_HEADS)
    subs = range(nsub)
    stack_order = [i for i in range(A_GROUP) if i % 2 == 0] + [i for i in range(A_GROUP) if i % 2 == 1]
    sink_cols = [jnp.concatenate([jnp.full((ch, 1), sinks_ref[A_GROUP * j + i], F32) for i in stack_order], axis=0)
                 for j in groups]

    def chunk_step(i):
        base_row = i * step
        r0 = [base_row + u * ch for u in subs]
        rows = [pl.ds(r0[u], ch) for u in subs]
        out_rows = [pl.ds(r0[u] + step, ch) for u in subs]
        cidx = [i * nsub + u for u in subs]

        mrows = pl.ds(base_row, step)
        qb = [q_sc[mrows, hcols[hd]] for hd in heads]
        vaug = [vaug_sc[hd, mrows, :] for hd in heads]
        kw = [(kf_sc[mrows, hcols[hd]] * wk_sc[hd, mrows, :]).astype(BF16) for hd in heads]
        upd = [lax.dot_general(kw[hd], vaug[hd], tn_dims, preferred_element_type=F32) for hd in heads]
        qk = [lax.dot_general(qb[hd], kb_sc[mrows, hcols[hd]], nt_dims, preferred_element_type=F32)
              for hd in heads]
        state = [cn_sc[hd] for hd in heads]
        inter = [jnp.dot(qb[hd], state[hd].astype(BF16), preferred_element_type=F32)
                 for hd in heads]

        kband = [[ka_sc[j, pl.ds(r0[u], band), :] for j in range(A_KV_HEADS)] for u in subs]
        sc = [[lax.dot_general(qa_sc[j, cidx[u]], kband[u][j], nt_dims, preferred_element_type=F32)
               for j in groups] for u in subs]
        if not has_cache:
            for u in subs:
                first_valid = WINDOW - (t * nch + cidx[u]) * ch
                mask_bias = jnp.where(band_pos >= first_valid, 0.0, NEG_INF)
                sc[u] = [a + mask_bias for a in sc[u]]

        head_norm(pl.ds(base_row, step), pl.ds(max(base_row - step, 0), step), pl.ds(base_row, step))

        s = [(qk[hd] * d_sc[hd, i]).astype(BF16) for hd in heads]
        mx_col = [[jnp.maximum(jnp.max(sc[u][j], axis=-1, keepdims=True), sink_cols[j]) for j in groups]
                  for u in subs]
        intra = [jnp.dot(s[hd], vaug[hd], preferred_element_type=F32) for hd in heads]
        mx_wide = [[jnp.broadcast_to(mx_col[u][j], (A_GROUP * ch, band)) for j in groups] for u in subs]
        pr = [[jnp.exp(sc[u][j] - mx_wide[u][j]).astype(BF16) for j in groups] for u in subs]
        sink_term = [[jnp.exp(sink_cols[j] - mx_wide[u][j][:, 0:LANES]) for j in groups] for u in subs]
        half = (A_GROUP // 2) * ch
        part = [[jnp.dot(pr[u][j][0:half], ve_sc[j, pl.ds(r0[u], band), :], preferred_element_type=F32)
                 + jnp.dot(pr[u][j][half:2 * half], vo_sc[j, pl.ds(r0[u], band), :], preferred_element_type=F32)
                 for j in groups] for u in subs]

        for hd in heads:
            wi = wi_sc[hd, mrows, :]
            num = intra[hd][:, 0:M_HEAD_DIM] + wi * inter[hd][:, 0:M_HEAD_DIM]
            den = intra[hd][:, M_HEAD_DIM:] + wi * inter[hd][:, M_HEAD_DIM:]
            hh_sc[pl.ds(base_row + step, step), hcols[hd]] = num / jnp.maximum(jnp.abs(den), em_sc[hd, mrows, :])
            dec = dec_sc[hd, pl.ds(i, 1), :]
            cn_sc[hd] = jnp.concatenate([dec, dec], axis=1) * state[hd] + upd[hd]
        for u in subs:
            for j in groups:
                acc = part[u][j]
                dn = acc[:, LANES:] + jnp.where(left, sink_term[u][j][0:half], sink_term[u][j][half:2 * half])
                out = (acc[:, 0:LANES] / dn).astype(BF16)
                for k in range(A_GROUP // 2):
                    p = j * (A_GROUP // 2) + k
                    mix_sc[out_rows[u], M_WIDTH + p * LANES:M_WIDTH + (p + 1) * LANES] = (
                        out[k * ch:(k + 1) * ch, :])

    for i in range(nstep):
        chunk_step(i)
    head_norm(slice(tm, tm + step), slice(tm - step, tm), slice(tm, tm + step))

    piece = min(tm, OUT_PIECE)
    mixed = [jnp.dot(mix_sc[step + lo:step + lo + piece, :], wout_ref[...], preferred_element_type=F32)
             for lo in range(0, tm, piece)]
    for i, lo in enumerate(range(0, tm, piece)):
        h_ref[0, lo:lo + piece, :] = _layer_norm(alpha * x_ref[0, lo:lo + piece, :] + mixed[i],
                                                 ln1g_ref[...], ln1b_ref[...])

    @pl.when(t == nt - 1)
    def _store_state():
        cnout_ref[0] = cn_sc[...]
        mout_ref[0] = m_sc[...]


def _ffn_body(h_ref, prev_ref, wup_ref, wconv_ref, bconv_ref, wdown_ref, ln2g_ref, ln2b_ref,
              y_ref, rows_ref, carry_sc, perm_sc, *, tm, nt, alpha, packed):
    t = pl.program_id(1)

    @pl.when(t == 0)
    def _load_carry():
        carry_sc[...] = prev_ref[0]

    nv = tm // SUBLANES
    pitch = nv + 1
    nl = D_MODEL // LANES
    for k in range(nl):
        for s in range(SUBLANES):
            perm_sc[k, s * pitch:s * pitch + nv, :] = h_ref[0, s * nv:(s + 1) * nv, k * LANES:(k + 1) * LANES]
    hf = jnp.concatenate(
        [jnp.concatenate([perm_sc[k, pl.ds(a, SUBLANES, stride=pitch), :] for k in range(nl)], axis=1)
         for a in range(nv)], axis=0)
    hb = hf.astype(BF16)
    nblk = D_FF // FF_BLOCK
    first_sublane = lax.broadcasted_iota(jnp.int32, (SUBLANES, FF_BLOCK), 0) == 0

    def up_project(blk, half):
        return jnp.dot(hb, wup_ref[:, half * D_FF + blk * FF_BLOCK:half * D_FF + (blk + 1) * FF_BLOCK],
                       preferred_element_type=F32)

    def conv(u, blk, half):
        cols = slice(half * D_FF + blk * FF_BLOCK, half * D_FF + (blk + 1) * FF_BLOCK)
        last1 = u[tm - SUBLANES:tm, :]
        last2 = u[tm - 2 * SUBLANES:tm - SUBLANES, :]
        if packed:
            lead2, lead1 = carry_sc[0, :, cols], carry_sc[1, :, cols]
            carry_sc[0, :, cols] = last2
            carry_sc[1, :, cols] = last1
        else:
            turned2, turned1 = pltpu.roll(last2, 1, 0), pltpu.roll(last1, 1, 0)
            lead2 = jnp.where(first_sublane, carry_sc[0, :, cols], turned2)
            lead1 = jnp.where(first_sublane, carry_sc[1, :, cols], turned1)
            carry_sc[0, :, cols] = turned2
            carry_sc[1, :, cols] = turned1
        back1 = jnp.concatenate([lead1, u[0:tm - SUBLANES, :]], axis=0)
        older = [u[0:tm - 2 * SUBLANES, :]] if nv > 2 else []
        back2 = jnp.concatenate([lead2, lead1] + older, axis=0)
        w = wconv_ref[:, cols]
        return bconv_ref[:, cols] + back2 * w[0:1, :] + back1 * w[1:2, :] + u * w[2:3, :]

    u_val, u_gate = up_project(0, 0), up_project(0, 1)
    acts = []
    for blk in range(nblk):
        more = blk + 1 < nblk
        if more:
            u_val_next = up_project(blk + 1, 0)
        val = conv(u_val, blk, 0)
        if more:
            u_gate_next = up_project(blk + 1, 1)
        gate = conv(u_gate, blk, 1)
        acts.append((gate * jax.nn.sigmoid(gate) * val).astype(BF16))
        if more:
            u_val, u_gate = u_val_next, u_gate_next
    act = jnp.concatenate(acts, axis=1)
    halves = [(0, tm // 2), (tm // 2, tm)] if nv % 2 == 0 and nv >= 2 * SUBLANES else [(0, tm)]
    for lo, hi in halves:
        total = jnp.dot(act[lo:hi, :], wdown_ref[...], preferred_element_type=F32)
        y = _layer_norm(alpha * hf[lo:hi, :] + total, ln2g_ref[...], ln2b_ref[...])
        for k in range(nl):
            for a in range(lo // SUBLANES, hi // SUBLANES):
                perm_sc[k, pl.ds(a, SUBLANES, stride=pitch), :] = (
                    y[a * SUBLANES - lo:(a + 1) * SUBLANES - lo, k * LANES:(k + 1) * LANES])
    for k in range(nl):
        for s in range(SUBLANES):
            y_ref[0, s * nv:(s + 1) * nv, k * LANES:(k + 1) * LANES] = perm_sc[k, s * pitch:s * pitch + nv, :]

    @pl.when(t == nt - 1)
    def _store_rows():
        rows_ref[0] = carry_sc[...]


def _const_spec(shape):
    zeros = (0,) * len(shape)
    return pl.BlockSpec(shape, lambda b, t: zeros, pipeline_mode=pl.Buffered(1))


def _batch_spec(shape):
    zeros = (0,) * (len(shape) - 1)
    return pl.BlockSpec(shape, lambda b, t: (b,) + zeros)


def _mixer_call(x, pos0, cn0, m0, kc, vc, wts, *, ch, tm, has_cache, alpha):
    bsz, seq, _ = x.shape
    nt = seq // tm
    nch = tm // ch
    rows_out = min(WINDOW, seq)
    assert seq % tm == 0 and tm % ch == 0 and rows_out <= tm and (nt == 1 or tm >= WINDOW)
    step = _chunks_per_step(nch) * ch

    half = A_HEAD_DIM // 2
    inv = ROPE_THETA ** (-np.arange(half, dtype=np.float64) / half)
    ang = (pos0 + np.arange(seq, dtype=np.float64))[:, None] * inv[None, :]
    cos, sin = np.cos(ang), np.sin(ang)
    cos_t = jnp.asarray(np.tile(cos, (1, LANES // half)), F32)
    sin_t = jnp.asarray(np.tile(np.concatenate([-sin, sin], axis=-1), (1, LANES // A_HEAD_DIM)), F32)

    body = functools.partial(_mixer_body, ch=ch, tm=tm, nt=nt, has_cache=has_cache, rows_out=rows_out,
                             alpha=alpha)
    tile3 = pl.BlockSpec((1, tm, D_MODEL), lambda b, t: (b, t, 0))
    state_shape = (1, M_HEADS, M_HEAD_DIM, 2 * M_HEAD_DIM)
    in_specs = [
        pl.BlockSpec(memory_space=pltpu.SMEM),
        tile3,
        pl.BlockSpec((tm, LANES), lambda b, t: (t, 0)),
        pl.BlockSpec((tm, LANES), lambda b, t: (t, 0)),
        _const_spec((D_MODEL, MAIN_WIDTH)),
        _const_spec((D_MODEL, LANES)),
        _const_spec((1, LANES)),
        _const_spec((1, M_WIDTH)),
        _const_spec((M_WIDTH + A_WIDTH, D_MODEL)),
        _const_spec((1, D_MODEL)),
        _const_spec((1, D_MODEL)),
        _batch_spec(state_shape),
        _batch_spec((1, 1, LANES)),
        _batch_spec((1, WINDOW, KV_WIDTH)),
        _batch_spec((1, WINDOW, KV_WIDTH)),
    ]
    out_specs = [
        tile3,
        _batch_spec(state_shape),
        _batch_spec((1, 1, LANES)),
        _batch_spec((1, rows_out, KV_WIDTH)),
        _batch_spec((1, rows_out, KV_WIDTH)),
    ]
    out_shape = [
        jax.ShapeDtypeStruct((bsz, seq, D_MODEL), F32),
        jax.ShapeDtypeStruct((bsz,) + state_shape[1:], F32),
        jax.ShapeDtypeStruct((bsz, 1, LANES), F32),
        jax.ShapeDtypeStruct((bsz, rows_out, KV_WIDTH), F32),
        jax.ShapeDtypeStruct((bsz, rows_out, KV_WIDTH), F32),
    ]
    scratch = [
        pltpu.VMEM((tm, M_WIDTH), BF16),
        pltpu.VMEM((tm, M_WIDTH), BF16),
        pltpu.VMEM((tm, M_WIDTH), F32),
        pltpu.VMEM((M_HEADS, tm, 2 * M_HEAD_DIM), BF16),
        pltpu.VMEM((tm, M_WIDTH), F32),
        pltpu.VMEM((A_KV_HEADS, nch, A_GROUP * ch, A_HEAD_DIM), BF16),
        pltpu.VMEM((A_KV_HEADS, WINDOW + tm, A_HEAD_DIM), BF16),
        pltpu.VMEM((A_KV_HEADS, WINDOW + tm, 2 * LANES), BF16),
        pltpu.VMEM((A_KV_HEADS, WINDOW + tm, 2 * LANES), BF16),
        pltpu.VMEM((M_HEADS, tm, LANES), F32),
        pltpu.VMEM((M_HEADS, tm, LANES), F32),
        pltpu.VMEM((M_HEADS, tm, LANES), F32),
        pltpu.VMEM((M_HEADS, tm // step, LANES), F32),
        pltpu.VMEM((M_HEADS, tm // step, step, step), F32),
        pltpu.VMEM((step + tm, M_WIDTH), F32),
        pltpu.VMEM((step + tm, M_WIDTH + A_WIDTH), BF16),
        pltpu.VMEM((M_HEADS, M_HEAD_DIM, 2 * M_HEAD_DIM), F32),
        pltpu.VMEM((1, LANES), F32),
    ]
    return pl.pallas_call(
        body,
        grid=(bsz, nt),
        in_specs=in_specs,
        out_specs=out_specs,
        out_shape=out_shape,
        scratch_shapes=scratch,
        compiler_params=pltpu.CompilerParams(dimension_semantics=("arbitrary", "arbitrary"),
                                             vmem_limit_bytes=VMEM_LIMIT_BYTES),
    )(wts["sinks"], x, cos_t, sin_t, wts["w_main"], wts["w_gate"], wts["gate_bias"], wts["g_norm"],
      wts["w_out"], wts["ln1_g"], wts["ln1_b"], cn0, m0, kc, vc)


def _ffn_call(h, conv_prev, wts, *, alpha):
    nbatch, nrows, _ = h.shape
    packed = nbatch == SUBLANES and nbatch * nrows <= FFN_ROWS and nrows >= CONV_W - 1
    if packed:
        h = h.reshape(1, nbatch * nrows, D_MODEL)
        prev = jnp.transpose(conv_prev, (1, 0, 2))[None]
    else:
        prev = jnp.broadcast_to(conv_prev[:, :, None, :], (nbatch, CONV_W - 1, SUBLANES, UP_WIDTH))
    bsz, seq, _ = h.shape
    tm = min(seq, FFN_ROWS)
    nt = seq // tm
    assert seq % tm == 0 and tm % (2 * SUBLANES) == 0
    body = functools.partial(_ffn_body, tm=tm, nt=nt, alpha=alpha, packed=packed)
    nl = D_MODEL // LANES
    tile3 = pl.BlockSpec((1, tm, D_MODEL), lambda b, t: (b, t, 0))
    carry_shape = (CONV_W - 1, SUBLANES, UP_WIDTH)
    y, rows = pl.pallas_call(
        body,
        grid=(bsz, nt),
        in_specs=[
            tile3,
            _batch_spec((1,) + carry_shape),
            _const_spec((D_MODEL, UP_WIDTH)),
            _const_spec((CONV_W, UP_WIDTH)),
            _const_spec((1, UP_WIDTH)),
            _const_spec((D_FF, D_MODEL)),
            _const_spec((1, D_MODEL)),
            _const_spec((1, D_MODEL)),
        ],
        out_specs=[tile3, _batch_spec((1,) + carry_shape)],
        out_shape=[jax.ShapeDtypeStruct((bsz, seq, D_MODEL), F32),
                   jax.ShapeDtypeStruct((bsz,) + carry_shape, F32)],
        scratch_shapes=[pltpu.VMEM(carry_shape, F32),
                        pltpu.VMEM((nl, SUBLANES * (tm // SUBLANES + 1), LANES), F32)],
        compiler_params=pltpu.CompilerParams(dimension_semantics=("arbitrary", "arbitrary"),
                                             vmem_limit_bytes=VMEM_LIMIT_BYTES),
    )(h, prev, wts["w_up"], wts["w_conv"], wts["b_conv"], wts["w_down"], wts["ln2_g"], wts["ln2_b"])
    if packed:
        return y.reshape(nbatch, nrows, D_MODEL), jnp.transpose(rows[0], (1, 0, 2))
    return y, rows[:, :, 0, :]


def _prep_weights(w_in, b_igate, b_fgate, g_mlstm_norm, attn_sinks, w_out, ln1_g, ln1_b,
                  w_up, w_conv, b_conv, w_down, ln2_g, ln2_b):
    gate_lo = 4 * M_WIDTH
    gate_hi = gate_lo + 2 * M_HEADS
    w_main = jnp.concatenate([w_in[:, :gate_lo], w_in[:, gate_hi:]], axis=1).astype(BF16)
    w_gate = jnp.pad(w_in[:, gate_lo:gate_hi], ((0, 0), (0, LANES - 2 * M_HEADS))).astype(BF16)
    gate_bias = jnp.pad(jnp.concatenate([b_igate, b_fgate]).astype(F32), (0, LANES - 2 * M_HEADS))
    row = lambda a: a.astype(F32).reshape(1, -1)
    return dict(
        sinks=attn_sinks.astype(F32), w_main=w_main, w_gate=w_gate, gate_bias=row(gate_bias),
        g_norm=row(g_mlstm_norm), w_out=w_out.astype(BF16), ln1_g=row(ln1_g), ln1_b=row(ln1_b),
        w_up=w_up.astype(BF16), w_conv=w_conv.astype(F32), b_conv=row(b_conv), w_down=w_down.astype(BF16),
        ln2_g=row(ln2_g), ln2_b=row(ln2_b))


def _stream_layer(x, pos0, c0, n0, m0, kc, vc, conv_prev, wts, *, ch, tm, has_cache, alpha):
    bsz, seq, _ = x.shape
    n_rep = jnp.broadcast_to(n0[..., None], (bsz, M_HEADS, M_HEAD_DIM, M_HEAD_DIM))
    cn0 = jnp.concatenate([c0, n_rep], axis=-1).astype(F32)
    m0 = jnp.pad(m0.astype(F32), ((0, 0), (GATE_LANE, LANES - GATE_LANE - M_HEADS))).reshape(bsz, 1, LANES)
    kc = kc.reshape(bsz, WINDOW, KV_WIDTH)
    vc = vc.reshape(bsz, WINDOW, KV_WIDTH)
    h, cn1, m1, k_rows, v_rows = _mixer_call(x, pos0, cn0, m0, kc, vc, wts, ch=ch, tm=tm,
                                             has_cache=has_cache, alpha=alpha)
    y, conv_rows = _ffn_call(h, conv_prev.astype(F32), wts, alpha=alpha)
    rows_out = k_rows.shape[1]
    state = (cn1[..., :M_HEAD_DIM], cn1[..., M_HEAD_DIM], m1[:, 0, GATE_LANE:GATE_LANE + M_HEADS],
             k_rows.reshape(bsz, rows_out, A_KV_HEADS, A_HEAD_DIM),
             v_rows.reshape(bsz, rows_out, A_KV_HEADS, A_HEAD_DIM),
             conv_rows)
    return y, state


def kernel(x_prompt, x_sample, state_mlstm_C, state_mlstm_n, state_mlstm_m, cache_swa_k, cache_swa_v,
           state_conv, w_in, b_igate, b_fgate, g_mlstm_norm, attn_sinks, w_out, ln1_g, ln1_b,
           w_up, w_conv, b_conv, w_down, ln2_g, ln2_b):
    depth = w_in.shape[0]
    alpha = (2 * depth) ** 0.25
    bp, tp = x_prompt.shape[0], x_prompt.shape[1]
    bs, ts = x_sample.shape[0], x_sample.shape[1]
    tm_p = min(tp, MIXER_ROWS)
    hp, hs = x_prompt, x_sample
    sp, ss = [], []
    for l in range(depth):
        wts = _prep_weights(w_in[l], b_igate[l], b_fgate[l], g_mlstm_norm[l], attn_sinks[l], w_out[l],
                            ln1_g[l], ln1_b[l], w_up[l], w_conv[l], b_conv[l], w_down[l], ln2_g[l], ln2_b[l])
        hp, st_p = _stream_layer(
            hp, 0,
            jnp.zeros((bp, M_HEADS, M_HEAD_DIM, M_HEAD_DIM), F32),
            jnp.zeros((bp, M_HEADS, M_HEAD_DIM), F32),
            jnp.zeros((bp, M_HEADS), F32),
            jnp.zeros((bp, WINDOW, A_KV_HEADS, A_HEAD_DIM), F32),
            jnp.zeros((bp, WINDOW, A_KV_HEADS, A_HEAD_DIM), F32),
            jnp.zeros((bp, CONV_W - 1, UP_WIDTH), F32),
            wts, ch=min(CHUNK, tp), tm=tm_p, has_cache=False, alpha=alpha)
        hs, st_s = _stream_layer(
            hs, PAST_LEN, state_mlstm_C[l], state_mlstm_n[l], state_mlstm_m[l],
            cache_swa_k[l], cache_swa_v[l], state_conv[l],
            wts, ch=ts, tm=ts, has_cache=True, alpha=alpha)
        sp.append(st_p)
        ss.append(st_s)
    P = [jnp.stack([s[i] for s in sp]) for i in range(6)]
    S = [jnp.stack([s[i] for s in ss]) for i in range(6)]
    return (hp, hs, P[0], P[1], P[2], P[3], P[4], P[5], S[0], S[1], S[2], S[3], S[4], S[5])
```
